```python
import math
import jax, jax.numpy as jnp
from jax import lax
import numpy as np

D_MODEL = 1024
BATCH = 16
SEQ = 2048
DEPTH = 2
DEC_BATCH = 8
DEC_SEQ = 16
PAST_LEN = 2048

CHUNK = 64
Q_BLOCK = 128
A_HEADS = 4
A_HALF_DIM = 64
A_QK_DIM = 2 * A_HALF_DIM
A_V_DIM = 2 * A_HALF_DIM
A_WIDTH = A_HEADS * A_V_DIM
LAMBDA_INIT_0 = 0.2
N_BUCKETS = 32
MAX_DISTANCE = 128
B_GROUPS = 4
B_GROUP_DIM = 128
B_WIDTH = B_GROUPS * B_GROUP_DIM
B_CHUNK = 128
L0_SIZES = [A_HEADS * A_QK_DIM, A_HEADS * A_QK_DIM, A_WIDTH, A_WIDTH, B_WIDTH, B_WIDTH, B_WIDTH]
L0_SPLITS = [int(s) for s in np.cumsum(L0_SIZES)[:-1]]
L0_IN = int(sum(L0_SIZES))
C_WIDTH = D_MODEL
CONV_WIDTH = 31
L1_IN = 3 * C_WIDTH
RMS_EPS = 1e-6
LN_EPS = 1e-5

kernel_name = "hybrid_stream_diffattn_gmlp_conformer_step"


def _rmsnorm(x, g):
    xf = x.astype(jnp.float32)
    y = xf * lax.rsqrt(jnp.mean(xf * xf, axis=-1, keepdims=True) + RMS_EPS)
    return (y * g.astype(jnp.float32)).astype(x.dtype)


def _layernorm(x, g, b):
    xf = x.astype(jnp.float32)
    mu = jnp.mean(xf, axis=-1, keepdims=True)
    var = jnp.mean(jnp.square(xf - mu), axis=-1, keepdims=True)
    y = (xf - mu) * lax.rsqrt(var + LN_EPS) * g.astype(jnp.float32) + b.astype(jnp.float32)
    return y.astype(x.dtype)


def _t5_bucket(rel):
    nb = N_BUCKETS // 2
    ret = jnp.where(rel > 0, nb, 0)
    n = jnp.abs(rel)
    max_exact = nb // 2
    nf = jnp.maximum(n, 1).astype(jnp.float32)
    large = max_exact + (jnp.log(nf / max_exact) / math.log(MAX_DISTANCE / max_exact)
                         * (nb - max_exact)).astype(jnp.int32)
    large = jnp.minimum(large, nb - 1)
    return ret + jnp.where(n < max_exact, n, large)


def _diff_attn(q, k, v, q_pos, k_pos, rel_bias, lam):
    s = jnp.einsum('bqhcd,bkhcd->bhcqk', q, k).astype(jnp.float32) * (A_HALF_DIM ** -0.5)
    bias = rel_bias[_t5_bucket(k_pos[None, :] - q_pos[:, None])]
    s = s + jnp.transpose(bias, (2, 0, 1)).astype(jnp.float32)[None, :, None]
    allowed = (k_pos[None, :] // CHUNK) <= (q_pos[:, None] // CHUNK)
    s = jnp.where(allowed[None, None, None], s, -jnp.inf)
    p = jax.nn.softmax(s, axis=-1)
    w = p[:, :, 0] - lam * p[:, :, 1]
    return jnp.einsum('bhqk,bkhe->bqhe', w.astype(v.dtype), v)


def _spatial_gate(v, w_s, b_s):
    bsz, t, g, c = v.shape
    n = -(-t // B_CHUNK)
    pad = n * B_CHUNK - t
    vp = jnp.pad(v, ((0, 0), (0, pad), (0, 0), (0, 0))).reshape(bsz, n, B_CHUNK, g, c)
    causal = jnp.tril(jnp.ones((B_CHUNK, B_CHUNK), dtype=bool))
    wm = jnp.where(causal[None], w_s, 0).astype(v.dtype)
    s = jnp.einsum('gij,bnjgc->bnigc', wm, vp) + b_s.T.astype(v.dtype)[None, None, :, :, None]
    return s.reshape(bsz, n * B_CHUNK, g, c)[:, :t]


def _attn_gmlp_layer(x, cache_k, cache_v, rel_bias, lam, norm_g, w_in, subln_g,
                     gv_ln_g, gv_ln_b, w_s, b_s, w_out):
    bsz, t, _ = x.shape
    xn = _rmsnorm(x, norm_g)
    z = jnp.einsum('btd,de->bte', xn, w_in)
    q, k, v, g_a, u_b, v_b, g_b = jnp.split(z, L0_SPLITS, axis=-1)
    q = q.reshape(bsz, t, A_HEADS, 2, A_HALF_DIM)
    k_new = k.reshape(bsz, t, A_HEADS, A_QK_DIM)
    v_new = v.reshape(bsz, t, A_HEADS, A_V_DIM)
    if cache_k is None:
        past = 0
        k_all, v_all = k_new, v_new
    else:
        past = cache_k.shape[1]
        k_all = jnp.concatenate([cache_k.astype(k_new.dtype), k_new], axis=1)
        v_all = jnp.concatenate([cache_v.astype(v_new.dtype), v_new], axis=1)
    q_pos = past + jnp.arange(t, dtype=jnp.int32)
    k_pos = jnp.arange(past + t, dtype=jnp.int32)
    k_all5 = k_all.reshape(bsz, past + t, A_HEADS, 2, A_HALF_DIM)
    if t % Q_BLOCK == 0:
        n_blk = t // Q_BLOCK
        qb = jnp.moveaxis(q.reshape(bsz, n_blk, Q_BLOCK, A_HEADS, 2, A_HALF_DIM), 1, 0)
        pb = q_pos.reshape(n_blk, Q_BLOCK)
        o = lax.map(lambda a: _diff_attn(a[0], k_all5, v_all, a[1], k_pos, rel_bias, lam), (qb, pb))
        o = jnp.moveaxis(o, 0, 1).reshape(bsz, t, A_HEADS, A_V_DIM)
    else:
        o = _diff_attn(q, k_all5, v_all, q_pos, k_pos, rel_bias, lam)
    o = _rmsnorm(o, subln_g) * (1.0 - LAMBDA_INIT_0)
    out_a = o.reshape(bsz, t, A_WIDTH) * jax.nn.silu(g_a)
    zu = jax.nn.gelu(u_b).reshape(bsz, t, B_GROUPS, B_GROUP_DIM)
    zv = _layernorm(jax.nn.gelu(v_b).reshape(bsz, t, B_GROUPS, B_GROUP_DIM), gv_ln_g, gv_ln_b)
    out_b = (zu * _spatial_gate(zv, w_s, b_s)).reshape(bsz, t, B_WIDTH) * jax.nn.silu(g_b)
    y = x + jnp.einsum('bte,ed->btd', jnp.concatenate([out_a, out_b], axis=-1), w_out)
    return y, k_new, v_new, zv.reshape(bsz, t, B_WIDTH)


def _conv_layer(x, conv_state, norm_g, w_in, w_dw, b_dw, ln_g, ln_b, w_out):
    xn = _rmsnorm(x, norm_g)
    z = jnp.einsum('btd,de->bte', xn, w_in)
    a, b, g = jnp.split(z, [C_WIDTH, 2 * C_WIDTH], axis=-1)
    u = a * jax.nn.sigmoid(b)
    if conv_state is None:
        ctx = jnp.pad(u, ((0, 0), (CONV_WIDTH - 1, 0), (0, 0)))
    else:
        ctx = jnp.concatenate([conv_state.astype(u.dtype), u], axis=1)
    c = lax.conv_general_dilated(ctx, w_dw.astype(ctx.dtype)[:, None, :], window_strides=(1,),
                                 padding='VALID', dimension_numbers=('NWC', 'WIO', 'NWC'),
                                 feature_group_count=C_WIDTH) + b_dw.astype(ctx.dtype)
    c = jax.nn.silu(_layernorm(c, ln_g, ln_b)) * jax.nn.silu(g)
    y = x + jnp.einsum('btc,cd->btd', c, w_out)
    return y, ctx[:, -(CONV_WIDTH - 1):]


def setup_inputs(seed: int = 0) -> dict:
    key = jax.random.key(seed)
    ks = jax.random.split(key, 26)

    def nrm(k, shape, s):
        return jax.random.normal(k, shape, jnp.float32) * s

    return {
        "x_prompt": nrm(ks[0], (BATCH, SEQ, D_MODEL), 1.0),
        "x_sample": nrm(ks[1], (DEC_BATCH, DEC_SEQ, D_MODEL), 1.0),
        "cache_k0": nrm(ks[2], (DEC_BATCH, PAST_LEN, A_HEADS, A_QK_DIM), 1.0),
        "cache_v0": nrm(ks[3], (DEC_BATCH, PAST_LEN, A_HEADS, A_V_DIM), 1.0),
        "state_conv1": nrm(ks[4], (DEC_BATCH, CONV_WIDTH - 1, C_WIDTH), 0.5),
        "rel_bias": nrm(ks[5], (N_BUCKETS, A_HEADS), 0.5),
        "norm_g0": 1.0 + nrm(ks[6], (D_MODEL,), 0.02),
        "w_in0": nrm(ks[7], (D_MODEL, L0_IN), D_MODEL ** -0.5),
        "lambda_q1": nrm(ks[8], (A_HALF_DIM,), 0.1),
        "lambda_k1": nrm(ks[9], (A_HALF_DIM,), 0.1),
        "lambda_q2": nrm(ks[10], (A_HALF_DIM,), 0.1),
        "lambda_k2": nrm(ks[11], (A_HALF_DIM,), 0.1),
        "subln_g0": 1.0 + nrm(ks[12], (A_V_DIM,), 0.02),
        "gv_ln_g0": 1.0 + nrm(ks[13], (B_GROUPS, B_GROUP_DIM), 0.02),
        "gv_ln_b0": nrm(ks[14], (B_GROUPS, B_GROUP_DIM), 0.02),
        "w_s0": nrm(ks[15], (B_GROUPS, B_CHUNK, B_CHUNK), B_CHUNK ** -0.5),
        "b_s0": 1.0 + nrm(ks[16], (B_GROUPS, B_CHUNK), 0.1),
        "w_out0": nrm(ks[17], (A_WIDTH + B_WIDTH, D_MODEL), (A_WIDTH + B_WIDTH) ** -0.5),
        "norm_g1": 1.0 + nrm(ks[18], (D_MODEL,), 0.02),
        "w_in1": nrm(ks[19], (D_MODEL, L1_IN), D_MODEL ** -0.5),
        "w_dw1": nrm(ks[20], (CONV_WIDTH, C_WIDTH), CONV_WIDTH ** -0.5),
        "b_dw1": nrm(ks[21], (C_WIDTH,), 0.02),
        "conv_ln_g1": 1.0 + nrm(ks[22], (C_WIDTH,), 0.02),
        "conv_ln_b1": nrm(ks[23], (C_WIDTH,), 0.02),
        "w_out1": nrm(ks[24], (C_WIDTH, D_MODEL), C_WIDTH ** -0.5),
        "final_g": 1.0 + nrm(ks[25], (D_MODEL,), 0.02),
    }


def reference(x_prompt, x_sample, cache_k0, cache_v0, state_conv1, rel_bias, norm_g0, w_in0,
              lambda_q1, lambda_k1, lambda_q2, lambda_k2, subln_g0, gv_ln_g0, gv_ln_b0, w_s0, b_s0,
              w_out0, norm_g1, w_in1, w_dw1, b_dw1, conv_ln_g1, conv_ln_b1, w_out1, final_g):
    f32 = jnp.float32
    lam = (jnp.exp(jnp.sum(lambda_q1.astype(f32) * lambda_k1.astype(f32)))
           - jnp.exp(jnp.sum(lambda_q2.astype(f32) * lambda_k2.astype(f32))) + LAMBDA_INIT_0)
    xp, xs = x_prompt, x_sample
    for layer in range(DEPTH):
        if layer % 2 == 0:
            xp, k0p, v0p, _ = _attn_gmlp_layer(xp, None, None, rel_bias, lam, norm_g0, w_in0,
                                               subln_g0, gv_ln_g0, gv_ln_b0, w_s0, b_s0, w_out0)
            xs, k0s, v0s, gv0s = _attn_gmlp_layer(xs, cache_k0, cache_v0, rel_bias, lam, norm_g0,
                                                  w_in0, subln_g0, gv_ln_g0, gv_ln_b0, w_s0, b_s0,
                                                  w_out0)
        else:
            xp, c1p = _conv_layer(xp, None, norm_g1, w_in1, w_dw1, b_dw1, conv_ln_g1, conv_ln_b1, w_out1)
            xs, c1s = _conv_layer(xs, state_conv1, norm_g1, w_in1, w_dw1, b_dw1, conv_ln_g1,
                                  conv_ln_b1, w_out1)
    y_prompt = _rmsnorm(xp, final_g)
    y_sample = _rmsnorm(xs, final_g)
    return (y_prompt, y_sample, k0p, v0p, c1p, k0s, v0s, gv0s, c1s)
```

```python
import functools
import math

import numpy as np
import jax
import jax.numpy as jnp
from jax import lax
from jax.experimental import pallas as pl
from jax.experimental.pallas import tpu as pltpu

F32 = jnp.float32
BF16 = jnp.bfloat16

CHUNK = 64
HEADS = 4
HALF = 64
HEAD_DIM = 2 * HALF
A_WIDTH = HEADS * HEAD_DIM
LAMBDA_INIT = 0.2
N_BUCKETS = 32
MAX_DISTANCE = 128
GROUPS = 4
GROUP_DIM = 128
B_WIDTH = GROUPS * GROUP_DIM
B_CHUNK = 128
CONV_WIDTH = 31
RMS_EPS = 1e-6
LN_EPS = 1e-5

Q0, K0, V0, GA0, UB0, VB0, GB0 = (i * 512 for i in range(7))
L0_IN = 7 * 512

TM = 256
CONV_PAD = 32
CONV_OFF = CONV_PAD - (CONV_WIDTH - 1)
CONV_RB = 32
NEG = -1e30
VMEM_LIMIT = 56 * 1024 * 1024


def _np_bucket(rel):
    nb = N_BUCKETS // 2
    ret = np.where(rel > 0, nb, 0)
    n = np.abs(rel)
    max_exact = nb // 2
    nf = np.maximum(n, 1).astype(np.float32)
    large = max_exact + (np.log(nf / np.float32(max_exact)) / np.float32(math.log(MAX_DISTANCE / max_exact))
                         * np.float32(nb - max_exact)).astype(np.int32)
    large = np.minimum(large, nb - 1)
    return (ret + np.where(n < max_exact, n, large)).astype(np.int32)


FAR_BUCKET = N_BUCKETS // 2 - 1


def _rms(x, g):
    return x * lax.rsqrt(jnp.mean(x * x, axis=-1, keepdims=True) + RMS_EPS) * g


def _ln(x, g, b):
    mu = jnp.mean(x, axis=-1, keepdims=True)
    xc = x - mu
    var = jnp.mean(xc * xc, axis=-1, keepdims=True)
    return xc * lax.rsqrt(var + LN_EPS) * g + b


def _lam(lamv):
    s1 = jnp.sum(lamv[0:1] * lamv[1:2], axis=-1, keepdims=True)
    s2 = jnp.sum(lamv[2:3] * lamv[3:4], axis=-1, keepdims=True)
    return jnp.exp(s1) - jnp.exp(s2) + LAMBDA_INIT


def _dot(a, b):
    return jnp.dot(a, b, preferred_element_type=F32)


def _dot_nt(a, b):
    return lax.dot_general(a, b, (((1,), (1,)), ((), ())), preferred_element_type=F32)


def _l0_prompt_kernel(lamv_ref, x_ref, g0_ref, win_ref, bias_ref, subg_ref, lng_ref, lnb_ref, wm_ref, bs_ref,
                      wout_ref, y_ref, k_ref, v_ref,
                      z_ref, kbf_ref, vt_ref, qm_ref, m_ref, l_ref, acc_ref, mix_ref):
    i = pl.program_id(1)
    x = x_ref[0]
    z_ref[...] = _dot(_rms(x, g0_ref[...]).astype(BF16), win_ref[...])

    k = z_ref[:, K0:K0 + A_WIDTH]
    v = z_ref[:, V0:V0 + A_WIDTH]
    k_ref[0] = k
    v_ref[0] = v
    kbf_ref[i] = k.astype(BF16)
    vt_ref[i] = v.T.astype(BF16)

    lo = lax.broadcasted_iota(jnp.int32, (TM, HEAD_DIM), 1) < HALF
    for h in range(HEADS):
        q = z_ref[:, Q0 + h * HEAD_DIM:Q0 + (h + 1) * HEAD_DIM] * (HALF ** -0.5)
        qm_ref[h, 0:TM, :] = jnp.where(lo, q, 0.0).astype(BF16)
        qm_ref[h, TM:2 * TM, :] = jnp.where(lo, 0.0, q).astype(BF16)

    def attend(j, bias_idx, first):
        for h in range(HEADS):
            kt = kbf_ref[j, :, h * HEAD_DIM:(h + 1) * HEAD_DIM]
            vt = vt_ref[j, h * HEAD_DIM:(h + 1) * HEAD_DIM, :]
            s = _dot_nt(kt, qm_ref[h])
            if bias_idx is not None:
                s = s + bias_ref[h, bias_idx]
            mt = jnp.max(s, axis=0, keepdims=True)
            if first:
                p = jnp.exp(s - mt)
                m_ref[h] = mt
                l_ref[h] = jnp.sum(p, axis=0, keepdims=True)
                acc_ref[h] = _dot(vt, p.astype(BF16))
            else:
                m_old = m_ref[h]
                m_new = jnp.maximum(m_old, mt)
                alpha = jnp.exp(m_old - m_new)
                p = jnp.exp(s - m_new)
                m_ref[h] = m_new
                l_ref[h] = alpha * l_ref[h] + jnp.sum(p, axis=0, keepdims=True)
                acc_ref[h] = alpha * acc_ref[h] + _dot(vt, p.astype(BF16))

    attend(i, 0, True)

    @pl.when(i > 0)
    def _():
        attend(i - 1, 1, False)

    def far(j, carry):
        attend(j, None, False)
        return carry

    lax.fori_loop(0, i - 1, far, 0)

    lam = _lam(lamv_ref[...])
    for h in range(HEADS):
        on = acc_ref[h] / l_ref[h]
        o = (on[:, :TM] - lam * on[:, TM:]).T
        o = _rms(o, subg_ref[...]) * (1.0 - LAMBDA_INIT)
        ga = z_ref[:, GA0 + h * HEAD_DIM:GA0 + (h + 1) * HEAD_DIM]
        mix_ref[:, h * HEAD_DIM:(h + 1) * HEAD_DIM] = (o * jax.nn.silu(ga)).astype(BF16)

    for g in range(GROUPS):
        sl = slice(g * GROUP_DIM, (g + 1) * GROUP_DIM)
        ub = z_ref[:, UB0 + g * GROUP_DIM:UB0 + (g + 1) * GROUP_DIM]
        vb = z_ref[:, VB0 + g * GROUP_DIM:VB0 + (g + 1) * GROUP_DIM]
        gb = z_ref[:, GB0 + g * GROUP_DIM:GB0 + (g + 1) * GROUP_DIM]
        zv = _ln(jax.nn.gelu(vb), lng_ref[:, sl], lnb_ref[:, sl]).astype(BF16)
        zu = jax.nn.gelu(ub)
        gate = jax.nn.silu(gb)
        for c in range(TM // B_CHUNK):
            rows = slice(c * B_CHUNK, (c + 1) * B_CHUNK)
            sg = _dot(wm_ref[g], zv[rows]) + bs_ref[:, sl]
            mix_ref[rows, A_WIDTH + g * GROUP_DIM:A_WIDTH + (g + 1) * GROUP_DIM] = (
                zu[rows] * sg * gate[rows]).astype(BF16)

    y_ref[0] = x + _dot(mix_ref[...], wout_ref[...])


def _l0_prompt(x, lamv, g0, win, bias, subg, lng, lnb, wm, bs, wout):
    bsz, t, d = x.shape
    nt = t // TM
    const = lambda shape: pl.BlockSpec(shape, lambda b, i: (0,) * len(shape))
    row = lambda w: pl.BlockSpec((1, TM, w), lambda b, i: (b, i, 0))
    return pl.pallas_call(
        _l0_prompt_kernel,
        grid=(bsz, nt),
        in_specs=[const((4, HALF)), row(d), const((1, d)), const((d, L0_IN)),
                  const((HEADS, 2, TM, 2 * TM)), const((1, HEAD_DIM)), const((1, B_WIDTH)), const((1, B_WIDTH)),
                  const((GROUPS, B_CHUNK, B_CHUNK)), const((B_CHUNK, B_WIDTH)), const((A_WIDTH + B_WIDTH, d))],
        out_specs=[row(d), row(A_WIDTH), row(A_WIDTH)],
        out_shape=[jax.ShapeDtypeStruct((bsz, t, d), F32),
                   jax.ShapeDtypeStruct((bsz, t, A_WIDTH), F32),
                   jax.ShapeDtypeStruct((bsz, t, A_WIDTH), F32)],
        scratch_shapes=[pltpu.VMEM((TM, L0_IN), F32),
                        pltpu.VMEM((nt, TM, A_WIDTH), BF16),
                        pltpu.VMEM((nt, A_WIDTH, TM), BF16),
                        pltpu.VMEM((HEADS, 2 * TM, HEAD_DIM), BF16),
                        pltpu.VMEM((HEADS, 1, 2 * TM), F32),
                        pltpu.VMEM((HEADS, 1, 2 * TM), F32),
                        pltpu.VMEM((HEADS, HEAD_DIM, 2 * TM), F32),
                        pltpu.VMEM((TM, A_WIDTH + B_WIDTH), BF16)],
        compiler_params=pltpu.CompilerParams(dimension_semantics=("arbitrary", "arbitrary"),
                                             vmem_limit_bytes=VMEM_LIMIT),
        name="l0_prompt",
    )(lamv, x, g0, win, bias, subg, lng, lnb, wm, bs, wout)


def _conv_rows(buf_ref, base, rows, wdw_ref, bdw_ref):
    acc = jnp.broadcast_to(bdw_ref[...], (rows, bdw_ref.shape[-1]))
    for tap in range(CONV_WIDTH):
        acc = acc + wdw_ref[tap:tap + 1, :] * buf_ref[pl.ds(base + (tap + CONV_OFF), rows), :]
    return acc


def _l1_prompt_kernel(x_ref, g1_ref, win_ref, wdw_ref, bdw_ref, clng_ref, clnb_ref, wout_ref, fg_ref,
                      y_ref, tail_ref, z_ref, buf_ref, conv_ref):
    i = pl.program_id(1)
    d = x_ref.shape[-1]

    @pl.when(i == 0)
    def _():
        buf_ref[0:CONV_PAD, :] = jnp.zeros((CONV_PAD, d), F32)

    x = x_ref[0]
    z_ref[...] = _dot(_rms(x, g1_ref[...]).astype(BF16), win_ref[...])
    buf_ref[CONV_PAD:CONV_PAD + TM, :] = z_ref[:, 0:d] * jax.nn.sigmoid(z_ref[:, d:2 * d])

    for r in range(TM // CONV_RB):
        conv_ref[r * CONV_RB:(r + 1) * CONV_RB, :] = _conv_rows(buf_ref, r * CONV_RB, CONV_RB, wdw_ref, bdw_ref)

    c = jax.nn.silu(_ln(conv_ref[...], clng_ref[...], clnb_ref[...])) * jax.nn.silu(z_ref[:, 2 * d:3 * d])
    y1 = x + _dot(c.astype(BF16), wout_ref[...])
    y_ref[0] = _rms(y1, fg_ref[...])

    tail = buf_ref[TM:TM + CONV_PAD, :]
    tail_ref[0] = tail
    buf_ref[0:CONV_PAD, :] = tail


def _l1_prompt(x, g1, win, wdw, bdw, clng, clnb, wout, fg):
    bsz, t, d = x.shape
    nt = t // TM
    const = lambda shape: pl.BlockSpec(shape, lambda b, i: (0,) * len(shape))
    row = pl.BlockSpec((1, TM, d), lambda b, i: (b, i, 0))
    return pl.pallas_call(
        _l1_prompt_kernel,
        grid=(bsz, nt),
        in_specs=[row, const((1, d)), const((d, 3 * d)), const((CONV_WIDTH, d)), const((1, d)),
                  const((1, d)), const((1, d)), const((d, d)), const((1, d))],
        out_specs=[row, pl.BlockSpec((1, CONV_PAD, d), lambda b, i: (b, 0, 0))],
        out_shape=[jax.ShapeDtypeStruct((bsz, t, d), F32),
                   jax.ShapeDtypeStruct((bsz, CONV_PAD, d), F32)],
        scratch_shapes=[pltpu.VMEM((TM, 3 * d), F32),
                        pltpu.VMEM((CONV_PAD + TM, d), F32),
                        pltpu.VMEM((TM, d), F32)],
        compiler_params=pltpu.CompilerParams(dimension_semantics=("arbitrary", "arbitrary"),
                                             vmem_limit_bytes=VMEM_LIMIT),
        name="l1_prompt",
    )(x, g1, win, wdw, bdw, clng, clnb, wout, fg)


def _s_proj_kernel(x_ref, g0_ref, win_ref, z_ref):
    z_ref[...] = _dot(_rms(x_ref[...], g0_ref[...]).astype(BF16), win_ref[...])


def _s_attn_kernel(lamv_ref, z_ref, ck_ref, cv_ref, bias_ref, o_ref, *, dec_seq, pad_rows):
    ncol = HEADS * 2 * dec_seq
    q = z_ref[:, Q0:Q0 + A_WIDTH] * (HALF ** -0.5)
    q_rep = jnp.concatenate([q] * (2 * HEADS), axis=0)
    row = lax.broadcasted_iota(jnp.int32, (ncol, A_WIDTH), 0)
    col = lax.broadcasted_iota(jnp.int32, (ncol, A_WIDTH), 1)
    q_bd = jnp.where(col // HALF == row // dec_seq, q_rep, 0.0).astype(BF16)

    zeros = jnp.zeros((pad_rows - dec_seq, A_WIDTH), F32)
    k_all = jnp.concatenate([ck_ref[0], z_ref[:, K0:K0 + A_WIDTH], zeros], axis=0).astype(BF16)
    v_all = jnp.concatenate([cv_ref[0], z_ref[:, V0:V0 + A_WIDTH], zeros], axis=0).astype(BF16)

    s = _dot_nt(k_all, q_bd) + bias_ref[...]
    p = jnp.exp(s - jnp.max(s, axis=0, keepdims=True))
    p = p / jnp.sum(p, axis=0, keepdims=True)
    w = p - _lam(lamv_ref[...]) * pltpu.roll(p, ncol - dec_seq, axis=1)
    o_all = _dot(w.T.astype(BF16), v_all)
    for h in range(HEADS):
        o_ref[:, h * HEAD_DIM:(h + 1) * HEAD_DIM] = o_all[2 * h * dec_seq:(2 * h + 1) * dec_seq,
                                                          h * HEAD_DIM:(h + 1) * HEAD_DIM]


def _s_rest_kernel(x_ref, z_ref, o_ref, subg_ref, lng_ref, lnb_ref, wbd_ref, bs_ref, wout0_ref,
                   g1_ref, win1_ref, st_ref, wdw_ref, bdw_ref, clng_ref, clnb_ref, wout1_ref, fg_ref,
                   y_ref, zv_ref, u_ref, mix_ref, buf_ref, conv_ref, *, dec_seq):
    d = x_ref.shape[-1]
    nb = st_ref.shape[0]
    for h in range(HEADS):
        sl = slice(h * HEAD_DIM, (h + 1) * HEAD_DIM)
        o = _rms(o_ref[:, sl], subg_ref[...]) * (1.0 - LAMBDA_INIT)
        mix_ref[:, sl] = (o * jax.nn.silu(z_ref[:, GA0 + h * HEAD_DIM:GA0 + (h + 1) * HEAD_DIM])).astype(BF16)
    for g in range(GROUPS):
        sl = slice(g * GROUP_DIM, (g + 1) * GROUP_DIM)
        ub = z_ref[:, UB0 + g * GROUP_DIM:UB0 + (g + 1) * GROUP_DIM]
        vb = z_ref[:, VB0 + g * GROUP_DIM:VB0 + (g + 1) * GROUP_DIM]
        gb = z_ref[:, GB0 + g * GROUP_DIM:GB0 + (g + 1) * GROUP_DIM]
        zv = _ln(jax.nn.gelu(vb), lng_ref[:, sl], lnb_ref[:, sl])
        zv_ref[:, sl] = zv
        sg = _dot(wbd_ref[g], zv.astype(BF16)) + bs_ref[:, sl]
        mix_ref[:, A_WIDTH + g * GROUP_DIM:A_WIDTH + (g + 1) * GROUP_DIM] = (
            jax.nn.gelu(ub) * sg * jax.nn.silu(gb)).astype(BF16)
    y0 = x_ref[...] + _dot(mix_ref[...], wout0_ref[...])

    z1 = _dot(_rms(y0, g1_ref[...]).astype(BF16), win1_ref[...])
    u = z1[:, 0:d] * jax.nn.sigmoid(z1[:, d:2 * d])
    u_ref[...] = u
    for b in range(nb):
        buf_ref[b, 0:CONV_PAD, :] = st_ref[b]
        buf_ref[b, CONV_PAD:CONV_PAD + dec_seq, :] = u[b * dec_seq:(b + 1) * dec_seq]
    for b in range(nb):
        conv_ref[b * dec_seq:(b + 1) * dec_seq, :] = _conv_rows(buf_ref.at[b], 0, dec_seq, wdw_ref, bdw_ref)
    c = jax.nn.silu(_ln(conv_ref[...], clng_ref[...], clnb_ref[...])) * jax.nn.silu(z1[:, 2 * d:3 * d])
    y1 = y0 + _dot(c.astype(BF16), wout1_ref[...])
    y_ref[...] = _rms(y1, fg_ref[...])


def _sample(xs, cache_k, cache_v, state, lamv, g0, win0, bias_s, subg, lng, lnb, wbd, bs_s, wout0,
            g1, win1, wdw, bdw, clng, clnb, wout1, fg):
    nb, dec_seq, d = xs.shape
    rows = nb * dec_seq
    past = cache_k.shape[1]
    pad_rows = bias_s.shape[0] - past
    x2 = xs.reshape(rows, d)
    z = pl.pallas_call(_s_proj_kernel, out_shape=jax.ShapeDtypeStruct((rows, L0_IN), F32),
                       compiler_params=pltpu.CompilerParams(vmem_limit_bytes=VMEM_LIMIT),
                       name="s_proj")(x2, g0, win0)

    ncol = HEADS * 2 * dec_seq
    o = pl.pallas_call(
        functools.partial(_s_attn_kernel, dec_seq=dec_seq, pad_rows=pad_rows),
        grid=(nb,),
        in_specs=[pl.BlockSpec((4, HALF), lambda b: (0, 0)),
                  pl.BlockSpec((dec_seq, L0_IN), lambda b: (b, 0)),
                  pl.BlockSpec((1, past, A_WIDTH), lambda b: (b, 0, 0)),
                  pl.BlockSpec((1, past, A_WIDTH), lambda b: (b, 0, 0)),
                  pl.BlockSpec((past + pad_rows, ncol), lambda b: (0, 0))],
        out_specs=pl.BlockSpec((dec_seq, A_WIDTH), lambda b: (b, 0)),
        out_shape=jax.ShapeDtypeStruct((rows, A_WIDTH), F32),
        compiler_params=pltpu.CompilerParams(dimension_semantics=("arbitrary",), vmem_limit_bytes=VMEM_LIMIT),
        name="s_attn",
    )(lamv, z, cache_k.reshape(nb, past, A_WIDTH), cache_v.reshape(nb, past, A_WIDTH), bias_s)

    state_pad = jnp.pad(state, ((0, 0), (CONV_OFF, 0), (0, 0)))
    y, zv, u = pl.pallas_call(
        functools.partial(_s_rest_kernel, dec_seq=dec_seq),
        out_shape=[jax.ShapeDtypeStruct((rows, d), F32),
                   jax.ShapeDtypeStruct((rows, B_WIDTH), F32),
                   jax.ShapeDtypeStruct((rows, d), F32)],
        scratch_shapes=[pltpu.VMEM((rows, A_WIDTH + B_WIDTH), BF16),
                        pltpu.VMEM((nb, CONV_PAD + dec_seq, d), F32),
                        pltpu.VMEM((rows, d), F32)],
        compiler_params=pltpu.CompilerParams(vmem_limit_bytes=VMEM_LIMIT),
        name="s_rest",
    )(x2, z, o, subg, lng, lnb, wbd, bs_s, wout0, g1, win1, state_pad, wdw, bdw, clng, clnb, wout1, fg)
    return z, y, zv, u


def kernel(x_prompt, x_sample, cache_k0, cache_v0, state_conv1, rel_bias, norm_g0, w_in0, lambda_q1, lambda_k1,
           lambda_q2, lambda_k2, subln_g0, gv_ln_g0, gv_ln_b0, w_s0, b_s0, w_out0, norm_g1, w_in1, w_dw1, b_dw1,
           conv_ln_g1, conv_ln_b1, w_out1, final_g):
    bsz, t, d = x_prompt.shape
    nb, dec_seq, _ = x_sample.shape
    past = cache_k0.shape[1]
    assert t % TM == 0 and TM % B_CHUNK == 0 and TM % CHUNK == 0 and TM >= MAX_DISTANCE
    assert dec_seq <= B_CHUNK and dec_seq % 8 == 0 and HEADS * 2 * dec_seq == 128
    assert past % CHUNK == 0 and (past + dec_seq - 1) // CHUNK == past // CHUNK

    row = lambda a: a.reshape(1, -1).astype(F32)
    lamv = jnp.stack([lambda_q1, lambda_k1, lambda_q2, lambda_k2]).astype(F32)
    win0, wout0 = w_in0.astype(BF16), w_out0.astype(BF16)
    win1, wout1 = w_in1.astype(BF16), w_out1.astype(BF16)
    g0, g1, fg, subg = row(norm_g0), row(norm_g1), row(final_g), row(subln_g0)
    lng, lnb = row(gv_ln_g0), row(gv_ln_b0)
    bdw, clng, clnb = row(b_dw1), row(conv_ln_g1), row(conv_ln_b1)
    wdw = w_dw1.astype(F32)

    rbs = (rel_bias - rel_bias[FAR_BUCKET:FAR_BUCKET + 1]).astype(F32)
    assert np.all(_np_bucket(-np.arange(TM + 1, 2 * max(t, past + dec_seq))) == FAR_BUCKET)
    kk = np.arange(TM)[:, None]
    qq = np.arange(TM)[None, :]
    own = jnp.where((kk // CHUNK <= qq // CHUNK)[..., None], rbs[_np_bucket(kk - qq)], NEG)
    prev = rbs[_np_bucket(kk - qq - TM)]
    bias_p = jnp.transpose(jnp.stack([own, prev]), (3, 0, 1, 2))
    bias_p = jnp.concatenate([bias_p, bias_p], axis=-1)

    tril = np.tril(np.ones((B_CHUNK, B_CHUNK), dtype=bool))
    wm = jnp.where(tril[None], w_s0, 0)
    bs = jnp.repeat(b_s0.T.astype(F32), GROUP_DIM, axis=1)

    y0p, k0p, v0p = _l0_prompt(x_prompt, lamv, g0, win0, bias_p, subg, lng, lnb, wm.astype(BF16), bs, wout0)
    y_prompt, tail_p = _l1_prompt(y0p, g1, win1, wdw, bdw, clng, clnb, wout1, fg)

    pad_rows = 128
    kpos = np.arange(past + dec_seq)[:, None]
    qpos = past + np.arange(dec_seq)[None, :]
    bsmp = jnp.transpose(rbs[_np_bucket(kpos - qpos)], (0, 2, 1))
    bsmp = jnp.broadcast_to(bsmp[:, :, None, :], (past + dec_seq, HEADS, 2, dec_seq)).reshape(past + dec_seq, -1)
    bias_s = jnp.pad(bsmp, ((0, pad_rows - dec_seq), (0, 0)), constant_values=NEG)
    wbd = jnp.stack([jnp.kron(jnp.eye(nb, dtype=F32), wm[g, :dec_seq, :dec_seq]) for g in range(GROUPS)])
    bs_s = jnp.tile(bs[:dec_seq], (nb, 1))

    zs, ys, zvs, us = _sample(x_sample, cache_k0, cache_v0, state_conv1, lamv, g0, win0, bias_s, subg, lng, lnb,
                              wbd.astype(BF16), bs_s, wout0, g1, win1, wdw, bdw, clng, clnb, wout1, fg)

    keep = CONV_WIDTH - 1
    return (y_prompt,
            ys.reshape(nb, dec_seq, d),
            k0p.reshape(bsz, t, HEADS, HEAD_DIM),
            v0p.reshape(bsz, t, HEADS, HEAD_DIM),
            tail_p[:, CONV_OFF:],
            zs[:, K0:K0 + A_WIDTH].reshape(nb, dec_seq, HEADS, HEAD_DIM),
            zs[:, V0:V0 + A_WIDTH].reshape(nb, dec_seq, HEADS, HEAD_DIM),
            zvs.reshape(nb, dec_seq, B_WIDTH),
            jnp.concatenate([state_conv1, us.reshape(nb, dec_seq, d)], axis=1)[:, -keep:])
```

```python
import functools
import math

import numpy as np
import jax
import jax.numpy as jnp
from jax import lax
from jax.experimental import pallas as pl
from jax.experimental.pallas import tpu as pltpu

F32 = jnp.float32
BF16 = jnp.bfloat16

CHUNK = 64
HEADS = 4
HALF = 64
HEAD_DIM = 2 * HALF
A_WIDTH = HEADS * HEAD_DIM
LAMBDA_INIT = 0.2
N_BUCKETS = 32
MAX_DISTANCE = 128
GROUPS = 4
GROUP_DIM = 128
B_WIDTH = GROUPS * GROUP_DIM
B_CHUNK = 128
CONV_WIDTH = 31
RMS_EPS = 1e-6
LN_EPS = 1e-5

Q0, K0, V0, GA0, UB0, VB0, GB0 = (i * 512 for i in range(7))
L0_IN = 7 * 512

TM = 256
CONV_PAD = 32
CONV_OFF = CONV_PAD - (CONV_WIDTH - 1)
CONV_RB = 64
NEG = -1e30
VMEM_LIMIT = 56 * 1024 * 1024


def _np_bucket(rel):
    nb = N_BUCKETS // 2
    ret = np.where(rel > 0, nb, 0)
    n = np.abs(rel)
    max_exact = nb // 2
    nf = np.maximum(n, 1).astype(np.float32)
    large = max_exact + (np.log(nf / np.float32(max_exact)) / np.float32(math.log(MAX_DISTANCE / max_exact))
                         * np.float32(nb - max_exact)).astype(np.int32)
    large = np.minimum(large, nb - 1)
    return (ret + np.where(n < max_exact, n, large)).astype(np.int32)


FAR_BUCKET = N_BUCKETS // 2 - 1


def _lookup(table, idx):
    idx_c = jnp.asarray(idx)[..., None]
    out = jnp.zeros(idx.shape + (table.shape[1],), F32)
    for b in np.unique(idx):
        out = jnp.where(idx_c == int(b), table[int(b)], out)
    return out


def _rms(x, g):
    return x * lax.rsqrt(jnp.mean(x * x, axis=-1, keepdims=True) + RMS_EPS) * g


def _ln(x, g, b):
    mu = jnp.mean(x, axis=-1, keepdims=True)
    xc = x - mu
    var = jnp.mean(xc * xc, axis=-1, keepdims=True)
    return xc * lax.rsqrt(var + LN_EPS) * g + b


def _lam(lamv):
    s1 = jnp.sum(lamv[0:1] * lamv[1:2], axis=-1, keepdims=True)
    s2 = jnp.sum(lamv[2:3] * lamv[3:4], axis=-1, keepdims=True)
    return jnp.exp(s1) - jnp.exp(s2) + LAMBDA_INIT


def _dot(a, b):
    return jnp.dot(a, b, preferred_element_type=F32)


def _dot_nt(a, b):
    return lax.dot_general(a, b, (((1,), (1,)), ((), ())), preferred_element_type=F32)


def _l0_prompt_kernel(lamv_ref, x_ref, g0_ref, win_ref, bias_ref, subg_ref, lng_ref, lnb_ref, wm_ref, bs_ref,
                      wout_ref, y_ref, k_ref, v_ref,
                      z_ref, kbf_ref, vt_ref, qm_ref, m_ref, l_ref, acc_ref, mix_ref):
    i = pl.program_id(1)
    x = x_ref[0]
    z_ref[...] = _dot(_rms(x, g0_ref[...]).astype(BF16), win_ref[...])

    k = z_ref[:, K0:K0 + A_WIDTH]
    v = z_ref[:, V0:V0 + A_WIDTH]
    k_ref[0] = k
    v_ref[0] = v
    kbf_ref[i] = k.astype(BF16)
    vt_ref[i] = v.T.astype(BF16)

    lo = lax.broadcasted_iota(jnp.int32, (TM, HEAD_DIM), 1) < HALF
    for h in range(HEADS):
        q = z_ref[:, Q0 + h * HEAD_DIM:Q0 + (h + 1) * HEAD_DIM] * (HALF ** -0.5)
        qm_ref[h, 0:TM, :] = jnp.where(lo, q, 0.0).astype(BF16)
        qm_ref[h, TM:2 * TM, :] = jnp.where(lo, 0.0, q).astype(BF16)

    def attend(j, bias_idx, first):
        for h in range(HEADS):
            kt = kbf_ref[j, :, h * HEAD_DIM:(h + 1) * HEAD_DIM]
            vt = vt_ref[j, h * HEAD_DIM:(h + 1) * HEAD_DIM, :]
            s = _dot_nt(kt, qm_ref[h])
            if bias_idx is not None:
                s = s + bias_ref[h, bias_idx]
            mt = jnp.max(s, axis=0, keepdims=True)
            if first:
                p = jnp.exp(s - mt)
                m_ref[h] = mt
                l_ref[h] = jnp.sum(p, axis=0, keepdims=True)
                acc_ref[h] = _dot(vt, p.astype(BF16))
            else:
                m_old = m_ref[h]
                m_new = jnp.maximum(m_old, mt)
                alpha = jnp.exp(m_old - m_new)
                p = jnp.exp(s - m_new)
                m_ref[h] = m_new
                l_ref[h] = alpha * l_ref[h] + jnp.sum(p, axis=0, keepdims=True)
                acc_ref[h] = alpha * acc_ref[h] + _dot(vt, p.astype(BF16))

    attend(i, 0, True)

    @pl.when(i > 0)
    def _():
        attend(i - 1, 1, False)

    def far(j, carry):
        attend(j, None, False)
        return carry

    lax.fori_loop(0, i - 1, far, 0)

    lam = _lam(lamv_ref[...])
    for h in range(HEADS):
        on = acc_ref[h] / l_ref[h]
        o = (on[:, :TM] - lam * on[:, TM:]).T
        o = _rms(o, subg_ref[...]) * (1.0 - LAMBDA_INIT)
        ga = z_ref[:, GA0 + h * HEAD_DIM:GA0 + (h + 1) * HEAD_DIM]
        mix_ref[:, h * HEAD_DIM:(h + 1) * HEAD_DIM] = (o * jax.nn.silu(ga)).astype(BF16)

    for g in range(GROUPS):
        sl = slice(g * GROUP_DIM, (g + 1) * GROUP_DIM)
        ub = z_ref[:, UB0 + g * GROUP_DIM:UB0 + (g + 1) * GROUP_DIM]
        vb = z_ref[:, VB0 + g * GROUP_DIM:VB0 + (g + 1) * GROUP_DIM]
        gb = z_ref[:, GB0 + g * GROUP_DIM:GB0 + (g + 1) * GROUP_DIM]
        zv = _ln(jax.nn.gelu(vb), lng_ref[:, sl], lnb_ref[:, sl]).astype(BF16)
        zu = jax.nn.gelu(ub)
        gate = jax.nn.silu(gb)
        for c in range(TM // B_CHUNK):
            rows = slice(c * B_CHUNK, (c + 1) * B_CHUNK)
            sg = _dot(wm_ref[g], zv[rows]) + bs_ref[:, sl]
            mix_ref[rows, A_WIDTH + g * GROUP_DIM:A_WIDTH + (g + 1) * GROUP_DIM] = (
                zu[rows] * sg * gate[rows]).astype(BF16)

    y_ref[0] = x + _dot(mix_ref[...], wout_ref[...])


def _l0_prompt(x, lamv, g0, win, bias, subg, lng, lnb, wm, bs, wout):
    bsz, t, d = x.shape
    nt = t // TM
    const = lambda shape: pl.BlockSpec(shape, lambda b, i: (0,) * len(shape))
    row = lambda w: pl.BlockSpec((1, TM, w), lambda b, i: (b, i, 0))
    return pl.pallas_call(
        _l0_prompt_kernel,
        grid=(bsz, nt),
        in_specs=[const((4, HALF)), row(d), const((1, d)), const((d, L0_IN)),
                  const((HEADS, 2, TM, 2 * TM)), const((1, HEAD_DIM)), const((1, B_WIDTH)), const((1, B_WIDTH)),
                  const((GROUPS, B_CHUNK, B_CHUNK)), const((B_CHUNK, B_WIDTH)), const((A_WIDTH + B_WIDTH, d))],
        out_specs=[row(d), row(A_WIDTH), row(A_WIDTH)],
        out_shape=[jax.ShapeDtypeStruct((bsz, t, d), F32),
                   jax.ShapeDtypeStruct((bsz, t, A_WIDTH), F32),
                   jax.ShapeDtypeStruct((bsz, t, A_WIDTH), F32)],
        scratch_shapes=[pltpu.VMEM((TM, L0_IN), F32),
                        pltpu.VMEM((nt, TM, A_WIDTH), BF16),
                        pltpu.VMEM((nt, A_WIDTH, TM), BF16),
                        pltpu.VMEM((HEADS, 2 * TM, HEAD_DIM), BF16),
                        pltpu.VMEM((HEADS, 1, 2 * TM), F32),
                        pltpu.VMEM((HEADS, 1, 2 * TM), F32),
                        pltpu.VMEM((HEADS, HEAD_DIM, 2 * TM), F32),
                        pltpu.VMEM((TM, A_WIDTH + B_WIDTH), BF16)],
        compiler_params=pltpu.CompilerParams(dimension_semantics=("arbitrary", "arbitrary"),
                                             vmem_limit_bytes=VMEM_LIMIT),
        name="l0_prompt",
    )(lamv, x, g0, win, bias, subg, lng, lnb, wm, bs, wout)


def _conv_rows(buf_ref, c, base, rows, wdw_ref, bdw_ref):
    lanes = slice(c * 128, (c + 1) * 128)
    acc = jnp.broadcast_to(bdw_ref[:, lanes], (rows, 128))
    for tap in range(CONV_WIDTH):
        lo = base + tap + CONV_OFF
        acc = acc + wdw_ref[tap:tap + 1, lanes] * buf_ref[c, lo:lo + rows, :]
    return acc


def _l1_prompt_kernel(x_ref, g1_ref, win_ref, wdw_ref, bdw_ref, clng_ref, clnb_ref, wout_ref, fg_ref,
                      y_ref, tail_ref, z_ref, buf_ref, conv_ref):
    i = pl.program_id(1)
    d = x_ref.shape[-1]
    ngroups = d // 128

    @pl.when(i == 0)
    def _():
        buf_ref[:, 0:CONV_PAD, :] = jnp.zeros((ngroups, CONV_PAD, 128), F32)

    x = x_ref[0]
    z_ref[...] = _dot(_rms(x, g1_ref[...]).astype(BF16), win_ref[...])
    for c in range(ngroups):
        buf_ref[c, CONV_PAD:CONV_PAD + TM, :] = (z_ref[:, c * 128:(c + 1) * 128]
                                                 * jax.nn.sigmoid(z_ref[:, d + c * 128:d + (c + 1) * 128]))
    for c in range(ngroups):
        for r in range(TM // CONV_RB):
            conv_ref[r * CONV_RB:(r + 1) * CONV_RB, c * 128:(c + 1) * 128] = _conv_rows(
                buf_ref, c, r * CONV_RB, CONV_RB, wdw_ref, bdw_ref)

    cn = jax.nn.silu(_ln(conv_ref[...], clng_ref[...], clnb_ref[...])) * jax.nn.silu(z_ref[:, 2 * d:3 * d])
    y1 = x + _dot(cn.astype(BF16), wout_ref[...])
    y_ref[0] = _rms(y1, fg_ref[...])

    for c in range(ngroups):
        tail = buf_ref[c, TM:TM + CONV_PAD, :]
        tail_ref[0, :, c * 128:(c + 1) * 128] = tail
        buf_ref[c, 0:CONV_PAD, :] = tail


def _l1_prompt(x, g1, win, wdw, bdw, clng, clnb, wout, fg):
    bsz, t, d = x.shape
    nt = t // TM
    const = lambda shape: pl.BlockSpec(shape, lambda b, i: (0,) * len(shape))
    row = pl.BlockSpec((1, TM, d), lambda b, i: (b, i, 0))
    return pl.pallas_call(
        _l1_prompt_kernel,
        grid=(bsz, nt),
        in_specs=[row, const((1, d)), const((d, 3 * d)), const((CONV_WIDTH, d)), const((1, d)),
                  const((1, d)), const((1, d)), const((d, d)), const((1, d))],
        out_specs=[row, pl.BlockSpec((1, CONV_PAD, d), lambda b, i: (b, 0, 0))],
        out_shape=[jax.ShapeDtypeStruct((bsz, t, d), F32),
                   jax.ShapeDtypeStruct((bsz, CONV_PAD, d), F32)],
        scratch_shapes=[pltpu.VMEM((TM, 3 * d), F32),
                        pltpu.VMEM((d // 128, CONV_PAD + TM, 128), F32),
                        pltpu.VMEM((TM, d), F32)],
        compiler_params=pltpu.CompilerParams(dimension_semantics=("arbitrary", "arbitrary"),
                                             vmem_limit_bytes=VMEM_LIMIT),
        name="l1_prompt",
    )(x, g1, win, wdw, bdw, clng, clnb, wout, fg)


def _s_proj_kernel(x_ref, g0_ref, win_ref, z_ref):
    z_ref[...] = _dot(_rms(x_ref[...], g0_ref[...]).astype(BF16), win_ref[...])


def _s_attn_kernel(lamv_ref, z_ref, ck_ref, cv_ref, bias_ref, o_ref, *, dec_seq, pad_rows):
    ncol = HEADS * 2 * dec_seq
    q = z_ref[:, Q0:Q0 + A_WIDTH] * (HALF ** -0.5)
    q_rep = jnp.concatenate([q] * (2 * HEADS), axis=0)
    row = lax.broadcasted_iota(jnp.int32, (ncol, A_WIDTH), 0)
    col = lax.broadcasted_iota(jnp.int32, (ncol, A_WIDTH), 1)
    q_bd = jnp.where(col // HALF == row // dec_seq, q_rep, 0.0).astype(BF16)

    zeros = jnp.zeros((pad_rows - dec_seq, A_WIDTH), F32)
    k_all = jnp.concatenate([ck_ref[0], z_ref[:, K0:K0 + A_WIDTH], zeros], axis=0).astype(BF16)
    v_all = jnp.concatenate([cv_ref[0], z_ref[:, V0:V0 + A_WIDTH], zeros], axis=0).astype(BF16)

    s = _dot_nt(k_all, q_bd) + bias_ref[...]
    p = jnp.exp(s - jnp.max(s, axis=0, keepdims=True))
    p = p / jnp.sum(p, axis=0, keepdims=True)
    w = p - _lam(lamv_ref[...]) * pltpu.roll(p, ncol - dec_seq, axis=1)
    o_all = _dot(w.T.astype(BF16), v_all)
    for h in range(HEADS):
        o_ref[:, h * HEAD_DIM:(h + 1) * HEAD_DIM] = o_all[2 * h * dec_seq:(2 * h + 1) * dec_seq,
                                                          h * HEAD_DIM:(h + 1) * HEAD_DIM]


def _s_rest_kernel(x_ref, z_ref, o_ref, subg_ref, lng_ref, lnb_ref, wbd_ref, bs_ref, wout0_ref,
                   g1_ref, win1_ref, st_ref, wdw_ref, bdw_ref, clng_ref, clnb_ref, wout1_ref, fg_ref,
                   y_ref, zv_ref, u_ref, mix_ref, buf_ref, conv_ref, *, dec_seq):
    d = x_ref.shape[-1]
    nb = st_ref.shape[0]
    for h in range(HEADS):
        sl = slice(h * HEAD_DIM, (h + 1) * HEAD_DIM)
        o = _rms(o_ref[:, sl], subg_ref[...]) * (1.0 - LAMBDA_INIT)
        mix_ref[:, sl] = (o * jax.nn.silu(z_ref[:, GA0 + h * HEAD_DIM:GA0 + (h + 1) * HEAD_DIM])).astype(BF16)
    for g in range(GROUPS):
        sl = slice(g * GROUP_DIM, (g + 1) * GROUP_DIM)
        ub = z_ref[:, UB0 + g * GROUP_DIM:UB0 + (g + 1) * GROUP_DIM]
        vb = z_ref[:, VB0 + g * GROUP_DIM:VB0 + (g + 1) * GROUP_DIM]
        gb = z_ref[:, GB0 + g * GROUP_DIM:GB0 + (g + 1) * GROUP_DIM]
        zv = _ln(jax.nn.gelu(vb), lng_ref[:, sl], lnb_ref[:, sl])
        zv_ref[:, sl] = zv
        sg = _dot(wbd_ref[g], zv.astype(BF16)) + bs_ref[:, sl]
        mix_ref[:, A_WIDTH + g * GROUP_DIM:A_WIDTH + (g + 1) * GROUP_DIM] = (
            jax.nn.gelu(ub) * sg * jax.nn.silu(gb)).astype(BF16)
    y0 = x_ref[...] + _dot(mix_ref[...], wout0_ref[...])

    z1 = _dot(_rms(y0, g1_ref[...]).astype(BF16), win1_ref[...])
    u = z1[:, 0:d] * jax.nn.sigmoid(z1[:, d:2 * d])
    u_ref[...] = u
    ngroups = d // 128
    for b in range(nb):
        for c in range(ngroups):
            lanes = slice(c * 128, (c + 1) * 128)
            buf_ref[b * ngroups + c, 0:CONV_PAD, :] = st_ref[b, :, lanes]
            buf_ref[b * ngroups + c, CONV_PAD:CONV_PAD + dec_seq, :] = u[b * dec_seq:(b + 1) * dec_seq, lanes]
    for b in range(nb):
        for c in range(ngroups):
            conv_ref[b * dec_seq:(b + 1) * dec_seq, c * 128:(c + 1) * 128] = _conv_rows(
                buf_ref.at[pl.ds(b * ngroups, ngroups)], c, 0, dec_seq, wdw_ref, bdw_ref)
    c = jax.nn.silu(_ln(conv_ref[...], clng_ref[...], clnb_ref[...])) * jax.nn.silu(z1[:, 2 * d:3 * d])
    y1 = y0 + _dot(c.astype(BF16), wout1_ref[...])
    y_ref[...] = _rms(y1, fg_ref[...])


def _sample(xs, cache_k, cache_v, state, lamv, g0, win0, bias_s, subg, lng, lnb, wbd, bs_s, wout0,
            g1, win1, wdw, bdw, clng, clnb, wout1, fg):
    nb, dec_seq, d = xs.shape
    rows = nb * dec_seq
    past = cache_k.shape[1]
    pad_rows = bias_s.shape[0] - past
    x2 = xs.reshape(rows, d)
    z = pl.pallas_call(_s_proj_kernel, out_shape=jax.ShapeDtypeStruct((rows, L0_IN), F32),
                       compiler_params=pltpu.CompilerParams(vmem_limit_bytes=VMEM_LIMIT),
                       name="s_proj")(x2, g0, win0)

    ncol = HEADS * 2 * dec_seq
    o = pl.pallas_call(
        functools.partial(_s_attn_kernel, dec_seq=dec_seq, pad_rows=pad_rows),
        grid=(nb,),
        in_specs=[pl.BlockSpec((4, HALF), lambda b: (0, 0)),
                  pl.BlockSpec((dec_seq, L0_IN), lambda b: (b, 0)),
                  pl.BlockSpec((1, past, A_WIDTH), lambda b: (b, 0, 0)),
                  pl.BlockSpec((1, past, A_WIDTH), lambda b: (b, 0, 0)),
                  pl.BlockSpec((past + pad_rows, ncol), lambda b: (0, 0))],
        out_specs=pl.BlockSpec((dec_seq, A_WIDTH), lambda b: (b, 0)),
        out_shape=jax.ShapeDtypeStruct((rows, A_WIDTH), F32),
        compiler_params=pltpu.CompilerParams(dimension_semantics=("arbitrary",), vmem_limit_bytes=VMEM_LIMIT),
        name="s_attn",
    )(lamv, z, cache_k.reshape(nb, past, A_WIDTH), cache_v.reshape(nb, past, A_WIDTH), bias_s)

    state_pad = jnp.pad(state, ((0, 0), (CONV_OFF, 0), (0, 0)))
    y, zv, u = pl.pallas_call(
        functools.partial(_s_rest_kernel, dec_seq=dec_seq),
        out_shape=[jax.ShapeDtypeStruct((rows, d), F32),
                   jax.ShapeDtypeStruct((rows, B_WIDTH), F32),
                   jax.ShapeDtypeStruct((rows, d), F32)],
        scratch_shapes=[pltpu.VMEM((rows, A_WIDTH + B_WIDTH), BF16),
                        pltpu.VMEM((nb * (d // 128), CONV_PAD + dec_seq, 128), F32),
                        pltpu.VMEM((rows, d), F32)],
        compiler_params=pltpu.CompilerParams(vmem_limit_bytes=VMEM_LIMIT),
        name="s_rest",
    )(x2, z, o, subg, lng, lnb, wbd, bs_s, wout0, g1, win1, state_pad, wdw, bdw, clng, clnb, wout1, fg)
    return z, y, zv, u


def kernel(x_prompt, x_sample, cache_k0, cache_v0, state_conv1, rel_bias, norm_g0, w_in0, lambda_q1, lambda_k1,
           lambda_q2, lambda_k2, subln_g0, gv_ln_g0, gv_ln_b0, w_s0, b_s0, w_out0, norm_g1, w_in1, w_dw1, b_dw1,
           conv_ln_g1, conv_ln_b1, w_out1, final_g):
    bsz, t, d = x_prompt.shape
    nb, dec_seq, _ = x_sample.shape
    past = cache_k0.shape[1]
    assert t % TM == 0 and TM % B_CHUNK == 0 and TM % CHUNK == 0 and TM >= MAX_DISTANCE
    assert dec_seq <= B_CHUNK and dec_seq % 8 == 0 and HEADS * 2 * dec_seq == 128
    assert past % CHUNK == 0 and (past + dec_seq - 1) // CHUNK == past // CHUNK

    row = lambda a: a.reshape(1, -1).astype(F32)
    lamv = jnp.stack([lambda_q1, lambda_k1, lambda_q2, lambda_k2]).astype(F32)
    win0, wout0 = w_in0.astype(BF16), w_out0.astype(BF16)
    win1, wout1 = w_in1.astype(BF16), w_out1.astype(BF16)
    g0, g1, fg, subg = row(norm_g0), row(norm_g1), row(final_g), row(subln_g0)
    lng, lnb = row(gv_ln_g0), row(gv_ln_b0)
    bdw, clng, clnb = row(b_dw1), row(conv_ln_g1), row(conv_ln_b1)
    wdw = w_dw1.astype(F32)

    rbs = (rel_bias - rel_bias[FAR_BUCKET:FAR_BUCKET + 1]).astype(F32)
    assert np.all(_np_bucket(-np.arange(TM + 1, 2 * max(t, past + dec_seq))) == FAR_BUCKET)
    kk = np.arange(TM)[:, None]
    qq = np.arange(TM)[None, :]
    own = jnp.where((kk // CHUNK <= qq // CHUNK)[..., None], _lookup(rbs, _np_bucket(kk - qq)), NEG)
    prev = _lookup(rbs, _np_bucket(kk - qq - TM))
    bias_p = jnp.transpose(jnp.stack([own, prev]), (3, 0, 1, 2))
    bias_p = jnp.concatenate([bias_p, bias_p], axis=-1)

    tril = np.tril(np.ones((B_CHUNK, B_CHUNK), dtype=bool))
    wm = jnp.where(tril[None], w_s0, 0)
    bs = jnp.repeat(b_s0.T.astype(F32), GROUP_DIM, axis=1)

    y0p, k0p, v0p = _l0_prompt(x_prompt, lamv, g0, win0, bias_p, subg, lng, lnb, wm.astype(BF16), bs, wout0)
    y_prompt, tail_p = _l1_prompt(y0p, g1, win1, wdw, bdw, clng, clnb, wout1, fg)

    pad_rows = 128
    kpos = np.arange(past + dec_seq)[:, None]
    qpos = past + np.arange(dec_seq)[None, :]
    bsmp = jnp.transpose(_lookup(rbs, _np_bucket(kpos - qpos)), (0, 2, 1))
    bsmp = jnp.broadcast_to(bsmp[:, :, None, :], (past + dec_seq, HEADS, 2, dec_seq)).reshape(past + dec_seq, -1)
    bias_s = jnp.pad(bsmp, ((0, pad_rows - dec_seq), (0, 0)), constant_values=NEG)
    wbd = jnp.stack([jnp.kron(jnp.eye(nb, dtype=F32), wm[g, :dec_seq, :dec_seq]) for g in range(GROUPS)])
    bs_s = jnp.tile(bs[:dec_seq], (nb, 1))

    zs, ys, zvs, us = _sample(x_sample, cache_k0, cache_v0, state_conv1, lamv, g0, win0, bias_s, subg, lng, lnb,
                              wbd.astype(BF16), bs_s, wout0, g1, win1, wdw, bdw, clng, clnb, wout1, fg)

    keep = CONV_WIDTH - 1
    return (y_prompt,
            ys.reshape(nb, dec_seq, d),
            k0p.reshape(bsz, t, HEADS, HEAD_DIM),
            v0p.reshape(bsz, t, HEADS, HEAD_DIM),
            tail_p[:, CONV_OFF:],
            zs[:, K0:K0 + A_WIDTH].reshape(nb, dec_seq, HEADS, HEAD_DIM),
            zs[:, V0:V0 + A_WIDTH].reshape(nb, dec_seq, HEADS, HEAD_DIM),
            zvs.reshape(nb, dec_seq, B_WIDTH),
            jnp.concatenate([state_conv1, us.reshape(nb, dec_seq, d)], axis=1)[:, -keep:])
```

```python
import functools
import math

import numpy as np
import jax
import jax.numpy as jnp
from jax import lax
from jax.experimental import pallas as pl
from jax.experimental.pallas import tpu as pltpu

F32 = jnp.float32
BF16 = jnp.bfloat16

CHUNK = 64
HEADS = 4
HALF = 64
HEAD_DIM = 2 * HALF
A_WIDTH = HEADS * HEAD_DIM
LAMBDA_INIT = 0.2
N_BUCKETS = 32
MAX_DISTANCE = 128
GROUPS = 4
GROUP_DIM = 128
B_WIDTH = GROUPS * GROUP_DIM
B_CHUNK = 128
CONV_WIDTH = 31
RMS_EPS = 1e-6
LN_EPS = 1e-5

Q0, K0, V0, GA0, UB0, VB0, GB0 = (i * 512 for i in range(7))
L0_IN = 7 * 512

TM = 256
CONV_PAD = 32
CONV_OFF = CONV_PAD - (CONV_WIDTH - 1)
CONV_RB = 64
EXP_ROWS = 64
NEG = -1e30
VMEM_LIMIT = 56 * 1024 * 1024


def _np_bucket(rel):
    nb = N_BUCKETS // 2
    ret = np.where(rel > 0, nb, 0)
    n = np.abs(rel)
    max_exact = nb // 2
    nf = np.maximum(n, 1).astype(np.float32)
    large = max_exact + (np.log(nf / np.float32(max_exact)) / np.float32(math.log(MAX_DISTANCE / max_exact))
                         * np.float32(nb - max_exact)).astype(np.int32)
    large = np.minimum(large, nb - 1)
    return (ret + np.where(n < max_exact, n, large)).astype(np.int32)


FAR_BUCKET = N_BUCKETS // 2 - 1


def _lookup(table, idx):
    idx_c = jnp.asarray(idx)[..., None]
    out = jnp.zeros(idx.shape + (table.shape[1],), F32)
    for b in np.unique(idx):
        out = jnp.where(idx_c == int(b), table[int(b)], out)
    return out


def _rms(x, g):
    return x * lax.rsqrt(jnp.mean(x * x, axis=-1, keepdims=True) + RMS_EPS) * g


def _ln(x, g, b):
    mu = jnp.mean(x, axis=-1, keepdims=True)
    xc = x - mu
    var = jnp.mean(xc * xc, axis=-1, keepdims=True)
    return xc * lax.rsqrt(var + LN_EPS) * g + b


def _lam(lamv):
    s1 = jnp.sum(lamv[0:1] * lamv[1:2], axis=-1, keepdims=True)
    s2 = jnp.sum(lamv[2:3] * lamv[3:4], axis=-1, keepdims=True)
    return jnp.exp(s1) - jnp.exp(s2) + LAMBDA_INIT


def _dot(a, b):
    return jnp.dot(a, b, preferred_element_type=F32)


def _dot_nt(a, b):
    return lax.dot_general(a, b, (((1,), (1,)), ((), ())), preferred_element_type=F32)


def _l0_prompt_kernel(lamv_ref, x_ref, g0_ref, win_ref, bias_ref, subg_ref, lng_ref, lnb_ref, wm_ref, bs_ref,
                      wout_ref, y_ref, k_ref, v_ref,
                      z_ref, kbf_ref, vt_ref, qm_ref, m_ref, l_ref, acc_ref, s_ref, p_ref, mix_ref):
    i = pl.program_id(1)
    x = x_ref[0]
    z_ref[...] = _dot(_rms(x, g0_ref[...]).astype(BF16), win_ref[...])

    k = z_ref[:, K0:K0 + A_WIDTH]
    v = z_ref[:, V0:V0 + A_WIDTH]
    k_ref[0] = k
    v_ref[0] = v
    kbf_ref[i] = k.astype(BF16)
    vt_ref[i] = v.T.astype(BF16)

    lo = lax.broadcasted_iota(jnp.int32, (TM, HEAD_DIM), 1) < HALF
    for h in range(HEADS):
        q = z_ref[:, Q0 + h * HEAD_DIM:Q0 + (h + 1) * HEAD_DIM] * (HALF ** -0.5)
        qm_ref[2 * h] = jnp.where(lo, q, 0.0).astype(BF16)
        qm_ref[2 * h + 1] = jnp.where(lo, 0.0, q).astype(BF16)

    units = range(2 * HEADS)

    def attend(j, bias_idx, first):
        m_new, alpha = {}, {}
        for u in units:
            h = u // 2
            s = _dot_nt(kbf_ref[j, :, h * HEAD_DIM:(h + 1) * HEAD_DIM], qm_ref[u])
            if bias_idx is not None:
                s = s + bias_ref[h, bias_idx]
            s_ref[u] = s
            mt = jnp.max(s, axis=0, keepdims=True)
            if first:
                m_new[u] = mt
            else:
                m_old = m_ref[u]
                m_new[u] = jnp.maximum(m_old, mt)
                alpha[u] = jnp.exp(m_old - m_new[u])
            m_ref[u] = m_new[u]
        for u in units:
            part = jnp.zeros((8, TM), F32)
            for r in range(TM // EXP_ROWS):
                rows = slice(r * EXP_ROWS, (r + 1) * EXP_ROWS)
                p = jnp.exp(s_ref[u, rows, :] - m_new[u])
                part = part + jnp.sum(p.reshape(EXP_ROWS // 8, 8, TM), axis=0)
                p_ref[u, rows, :] = p.astype(BF16)
            lsum = jnp.sum(part, axis=0, keepdims=True)
            l_ref[u] = lsum if first else alpha[u] * l_ref[u] + lsum
        for u in units:
            h = u // 2
            pv = _dot(vt_ref[j, h * HEAD_DIM:(h + 1) * HEAD_DIM, :], p_ref[u])
            acc_ref[u] = pv if first else alpha[u] * acc_ref[u] + pv

    attend(i, 0, True)

    @pl.when(i > 0)
    def _():
        attend(i - 1, 1, False)

    def far(j, carry):
        attend(j, None, False)
        return carry

    lax.fori_loop(0, i - 1, far, 0)

    lam = _lam(lamv_ref[...])
    for h in range(HEADS):
        on1 = acc_ref[2 * h] / l_ref[2 * h]
        on2 = acc_ref[2 * h + 1] / l_ref[2 * h + 1]
        o = (on1 - lam * on2).T
        o = _rms(o, subg_ref[...]) * (1.0 - LAMBDA_INIT)
        ga = z_ref[:, GA0 + h * HEAD_DIM:GA0 + (h + 1) * HEAD_DIM]
        mix_ref[:, h * HEAD_DIM:(h + 1) * HEAD_DIM] = (o * jax.nn.silu(ga)).astype(BF16)

    for g in range(GROUPS):
        sl = slice(g * GROUP_DIM, (g + 1) * GROUP_DIM)
        ub = z_ref[:, UB0 + g * GROUP_DIM:UB0 + (g + 1) * GROUP_DIM]
        vb = z_ref[:, VB0 + g * GROUP_DIM:VB0 + (g + 1) * GROUP_DIM]
        gb = z_ref[:, GB0 + g * GROUP_DIM:GB0 + (g + 1) * GROUP_DIM]
        zv = _ln(jax.nn.gelu(vb), lng_ref[:, sl], lnb_ref[:, sl]).astype(BF16)
        zu = jax.nn.gelu(ub)
        gate = jax.nn.silu(gb)
        for c in range(TM // B_CHUNK):
            rows = slice(c * B_CHUNK, (c + 1) * B_CHUNK)
            sg = _dot(wm_ref[g], zv[rows]) + bs_ref[:, sl]
            mix_ref[rows, A_WIDTH + g * GROUP_DIM:A_WIDTH + (g + 1) * GROUP_DIM] = (
                zu[rows] * sg * gate[rows]).astype(BF16)

    y_ref[0] = x + _dot(mix_ref[...], wout_ref[...])


def _l0_prompt(x, lamv, g0, win, bias, subg, lng, lnb, wm, bs, wout):
    bsz, t, d = x.shape
    nt = t // TM
    const = lambda shape: pl.BlockSpec(shape, lambda b, i: (0,) * len(shape))
    row = lambda w: pl.BlockSpec((1, TM, w), lambda b, i: (b, i, 0))
    return pl.pallas_call(
        _l0_prompt_kernel,
        grid=(bsz, nt),
        in_specs=[const((4, HALF)), row(d), const((1, d)), const((d, L0_IN)),
                  const((HEADS, 2, TM, TM)), const((1, HEAD_DIM)), const((1, B_WIDTH)), const((1, B_WIDTH)),
                  const((GROUPS, B_CHUNK, B_CHUNK)), const((B_CHUNK, B_WIDTH)), const((A_WIDTH + B_WIDTH, d))],
        out_specs=[row(d), row(A_WIDTH), row(A_WIDTH)],
        out_shape=[jax.ShapeDtypeStruct((bsz, t, d), F32),
                   jax.ShapeDtypeStruct((bsz, t, A_WIDTH), F32),
                   jax.ShapeDtypeStruct((bsz, t, A_WIDTH), F32)],
        scratch_shapes=[pltpu.VMEM((TM, L0_IN), F32),
                        pltpu.VMEM((nt, TM, A_WIDTH), BF16),
                        pltpu.VMEM((nt, A_WIDTH, TM), BF16),
                        pltpu.VMEM((2 * HEADS, TM, HEAD_DIM), BF16),
                        pltpu.VMEM((2 * HEADS, 1, TM), F32),
                        pltpu.VMEM((2 * HEADS, 1, TM), F32),
                        pltpu.VMEM((2 * HEADS, HEAD_DIM, TM), F32),
                        pltpu.VMEM((2 * HEADS, TM, TM), F32),
                        pltpu.VMEM((2 * HEADS, TM, TM), BF16),
                        pltpu.VMEM((TM, A_WIDTH + B_WIDTH), BF16)],
        compiler_params=pltpu.CompilerParams(dimension_semantics=("arbitrary", "arbitrary"),
                                             vmem_limit_bytes=VMEM_LIMIT),
        name="l0_prompt",
    )(lamv, x, g0, win, bias, subg, lng, lnb, wm, bs, wout)


def _conv_rows(buf_ref, c, base, rows, wdw_ref, bdw_ref):
    lanes = slice(c * 128, (c + 1) * 128)
    acc = jnp.broadcast_to(bdw_ref[:, lanes], (rows, 128))
    for tap in range(CONV_WIDTH):
        lo = base + tap + CONV_OFF
        acc = acc + wdw_ref[tap:tap + 1, lanes] * buf_ref[c, lo:lo + rows, :]
    return acc


def _l1_prompt_kernel(x_ref, g1_ref, win_ref, wdw_ref, bdw_ref, clng_ref, clnb_ref, wout_ref, fg_ref,
                      y_ref, tail_ref, z_ref, buf_ref, conv_ref):
    i = pl.program_id(1)
    d = x_ref.shape[-1]
    ngroups = d // 128

    @pl.when(i == 0)
    def _():
        buf_ref[:, 0:CONV_PAD, :] = jnp.zeros((ngroups, CONV_PAD, 128), F32)

    x = x_ref[0]
    z_ref[...] = _dot(_rms(x, g1_ref[...]).astype(BF16), win_ref[...])
    for c in range(ngroups):
        buf_ref[c, CONV_PAD:CONV_PAD + TM, :] = (z_ref[:, c * 128:(c + 1) * 128]
                                                 * jax.nn.sigmoid(z_ref[:, d + c * 128:d + (c + 1) * 128]))
    for c in range(ngroups):
        for r in range(TM // CONV_RB):
            conv_ref[r * CONV_RB:(r + 1) * CONV_RB, c * 128:(c + 1) * 128] = _conv_rows(
                buf_ref, c, r * CONV_RB, CONV_RB, wdw_ref, bdw_ref)

    cn = jax.nn.silu(_ln(conv_ref[...], clng_ref[...], clnb_ref[...])) * jax.nn.silu(z_ref[:, 2 * d:3 * d])
    y1 = x + _dot(cn.astype(BF16), wout_ref[...])
    y_ref[0] = _rms(y1, fg_ref[...])

    for c in range(ngroups):
        tail = buf_ref[c, TM:TM + CONV_PAD, :]
        tail_ref[0, :, c * 128:(c + 1) * 128] = tail
        buf_ref[c, 0:CONV_PAD, :] = tail


def _l1_prompt(x, g1, win, wdw, bdw, clng, clnb, wout, fg):
    bsz, t, d = x.shape
    nt = t // TM
    const = lambda shape: pl.BlockSpec(shape, lambda b, i: (0,) * len(shape))
    row = pl.BlockSpec((1, TM, d), lambda b, i: (b, i, 0))
    return pl.pallas_call(
        _l1_prompt_kernel,
        grid=(bsz, nt),
        in_specs=[row, const((1, d)), const((d, 3 * d)), const((CONV_WIDTH, d)), const((1, d)),
                  const((1, d)), const((1, d)), const((d, d)), const((1, d))],
        out_specs=[row, pl.BlockSpec((1, CONV_PAD, d), lambda b, i: (b, 0, 0))],
        out_shape=[jax.ShapeDtypeStruct((bsz, t, d), F32),
                   jax.ShapeDtypeStruct((bsz, CONV_PAD, d), F32)],
        scratch_shapes=[pltpu.VMEM((TM, 3 * d), F32),
                        pltpu.VMEM((d // 128, CONV_PAD + TM, 128), F32),
                        pltpu.VMEM((TM, d), F32)],
        compiler_params=pltpu.CompilerParams(dimension_semantics=("arbitrary", "arbitrary"),
                                             vmem_limit_bytes=VMEM_LIMIT),
        name="l1_prompt",
    )(x, g1, win, wdw, bdw, clng, clnb, wout, fg)


def _s_proj_kernel(x_ref, g0_ref, win_ref, z_ref):
    z_ref[...] = _dot(_rms(x_ref[...], g0_ref[...]).astype(BF16), win_ref[...])


def _s_attn_kernel(lamv_ref, z_ref, ck_ref, cv_ref, bias_ref, o_ref, *, dec_seq, pad_rows):
    ncol = HEADS * 2 * dec_seq
    q = z_ref[:, Q0:Q0 + A_WIDTH] * (HALF ** -0.5)
    q_rep = jnp.concatenate([q] * (2 * HEADS), axis=0)
    row = lax.broadcasted_iota(jnp.int32, (ncol, A_WIDTH), 0)
    col = lax.broadcasted_iota(jnp.int32, (ncol, A_WIDTH), 1)
    q_bd = jnp.where(col // HALF == row // dec_seq, q_rep, 0.0).astype(BF16)

    zeros = jnp.zeros((pad_rows - dec_seq, A_WIDTH), F32)
    k_all = jnp.concatenate([ck_ref[0], z_ref[:, K0:K0 + A_WIDTH], zeros], axis=0).astype(BF16)
    v_all = jnp.concatenate([cv_ref[0], z_ref[:, V0:V0 + A_WIDTH], zeros], axis=0).astype(BF16)

    s = _dot_nt(k_all, q_bd) + bias_ref[...]
    p = jnp.exp(s - jnp.max(s, axis=0, keepdims=True))
    p = p / jnp.sum(p, axis=0, keepdims=True)
    w = p - _lam(lamv_ref[...]) * pltpu.roll(p, ncol - dec_seq, axis=1)
    o_all = _dot(w.T.astype(BF16), v_all)
    for h in range(HEADS):
        o_ref[:, h * HEAD_DIM:(h + 1) * HEAD_DIM] = o_all[2 * h * dec_seq:(2 * h + 1) * dec_seq,
                                                          h * HEAD_DIM:(h + 1) * HEAD_DIM]


def _s_rest_kernel(x_ref, z_ref, o_ref, subg_ref, lng_ref, lnb_ref, wbd_ref, bs_ref, wout0_ref,
                   g1_ref, win1_ref, st_ref, wdw_ref, bdw_ref, clng_ref, clnb_ref, wout1_ref, fg_ref,
                   y_ref, zv_ref, u_ref, mix_ref, buf_ref, conv_ref, *, dec_seq):
    d = x_ref.shape[-1]
    nb = st_ref.shape[0]
    for h in range(HEADS):
        sl = slice(h * HEAD_DIM, (h + 1) * HEAD_DIM)
        o = _rms(o_ref[:, sl], subg_ref[...]) * (1.0 - LAMBDA_INIT)
        mix_ref[:, sl] = (o * jax.nn.silu(z_ref[:, GA0 + h * HEAD_DIM:GA0 + (h + 1) * HEAD_DIM])).astype(BF16)
    for g in range(GROUPS):
        sl = slice(g * GROUP_DIM, (g + 1) * GROUP_DIM)
        ub = z_ref[:, UB0 + g * GROUP_DIM:UB0 + (g + 1) * GROUP_DIM]
        vb = z_ref[:, VB0 + g * GROUP_DIM:VB0 + (g + 1) * GROUP_DIM]
        gb = z_ref[:, GB0 + g * GROUP_DIM:GB0 + (g + 1) * GROUP_DIM]
        zv = _ln(jax.nn.gelu(vb), lng_ref[:, sl], lnb_ref[:, sl])
        zv_ref[:, sl] = zv
        sg = _dot(wbd_ref[g], zv.astype(BF16)) + bs_ref[:, sl]
        mix_ref[:, A_WIDTH + g * GROUP_DIM:A_WIDTH + (g + 1) * GROUP_DIM] = (
            jax.nn.gelu(ub) * sg * jax.nn.silu(gb)).astype(BF16)
    y0 = x_ref[...] + _dot(mix_ref[...], wout0_ref[...])

    z1 = _dot(_rms(y0, g1_ref[...]).astype(BF16), win1_ref[...])
    u = z1[:, 0:d] * jax.nn.sigmoid(z1[:, d:2 * d])
    u_ref[...] = u
    ngroups = d // 128
    for b in range(nb):
        for c in range(ngroups):
            lanes = slice(c * 128, (c + 1) * 128)
            buf_ref[b * ngroups + c, 0:CONV_PAD, :] = st_ref[b, :, lanes]
            buf_ref[b * ngroups + c, CONV_PAD:CONV_PAD + dec_seq, :] = u[b * dec_seq:(b + 1) * dec_seq, lanes]
    for b in range(nb):
        for c in range(ngroups):
            conv_ref[b * dec_seq:(b + 1) * dec_seq, c * 128:(c + 1) * 128] = _conv_rows(
                buf_ref.at[pl.ds(b * ngroups, ngroups)], c, 0, dec_seq, wdw_ref, bdw_ref)
    c = jax.nn.silu(_ln(conv_ref[...], clng_ref[...], clnb_ref[...])) * jax.nn.silu(z1[:, 2 * d:3 * d])
    y1 = y0 + _dot(c.astype(BF16), wout1_ref[...])
    y_ref[...] = _rms(y1, fg_ref[...])


def _sample(xs, cache_k, cache_v, state, lamv, g0, win0, bias_s, subg, lng, lnb, wbd, bs_s, wout0,
            g1, win1, wdw, bdw, clng, clnb, wout1, fg):
    nb, dec_seq, d = xs.shape
    rows = nb * dec_seq
    past = cache_k.shape[1]
    pad_rows = bias_s.shape[0] - past
    x2 = xs.reshape(rows, d)
    z = pl.pallas_call(_s_proj_kernel, out_shape=jax.ShapeDtypeStruct((rows, L0_IN), F32),
                       compiler_params=pltpu.CompilerParams(vmem_limit_bytes=VMEM_LIMIT),
                       name="s_proj")(x2, g0, win0)

    ncol = HEADS * 2 * dec_seq
    o = pl.pallas_call(
        functools.partial(_s_attn_kernel, dec_seq=dec_seq, pad_rows=pad_rows),
        grid=(nb,),
        in_specs=[pl.BlockSpec((4, HALF), lambda b: (0, 0)),
                  pl.BlockSpec((dec_seq, L0_IN), lambda b: (b, 0)),
                  pl.BlockSpec((1, past, A_WIDTH), lambda b: (b, 0, 0)),
                  pl.BlockSpec((1, past, A_WIDTH), lambda b: (b, 0, 0)),
                  pl.BlockSpec((past + pad_rows, ncol), lambda b: (0, 0))],
        out_specs=pl.BlockSpec((dec_seq, A_WIDTH), lambda b: (b, 0)),
        out_shape=jax.ShapeDtypeStruct((rows, A_WIDTH), F32),
        compiler_params=pltpu.CompilerParams(dimension_semantics=("arbitrary",), vmem_limit_bytes=VMEM_LIMIT),
        name="s_attn",
    )(lamv, z, cache_k.reshape(nb, past, A_WIDTH), cache_v.reshape(nb, past, A_WIDTH), bias_s)

    state_pad = jnp.pad(state, ((0, 0), (CONV_OFF, 0), (0, 0)))
    y, zv, u = pl.pallas_call(
        functools.partial(_s_rest_kernel, dec_seq=dec_seq),
        out_shape=[jax.ShapeDtypeStruct((rows, d), F32),
                   jax.ShapeDtypeStruct((rows, B_WIDTH), F32),
                   jax.ShapeDtypeStruct((rows, d), F32)],
        scratch_shapes=[pltpu.VMEM((rows, A_WIDTH + B_WIDTH), BF16),
                        pltpu.VMEM((nb * (d // 128), CONV_PAD + dec_seq, 128), F32),
                        pltpu.VMEM((rows, d), F32)],
        compiler_params=pltpu.CompilerParams(vmem_limit_bytes=VMEM_LIMIT),
        name="s_rest",
    )(x2, z, o, subg, lng, lnb, wbd, bs_s, wout0, g1, win1, state_pad, wdw, bdw, clng, clnb, wout1, fg)
    return z, y, zv, u


def kernel(x_prompt, x_sample, cache_k0, cache_v0, state_conv1, rel_bias, norm_g0, w_in0, lambda_q1, lambda_k1,
           lambda_q2, lambda_k2, subln_g0, gv_ln_g0, gv_ln_b0, w_s0, b_s0, w_out0, norm_g1, w_in1, w_dw1, b_dw1,
           conv_ln_g1, conv_ln_b1, w_out1, final_g):
    bsz, t, d = x_prompt.shape
    nb, dec_seq, _ = x_sample.shape
    past = cache_k0.shape[1]
    assert t % TM == 0 and TM % B_CHUNK == 0 and TM % CHUNK == 0 and TM >= MAX_DISTANCE
    assert dec_seq <= B_CHUNK and dec_seq % 8 == 0 and HEADS * 2 * dec_seq == 128
    assert past % CHUNK == 0 and (past + dec_seq - 1) // CHUNK == past // CHUNK

    row = lambda a: a.reshape(1, -1).astype(F32)
    lamv = jnp.stack([lambda_q1, lambda_k1, lambda_q2, lambda_k2]).astype(F32)
    win0, wout0 = w_in0.astype(BF16), w_out0.astype(BF16)
    win1, wout1 = w_in1.astype(BF16), w_out1.astype(BF16)
    g0, g1, fg, subg = row(norm_g0), row(norm_g1), row(final_g), row(subln_g0)
    lng, lnb = row(gv_ln_g0), row(gv_ln_b0)
    bdw, clng, clnb = row(b_dw1), row(conv_ln_g1), row(conv_ln_b1)
    wdw = w_dw1.astype(F32)

    rbs = (rel_bias - rel_bias[FAR_BUCKET:FAR_BUCKET + 1]).astype(F32)
    assert np.all(_np_bucket(-np.arange(TM + 1, 2 * max(t, past + dec_seq))) == FAR_BUCKET)
    kk = np.arange(TM)[:, None]
    qq = np.arange(TM)[None, :]
    own = jnp.where((kk // CHUNK <= qq // CHUNK)[..., None], _lookup(rbs, _np_bucket(kk - qq)), NEG)
    prev = _lookup(rbs, _np_bucket(kk - qq - TM))
    bias_p = jnp.transpose(jnp.stack([own, prev]), (3, 0, 1, 2))

    tril = np.tril(np.ones((B_CHUNK, B_CHUNK), dtype=bool))
    wm = jnp.where(tril[None], w_s0, 0)
    bs = jnp.repeat(b_s0.T.astype(F32), GROUP_DIM, axis=1)

    y0p, k0p, v0p = _l0_prompt(x_prompt, lamv, g0, win0, bias_p, subg, lng, lnb, wm.astype(BF16), bs, wout0)
    y_prompt, tail_p = _l1_prompt(y0p, g1, win1, wdw, bdw, clng, clnb, wout1, fg)

    pad_rows = 128
    kpos = np.arange(past + dec_seq)[:, None]
    qpos = past + np.arange(dec_seq)[None, :]
    bsmp = jnp.transpose(_lookup(rbs, _np_bucket(kpos - qpos)), (0, 2, 1))
    bsmp = jnp.broadcast_to(bsmp[:, :, None, :], (past + dec_seq, HEADS, 2, dec_seq)).reshape(past + dec_seq, -1)
    bias_s = jnp.pad(bsmp, ((0, pad_rows - dec_seq), (0, 0)), constant_values=NEG)
    wbd = jnp.stack([jnp.kron(jnp.eye(nb, dtype=F32), wm[g, :dec_seq, :dec_seq]) for g in range(GROUPS)])
    bs_s = jnp.tile(bs[:dec_seq], (nb, 1))

    zs, ys, zvs, us = _sample(x_sample, cache_k0, cache_v0, state_conv1, lamv, g0, win0, bias_s, subg, lng, lnb,
                              wbd.astype(BF16), bs_s, wout0, g1, win1, wdw, bdw, clng, clnb, wout1, fg)

    keep = CONV_WIDTH - 1
    return (y_prompt,
            ys.reshape(nb, dec_seq, d),
            k0p.reshape(bsz, t, HEADS, HEAD_DIM),
            v0p.reshape(bsz, t, HEADS, HEAD_DIM),
            tail_p[:, CONV_OFF:],
            zs[:, K0:K0 + A_WIDTH].reshape(nb, dec_seq, HEADS, HEAD_DIM),
            zs[:, V0:V0 + A_WIDTH].reshape(nb, dec_seq, HEADS, HEAD_DIM),
            zvs.reshape(nb, dec_seq, B_WIDTH),
            jnp.concatenate([state_conv1, us.reshape(nb, dec_seq, d)], axis=1)[:, -keep:])
```

```python
import functools
import math

import numpy as np
import jax
import jax.numpy as jnp
from jax import lax
from jax.experimental import pallas as pl
from jax.experimental.pallas import tpu as pltpu

F32 = jnp.float32
BF16 = jnp.bfloat16

CHUNK = 64
HEADS = 4
HALF = 64
HEAD_DIM = 2 * HALF
A_WIDTH = HEADS * HEAD_DIM
LAMBDA_INIT = 0.2
N_BUCKETS = 32
MAX_DISTANCE = 128
GROUPS = 4
GROUP_DIM = 128
B_WIDTH = GROUPS * GROUP_DIM
B_CHUNK = 128
CONV_WIDTH = 31
RMS_EPS = 1e-6
LN_EPS = 1e-5

Q0, K0, V0, GA0, UB0, VB0, GB0 = (i * 512 for i in range(7))
L0_IN = 7 * 512

TM = 256
CONV_PAD = 32
CONV_OFF = CONV_PAD - (CONV_WIDTH - 1)
CONV_RB = 64
EXP_ROWS = 64
NEG = -1e30
VMEM_LIMIT = 56 * 1024 * 1024


def _np_bucket(rel):
    nb = N_BUCKETS // 2
    ret = np.where(rel > 0, nb, 0)
    n = np.abs(rel)
    max_exact = nb // 2
    nf = np.maximum(n, 1).astype(np.float32)
    large = max_exact + (np.log(nf / np.float32(max_exact)) / np.float32(math.log(MAX_DISTANCE / max_exact))
                         * np.float32(nb - max_exact)).astype(np.int32)
    large = np.minimum(large, nb - 1)
    return (ret + np.where(n < max_exact, n, large)).astype(np.int32)


FAR_BUCKET = N_BUCKETS // 2 - 1


def _lookup(table, idx):
    idx_c = jnp.asarray(idx)[..., None]
    out = jnp.zeros(idx.shape + (table.shape[1],), F32)
    for b in np.unique(idx):
        out = jnp.where(idx_c == int(b), table[int(b)], out)
    return out


def _rms(x, g):
    return x * lax.rsqrt(jnp.mean(x * x, axis=-1, keepdims=True) + RMS_EPS) * g


def _ln(x, g, b):
    mu = jnp.mean(x, axis=-1, keepdims=True)
    xc = x - mu
    var = jnp.mean(xc * xc, axis=-1, keepdims=True)
    return xc * lax.rsqrt(var + LN_EPS) * g + b


def _lam(lamv):
    s1 = jnp.sum(lamv[0:1] * lamv[1:2], axis=-1, keepdims=True)
    s2 = jnp.sum(lamv[2:3] * lamv[3:4], axis=-1, keepdims=True)
    return jnp.exp(s1) - jnp.exp(s2) + LAMBDA_INIT


def _dot(a, b):
    return jnp.dot(a, b, preferred_element_type=F32)


def _dot_nt(a, b):
    return lax.dot_general(a, b, (((1,), (1,)), ((), ())), preferred_element_type=F32)


def _l0_prompt_kernel(lamv_ref, x_ref, g0_ref, win_ref, bias_ref, subg_ref, lng_ref, lnb_ref, wm_ref, bs_ref,
                      wout_ref, y_ref, k_ref, v_ref,
                      z_ref, kbf_ref, vt_ref, qm_ref, m_ref, l_ref, acc_ref, s_ref, p_ref, mix_ref):
    i = pl.program_id(1)
    x = x_ref[0]
    z_ref[...] = _dot(_rms(x, g0_ref[...]).astype(BF16), win_ref[...])

    k = z_ref[:, K0:K0 + A_WIDTH]
    v = z_ref[:, V0:V0 + A_WIDTH]
    for h in range(HEADS):
        k_ref[0, :, h, :] = k[:, h * HEAD_DIM:(h + 1) * HEAD_DIM]
        v_ref[0, :, h, :] = v[:, h * HEAD_DIM:(h + 1) * HEAD_DIM]
    kbf_ref[i] = k.astype(BF16)
    vt_ref[i] = v.T.astype(BF16)

    lo = lax.broadcasted_iota(jnp.int32, (TM, HEAD_DIM), 1) < HALF
    for h in range(HEADS):
        q = z_ref[:, Q0 + h * HEAD_DIM:Q0 + (h + 1) * HEAD_DIM] * (HALF ** -0.5)
        qm_ref[2 * h] = jnp.where(lo, q, 0.0).astype(BF16)
        qm_ref[2 * h + 1] = jnp.where(lo, 0.0, q).astype(BF16)

    units = range(2 * HEADS)

    def attend(j, bias_idx, first):
        m_new, alpha = {}, {}
        for u in units:
            h = u // 2
            s = _dot_nt(kbf_ref[j, :, h * HEAD_DIM:(h + 1) * HEAD_DIM], qm_ref[u])
            if bias_idx is not None:
                s = s + bias_ref[h, bias_idx]
            s_ref[u] = s
            mt = jnp.max(s, axis=0, keepdims=True)
            if first:
                m_new[u] = mt
            else:
                m_old = m_ref[u]
                m_new[u] = jnp.maximum(m_old, mt)
                alpha[u] = jnp.exp(m_old - m_new[u])
            m_ref[u] = m_new[u]
        for u in units:
            part = jnp.zeros((8, TM), F32)
            for r in range(TM // EXP_ROWS):
                rows = slice(r * EXP_ROWS, (r + 1) * EXP_ROWS)
                p = jnp.exp(s_ref[u, rows, :] - m_new[u])
                part = part + jnp.sum(p.reshape(EXP_ROWS // 8, 8, TM), axis=0)
                p_ref[u, rows, :] = p.astype(BF16)
            lsum = jnp.sum(part, axis=0, keepdims=True)
            l_ref[u] = lsum if first else alpha[u] * l_ref[u] + lsum
        for u in units:
            h = u // 2
            pv = _dot(vt_ref[j, h * HEAD_DIM:(h + 1) * HEAD_DIM, :], p_ref[u])
            acc_ref[u] = pv if first else alpha[u] * acc_ref[u] + pv

    attend(i, 0, True)

    @pl.when(i > 0)
    def _():
        attend(i - 1, 1, False)

    def far(j, carry):
        attend(j, None, False)
        return carry

    lax.fori_loop(0, i - 1, far, 0)

    lam = _lam(lamv_ref[...])
    for h in range(HEADS):
        on1 = acc_ref[2 * h] / l_ref[2 * h]
        on2 = acc_ref[2 * h + 1] / l_ref[2 * h + 1]
        o = (on1 - lam * on2).T
        o = _rms(o, subg_ref[...]) * (1.0 - LAMBDA_INIT)
        ga = z_ref[:, GA0 + h * HEAD_DIM:GA0 + (h + 1) * HEAD_DIM]
        mix_ref[:, h * HEAD_DIM:(h + 1) * HEAD_DIM] = (o * jax.nn.silu(ga)).astype(BF16)

    for g in range(GROUPS):
        sl = slice(g * GROUP_DIM, (g + 1) * GROUP_DIM)
        ub = z_ref[:, UB0 + g * GROUP_DIM:UB0 + (g + 1) * GROUP_DIM]
        vb = z_ref[:, VB0 + g * GROUP_DIM:VB0 + (g + 1) * GROUP_DIM]
        gb = z_ref[:, GB0 + g * GROUP_DIM:GB0 + (g + 1) * GROUP_DIM]
        zv = _ln(jax.nn.gelu(vb), lng_ref[:, sl], lnb_ref[:, sl]).astype(BF16)
        zu = jax.nn.gelu(ub)
        gate = jax.nn.silu(gb)
        for c in range(TM // B_CHUNK):
            rows = slice(c * B_CHUNK, (c + 1) * B_CHUNK)
            sg = _dot(wm_ref[g], zv[rows]) + bs_ref[:, sl]
            mix_ref[rows, A_WIDTH + g * GROUP_DIM:A_WIDTH + (g + 1) * GROUP_DIM] = (
                zu[rows] * sg * gate[rows]).astype(BF16)

    y_ref[0] = x + _dot(mix_ref[...], wout_ref[...])


def _l0_prompt(x, lamv, g0, win, bias, subg, lng, lnb, wm, bs, wout):
    bsz, t, d = x.shape
    nt = t // TM
    const = lambda shape: pl.BlockSpec(shape, lambda b, i: (0,) * len(shape))
    row = lambda w: pl.BlockSpec((1, TM, w), lambda b, i: (b, i, 0))
    kv_row = pl.BlockSpec((1, TM, HEADS, HEAD_DIM), lambda b, i: (b, i, 0, 0))
    return pl.pallas_call(
        _l0_prompt_kernel,
        grid=(bsz, nt),
        in_specs=[const((4, HALF)), row(d), const((1, d)), const((d, L0_IN)),
                  const((HEADS, 2, TM, TM)), const((1, HEAD_DIM)), const((1, B_WIDTH)), const((1, B_WIDTH)),
                  const((GROUPS, B_CHUNK, B_CHUNK)), const((B_CHUNK, B_WIDTH)), const((A_WIDTH + B_WIDTH, d))],
        out_specs=[row(d), kv_row, kv_row],
        out_shape=[jax.ShapeDtypeStruct((bsz, t, d), F32),
                   jax.ShapeDtypeStruct((bsz, t, HEADS, HEAD_DIM), F32),
                   jax.ShapeDtypeStruct((bsz, t, HEADS, HEAD_DIM), F32)],
        scratch_shapes=[pltpu.VMEM((TM, L0_IN), F32),
                        pltpu.VMEM((nt, TM, A_WIDTH), BF16),
                        pltpu.VMEM((nt, A_WIDTH, TM), BF16),
                        pltpu.VMEM((2 * HEADS, TM, HEAD_DIM), BF16),
                        pltpu.VMEM((2 * HEADS, 1, TM), F32),
                        pltpu.VMEM((2 * HEADS, 1, TM), F32),
                        pltpu.VMEM((2 * HEADS, HEAD_DIM, TM), F32),
                        pltpu.VMEM((2 * HEADS, TM, TM), F32),
                        pltpu.VMEM((2 * HEADS, TM, TM), BF16),
                        pltpu.VMEM((TM, A_WIDTH + B_WIDTH), BF16)],
        compiler_params=pltpu.CompilerParams(dimension_semantics=("arbitrary", "arbitrary"),
                                             vmem_limit_bytes=VMEM_LIMIT),
        name="l0_prompt",
    )(lamv, x, g0, win, bias, subg, lng, lnb, wm, bs, wout)


def _conv_rows(buf_ref, c, base, rows, wdw_ref, bdw_ref):
    lanes = slice(c * 128, (c + 1) * 128)
    acc = jnp.broadcast_to(bdw_ref[:, lanes], (rows, 128))
    for tap in range(CONV_WIDTH):
        lo = base + tap + CONV_OFF
        acc = acc + wdw_ref[tap:tap + 1, lanes] * buf_ref[c, lo:lo + rows, :]
    return acc


def _l1_prompt_kernel(x_ref, g1_ref, win_ref, wdw_ref, bdw_ref, clng_ref, clnb_ref, wout_ref, fg_ref,
                      y_ref, tail_ref, z_ref, buf_ref, conv_ref):
    i = pl.program_id(1)
    d = x_ref.shape[-1]
    ngroups = d // 128

    @pl.when(i == 0)
    def _():
        buf_ref[:, 0:CONV_PAD, :] = jnp.zeros((ngroups, CONV_PAD, 128), F32)

    x = x_ref[0]
    z_ref[...] = _dot(_rms(x, g1_ref[...]).astype(BF16), win_ref[...])
    for c in range(ngroups):
        buf_ref[c, CONV_PAD:CONV_PAD + TM, :] = (z_ref[:, c * 128:(c + 1) * 128]
                                                 * jax.nn.sigmoid(z_ref[:, d + c * 128:d + (c + 1) * 128]))
    for c in range(ngroups):
        for r in range(TM // CONV_RB):
            conv_ref[r * CONV_RB:(r + 1) * CONV_RB, c * 128:(c + 1) * 128] = _conv_rows(
                buf_ref, c, r * CONV_RB, CONV_RB, wdw_ref, bdw_ref)

    cn = jax.nn.silu(_ln(conv_ref[...], clng_ref[...], clnb_ref[...])) * jax.nn.silu(z_ref[:, 2 * d:3 * d])
    y1 = x + _dot(cn.astype(BF16), wout_ref[...])
    y_ref[0] = _rms(y1, fg_ref[...])

    for c in range(ngroups):
        tail = buf_ref[c, TM:TM + CONV_PAD, :]
        tail_ref[0, :, c * 128:(c + 1) * 128] = tail
        buf_ref[c, 0:CONV_PAD, :] = tail


def _l1_prompt(x, g1, win, wdw, bdw, clng, clnb, wout, fg):
    bsz, t, d = x.shape
    nt = t // TM
    const = lambda shape: pl.BlockSpec(shape, lambda b, i: (0,) * len(shape))
    row = pl.BlockSpec((1, TM, d), lambda b, i: (b, i, 0))
    return pl.pallas_call(
        _l1_prompt_kernel,
        grid=(bsz, nt),
        in_specs=[row, const((1, d)), const((d, 3 * d)), const((CONV_WIDTH, d)), const((1, d)),
                  const((1, d)), const((1, d)), const((d, d)), const((1, d))],
        out_specs=[row, pl.BlockSpec((1, CONV_PAD, d), lambda b, i: (b, 0, 0))],
        out_shape=[jax.ShapeDtypeStruct((bsz, t, d), F32),
                   jax.ShapeDtypeStruct((bsz, CONV_PAD, d), F32)],
        scratch_shapes=[pltpu.VMEM((TM, 3 * d), F32),
                        pltpu.VMEM((d // 128, CONV_PAD + TM, 128), F32),
                        pltpu.VMEM((TM, d), F32)],
        compiler_params=pltpu.CompilerParams(dimension_semantics=("arbitrary", "arbitrary"),
                                             vmem_limit_bytes=VMEM_LIMIT),
        name="l1_prompt",
    )(x, g1, win, wdw, bdw, clng, clnb, wout, fg)


def _s_proj_kernel(x_ref, g0_ref, win_ref, z_ref):
    z_ref[...] = _dot(_rms(x_ref[...], g0_ref[...]).astype(BF16), win_ref[...])


def _s_attn_kernel(lamv_ref, z_ref, ck_ref, cv_ref, bias_ref, o_ref, *, dec_seq, pad_rows):
    ncol = HEADS * 2 * dec_seq
    q = z_ref[:, Q0:Q0 + A_WIDTH] * (HALF ** -0.5)
    q_rep = jnp.concatenate([q] * (2 * HEADS), axis=0)
    row = lax.broadcasted_iota(jnp.int32, (ncol, A_WIDTH), 0)
    col = lax.broadcasted_iota(jnp.int32, (ncol, A_WIDTH), 1)
    q_bd = jnp.where(col // HALF == row // dec_seq, q_rep, 0.0).astype(BF16)

    zeros = jnp.zeros((pad_rows - dec_seq, A_WIDTH), F32)
    ck = jnp.concatenate([ck_ref[0, :, h, :] for h in range(HEADS)], axis=1)
    cv = jnp.concatenate([cv_ref[0, :, h, :] for h in range(HEADS)], axis=1)
    k_all = jnp.concatenate([ck, z_ref[:, K0:K0 + A_WIDTH], zeros], axis=0).astype(BF16)
    v_all = jnp.concatenate([cv, z_ref[:, V0:V0 + A_WIDTH], zeros], axis=0).astype(BF16)

    s = _dot_nt(k_all, q_bd) + bias_ref[...]
    p = jnp.exp(s - jnp.max(s, axis=0, keepdims=True))
    p = p / jnp.sum(p, axis=0, keepdims=True)
    w = p - _lam(lamv_ref[...]) * pltpu.roll(p, ncol - dec_seq, axis=1)
    o_all = _dot(w.T.astype(BF16), v_all)
    for h in range(HEADS):
        o_ref[:, h * HEAD_DIM:(h + 1) * HEAD_DIM] = o_all[2 * h * dec_seq:(2 * h + 1) * dec_seq,
                                                          h * HEAD_DIM:(h + 1) * HEAD_DIM]


def _s_rest_kernel(x_ref, z_ref, o_ref, subg_ref, lng_ref, lnb_ref, wbd_ref, bs_ref, wout0_ref,
                   g1_ref, win1_ref, st_ref, wdw_ref, bdw_ref, clng_ref, clnb_ref, wout1_ref, fg_ref,
                   y_ref, zv_ref, u_ref, mix_ref, buf_ref, conv_ref, *, dec_seq):
    d = x_ref.shape[-1]
    nb = st_ref.shape[0]
    for h in range(HEADS):
        sl = slice(h * HEAD_DIM, (h + 1) * HEAD_DIM)
        o = _rms(o_ref[:, sl], subg_ref[...]) * (1.0 - LAMBDA_INIT)
        mix_ref[:, sl] = (o * jax.nn.silu(z_ref[:, GA0 + h * HEAD_DIM:GA0 + (h + 1) * HEAD_DIM])).astype(BF16)
    for g in range(GROUPS):
        sl = slice(g * GROUP_DIM, (g + 1) * GROUP_DIM)
        ub = z_ref[:, UB0 + g * GROUP_DIM:UB0 + (g + 1) * GROUP_DIM]
        vb = z_ref[:, VB0 + g * GROUP_DIM:VB0 + (g + 1) * GROUP_DIM]
        gb = z_ref[:, GB0 + g * GROUP_DIM:GB0 + (g + 1) * GROUP_DIM]
        zv = _ln(jax.nn.gelu(vb), lng_ref[:, sl], lnb_ref[:, sl])
        zv_ref[:, sl] = zv
        sg = _dot(wbd_ref[g], zv.astype(BF16)) + bs_ref[:, sl]
        mix_ref[:, A_WIDTH + g * GROUP_DIM:A_WIDTH + (g + 1) * GROUP_DIM] = (
            jax.nn.gelu(ub) * sg * jax.nn.silu(gb)).astype(BF16)
    y0 = x_ref[...] + _dot(mix_ref[...], wout0_ref[...])

    z1 = _dot(_rms(y0, g1_ref[...]).astype(BF16), win1_ref[...])
    u = z1[:, 0:d] * jax.nn.sigmoid(z1[:, d:2 * d])
    u_ref[...] = u
    ngroups = d // 128
    for b in range(nb):
        for c in range(ngroups):
            lanes = slice(c * 128, (c + 1) * 128)
            buf_ref[b * ngroups + c, 0:CONV_PAD, :] = st_ref[b, :, lanes]
            buf_ref[b * ngroups + c, CONV_PAD:CONV_PAD + dec_seq, :] = u[b * dec_seq:(b + 1) * dec_seq, lanes]
    for b in range(nb):
        for c in range(ngroups):
            conv_ref[b * dec_seq:(b + 1) * dec_seq, c * 128:(c + 1) * 128] = _conv_rows(
                buf_ref.at[pl.ds(b * ngroups, ngroups)], c, 0, dec_seq, wdw_ref, bdw_ref)
    c = jax.nn.silu(_ln(conv_ref[...], clng_ref[...], clnb_ref[...])) * jax.nn.silu(z1[:, 2 * d:3 * d])
    y1 = y0 + _dot(c.astype(BF16), wout1_ref[...])
    y_ref[...] = _rms(y1, fg_ref[...])


def _sample(xs, cache_k, cache_v, state, lamv, g0, win0, bias_s, subg, lng, lnb, wbd, bs_s, wout0,
            g1, win1, wdw, bdw, clng, clnb, wout1, fg):
    nb, dec_seq, d = xs.shape
    rows = nb * dec_seq
    past = cache_k.shape[1]
    pad_rows = bias_s.shape[0] - past
    x2 = xs.reshape(rows, d)
    z = pl.pallas_call(_s_proj_kernel, out_shape=jax.ShapeDtypeStruct((rows, L0_IN), F32),
                       compiler_params=pltpu.CompilerParams(vmem_limit_bytes=VMEM_LIMIT),
                       name="s_proj")(x2, g0, win0)

    ncol = HEADS * 2 * dec_seq
    o = pl.pallas_call(
        functools.partial(_s_attn_kernel, dec_seq=dec_seq, pad_rows=pad_rows),
        grid=(nb,),
        in_specs=[pl.BlockSpec((4, HALF), lambda b: (0, 0)),
                  pl.BlockSpec((dec_seq, L0_IN), lambda b: (b, 0)),
                  pl.BlockSpec((1, past, HEADS, HEAD_DIM), lambda b: (b, 0, 0, 0)),
                  pl.BlockSpec((1, past, HEADS, HEAD_DIM), lambda b: (b, 0, 0, 0)),
                  pl.BlockSpec((past + pad_rows, ncol), lambda b: (0, 0))],
        out_specs=pl.BlockSpec((dec_seq, A_WIDTH), lambda b: (b, 0)),
        out_shape=jax.ShapeDtypeStruct((rows, A_WIDTH), F32),
        compiler_params=pltpu.CompilerParams(dimension_semantics=("arbitrary",), vmem_limit_bytes=VMEM_LIMIT),
        name="s_attn",
    )(lamv, z, cache_k, cache_v, bias_s)

    state_pad = jnp.pad(state, ((0, 0), (CONV_OFF, 0), (0, 0)))
    y, zv, u = pl.pallas_call(
        functools.partial(_s_rest_kernel, dec_seq=dec_seq),
        out_shape=[jax.ShapeDtypeStruct((rows, d), F32),
                   jax.ShapeDtypeStruct((rows, B_WIDTH), F32),
                   jax.ShapeDtypeStruct((rows, d), F32)],
        scratch_shapes=[pltpu.VMEM((rows, A_WIDTH + B_WIDTH), BF16),
                        pltpu.VMEM((nb * (d // 128), CONV_PAD + dec_seq, 128), F32),
                        pltpu.VMEM((rows, d), F32)],
        compiler_params=pltpu.CompilerParams(vmem_limit_bytes=VMEM_LIMIT),
        name="s_rest",
    )(x2, z, o, subg, lng, lnb, wbd, bs_s, wout0, g1, win1, state_pad, wdw, bdw, clng, clnb, wout1, fg)
    return z, y, zv, u


def kernel(x_prompt, x_sample, cache_k0, cache_v0, state_conv1, rel_bias, norm_g0, w_in0, lambda_q1, lambda_k1,
           lambda_q2, lambda_k2, subln_g0, gv_ln_g0, gv_ln_b0, w_s0, b_s0, w_out0, norm_g1, w_in1, w_dw1, b_dw1,
           conv_ln_g1, conv_ln_b1, w_out1, final_g):
    bsz, t, d = x_prompt.shape
    nb, dec_seq, _ = x_sample.shape
    past = cache_k0.shape[1]
    assert t % TM == 0 and TM % B_CHUNK == 0 and TM % CHUNK == 0 and TM >= MAX_DISTANCE
    assert dec_seq <= B_CHUNK and dec_seq % 8 == 0 and HEADS * 2 * dec_seq == 128
    assert past % CHUNK == 0 and (past + dec_seq - 1) // CHUNK == past // CHUNK

    row = lambda a: a.reshape(1, -1).astype(F32)
    lamv = jnp.stack([lambda_q1, lambda_k1, lambda_q2, lambda_k2]).astype(F32)
    win0, wout0 = w_in0.astype(BF16), w_out0.astype(BF16)
    win1, wout1 = w_in1.astype(BF16), w_out1.astype(BF16)
    g0, g1, fg, subg = row(norm_g0), row(norm_g1), row(final_g), row(subln_g0)
    lng, lnb = row(gv_ln_g0), row(gv_ln_b0)
    bdw, clng, clnb = row(b_dw1), row(conv_ln_g1), row(conv_ln_b1)
    wdw = w_dw1.astype(F32)

    rbs = (rel_bias - rel_bias[FAR_BUCKET:FAR_BUCKET + 1]).astype(F32)
    assert np.all(_np_bucket(-np.arange(TM + 1, 2 * max(t, past + dec_seq))) == FAR_BUCKET)
    kk = np.arange(TM)[:, None]
    qq = np.arange(TM)[None, :]
    own = jnp.where((kk // CHUNK <= qq // CHUNK)[..., None], _lookup(rbs, _np_bucket(kk - qq)), NEG)
    prev = _lookup(rbs, _np_bucket(kk - qq - TM))
    bias_p = jnp.transpose(jnp.stack([own, prev]), (3, 0, 1, 2))

    tril = np.tril(np.ones((B_CHUNK, B_CHUNK), dtype=bool))
    wm = jnp.where(tril[None], w_s0, 0)
    bs = jnp.repeat(b_s0.T.astype(F32), GROUP_DIM, axis=1)

    y0p, k0p, v0p = _l0_prompt(x_prompt, lamv, g0, win0, bias_p, subg, lng, lnb, wm.astype(BF16), bs, wout0)
    y_prompt, tail_p = _l1_prompt(y0p, g1, win1, wdw, bdw, clng, clnb, wout1, fg)

    pad_rows = 128
    kpos = np.arange(past + dec_seq)[:, None]
    qpos = past + np.arange(dec_seq)[None, :]
    bsmp = jnp.transpose(_lookup(rbs, _np_bucket(kpos - qpos)), (0, 2, 1))
    bsmp = jnp.broadcast_to(bsmp[:, :, None, :], (past + dec_seq, HEADS, 2, dec_seq)).reshape(past + dec_seq, -1)
    bias_s = jnp.pad(bsmp, ((0, pad_rows - dec_seq), (0, 0)), constant_values=NEG)
    wbd = jnp.stack([jnp.kron(jnp.eye(nb, dtype=F32), wm[g, :dec_seq, :dec_seq]) for g in range(GROUPS)])
    bs_s = jnp.tile(bs[:dec_seq], (nb, 1))

    zs, ys, zvs, us = _sample(x_sample, cache_k0, cache_v0, state_conv1, lamv, g0, win0, bias_s, subg, lng, lnb,
                              wbd.astype(BF16), bs_s, wout0, g1, win1, wdw, bdw, clng, clnb, wout1, fg)

    keep = CONV_WIDTH - 1
    return (y_prompt,
            ys.reshape(nb, dec_seq, d),
            k0p,
            v0p,
            tail_p[:, CONV_OFF:],
            zs[:, K0:K0 + A_WIDTH].reshape(nb, dec_seq, HEADS, HEAD_DIM),
            zs[:, V0:V0 + A_WIDTH].reshape(nb, dec_seq, HEADS, HEAD_DIM),
            zvs.reshape(nb, dec_seq, B_WIDTH),
            jnp.concatenate([state_conv1, us.reshape(nb, dec_seq, d)], axis=1)[:, -keep:])
```

```python
import functools
import math

import numpy as np
import jax
import jax.numpy as jnp
from jax import lax
from jax.experimental import pallas as pl
from jax.experimental.pallas import tpu as pltpu

F32 = jnp.float32
BF16 = jnp.bfloat16

CHUNK = 64
HEADS = 4
HALF = 64
HEAD_DIM = 2 * HALF
A_WIDTH = HEADS * HEAD_DIM
LAMBDA_INIT = 0.2
N_BUCKETS = 32
MAX_DISTANCE = 128
GROUPS = 4
GROUP_DIM = 128
B_WIDTH = GROUPS * GROUP_DIM
B_CHUNK = 128
CONV_WIDTH = 31
RMS_EPS = 1e-6
LN_EPS = 1e-5

Q0, K0, V0, GA0, UB0, VB0, GB0 = (i * 512 for i in range(7))
L0_IN = 7 * 512

TM = 256
CONV_PAD = 32
CONV_OFF = CONV_PAD - (CONV_WIDTH - 1)
CONV_RB = 64
EXP_ROWS = 64
VT_ROWS = HEAD_DIM + 16
LOG2E = 1.4426950408889634
Q_SCALE_LOG2 = HALF ** -0.5 * LOG2E
NEG = -1e30
VMEM_LIMIT = 56 * 1024 * 1024


def _np_bucket(rel):
    nb = N_BUCKETS // 2
    ret = np.where(rel > 0, nb, 0)
    n = np.abs(rel)
    max_exact = nb // 2
    nf = np.maximum(n, 1).astype(np.float32)
    large = max_exact + (np.log(nf / np.float32(max_exact)) / np.float32(math.log(MAX_DISTANCE / max_exact))
                         * np.float32(nb - max_exact)).astype(np.int32)
    large = np.minimum(large, nb - 1)
    return (ret + np.where(n < max_exact, n, large)).astype(np.int32)


FAR_BUCKET = N_BUCKETS // 2 - 1


def _lookup(table, idx):
    idx_c = jnp.asarray(idx)[..., None]
    out = jnp.zeros(idx.shape + (table.shape[1],), F32)
    for b in np.unique(idx):
        out = jnp.where(idx_c == int(b), table[int(b)], out)
    return out


def _rms(x, g):
    return x * lax.rsqrt(jnp.mean(x * x, axis=-1, keepdims=True) + RMS_EPS) * g


def _ln(x, g, b):
    mu = jnp.mean(x, axis=-1, keepdims=True)
    xc = x - mu
    var = jnp.mean(xc * xc, axis=-1, keepdims=True)
    return xc * lax.rsqrt(var + LN_EPS) * g + b


def _lam(lamv):
    s1 = jnp.sum(lamv[0:1] * lamv[1:2], axis=-1, keepdims=True)
    s2 = jnp.sum(lamv[2:3] * lamv[3:4], axis=-1, keepdims=True)
    return jnp.exp(s1) - jnp.exp(s2) + LAMBDA_INIT


def _dot(a, b):
    return jnp.dot(a, b, preferred_element_type=F32)


def _dot_nt(a, b):
    return lax.dot_general(a, b, (((1,), (1,)), ((), ())), preferred_element_type=F32)


def _l0_prompt_kernel(lamv_ref, x_ref, g0_ref, win_ref, bias_ref, subg_ref, lng_ref, lnb_ref, wm_ref, bs_ref,
                      wout_ref, y_ref, k_ref, v_ref,
                      z_ref, kbf_ref, vt_ref, qm_ref, m_ref, acc_ref, s_ref, p_ref, mix_ref):
    i = pl.program_id(1)
    x = x_ref[0]
    z_ref[...] = _dot(_rms(x, g0_ref[...]).astype(BF16), win_ref[...])

    k = z_ref[:, K0:K0 + A_WIDTH]
    v = z_ref[:, V0:V0 + A_WIDTH]
    for h in range(HEADS):
        k_ref[0, :, h, :] = k[:, h * HEAD_DIM:(h + 1) * HEAD_DIM]
        v_ref[0, :, h, :] = v[:, h * HEAD_DIM:(h + 1) * HEAD_DIM]
    kbf_ref[i] = k.astype(BF16)
    for h in range(HEADS):
        vt_ref[i, h, 0:HEAD_DIM, :] = v[:, h * HEAD_DIM:(h + 1) * HEAD_DIM].T.astype(BF16)
        vt_ref[i, h, HEAD_DIM:VT_ROWS, :] = jnp.ones((VT_ROWS - HEAD_DIM, TM), BF16)

    lo = lax.broadcasted_iota(jnp.int32, (TM, HEAD_DIM), 1) < HALF
    for h in range(HEADS):
        q = z_ref[:, Q0 + h * HEAD_DIM:Q0 + (h + 1) * HEAD_DIM] * Q_SCALE_LOG2
        qm_ref[2 * h] = jnp.where(lo, q, 0.0).astype(BF16)
        qm_ref[2 * h + 1] = jnp.where(lo, 0.0, q).astype(BF16)

    units = range(2 * HEADS)

    def attend(tiles, first):
        m_new, alpha = {}, {}
        for u in units:
            h = u // 2
            mt = None
            for t, (j, bias_idx) in enumerate(tiles):
                s = _dot_nt(kbf_ref[j, :, h * HEAD_DIM:(h + 1) * HEAD_DIM], qm_ref[u])
                if bias_idx is not None:
                    s = s + bias_ref[h, bias_idx]
                s_ref[t, u] = s
                smax = jnp.max(s, axis=0, keepdims=True)
                mt = smax if mt is None else jnp.maximum(mt, smax)
            if first:
                m_new[u] = mt
            else:
                m_old = m_ref[u]
                m_new[u] = jnp.maximum(m_old, mt)
                alpha[u] = jnp.exp2(m_old - m_new[u])
            m_ref[u] = m_new[u]
        for u in units:
            for t in range(len(tiles)):
                for r in range(TM // EXP_ROWS):
                    rows = slice(r * EXP_ROWS, (r + 1) * EXP_ROWS)
                    p_ref[t, u, rows, :] = jnp.exp2(s_ref[t, u, rows, :] - m_new[u]).astype(BF16)
        for u in units:
            pv = _dot(vt_ref[tiles[0][0], u // 2], p_ref[0, u])
            for t in range(1, len(tiles)):
                pv = pv + _dot(vt_ref[tiles[t][0], u // 2], p_ref[t, u])
            acc_ref[u] = pv if first else alpha[u] * acc_ref[u] + pv

    @pl.when(i == 0)
    def _():
        attend([(i, 0)], True)

    @pl.when(i > 0)
    def _():
        attend([(i, 0), (i - 1, 1)], True)

    def far_pair(jj, carry):
        attend([(2 * jj, None), (2 * jj + 1, None)], False)
        return carry

    n_far = jnp.maximum(i - 1, 0)
    lax.fori_loop(0, n_far // 2, far_pair, 0)

    @pl.when(n_far % 2 == 1)
    def _():
        attend([(n_far - 1, None)], False)

    lam = _lam(lamv_ref[...])
    for h in range(HEADS):
        on1 = acc_ref[2 * h, 0:HEAD_DIM, :] / acc_ref[2 * h, HEAD_DIM:HEAD_DIM + 1, :]
        on2 = acc_ref[2 * h + 1, 0:HEAD_DIM, :] / acc_ref[2 * h + 1, HEAD_DIM:HEAD_DIM + 1, :]
        o = (on1 - lam * on2).T
        o = _rms(o, subg_ref[...]) * (1.0 - LAMBDA_INIT)
        ga = z_ref[:, GA0 + h * HEAD_DIM:GA0 + (h + 1) * HEAD_DIM]
        mix_ref[:, h * HEAD_DIM:(h + 1) * HEAD_DIM] = (o * jax.nn.silu(ga)).astype(BF16)

    for g in range(GROUPS):
        sl = slice(g * GROUP_DIM, (g + 1) * GROUP_DIM)
        ub = z_ref[:, UB0 + g * GROUP_DIM:UB0 + (g + 1) * GROUP_DIM]
        vb = z_ref[:, VB0 + g * GROUP_DIM:VB0 + (g + 1) * GROUP_DIM]
        gb = z_ref[:, GB0 + g * GROUP_DIM:GB0 + (g + 1) * GROUP_DIM]
        zv = _ln(jax.nn.gelu(vb), lng_ref[:, sl], lnb_ref[:, sl]).astype(BF16)
        zu = jax.nn.gelu(ub)
        gate = jax.nn.silu(gb)
        for c in range(TM // B_CHUNK):
            rows = slice(c * B_CHUNK, (c + 1) * B_CHUNK)
            sg = _dot(wm_ref[g], zv[rows]) + bs_ref[:, sl]
            mix_ref[rows, A_WIDTH + g * GROUP_DIM:A_WIDTH + (g + 1) * GROUP_DIM] = (
                zu[rows] * sg * gate[rows]).astype(BF16)

    y_ref[0] = x + _dot(mix_ref[...], wout_ref[...])


def _l0_prompt(x, lamv, g0, win, bias, subg, lng, lnb, wm, bs, wout):
    bsz, t, d = x.shape
    nt = t // TM
    const = lambda shape: pl.BlockSpec(shape, lambda b, i: (0,) * len(shape))
    row = lambda w: pl.BlockSpec((1, TM, w), lambda b, i: (b, i, 0))
    kv_row = pl.BlockSpec((1, TM, HEADS, HEAD_DIM), lambda b, i: (b, i, 0, 0))
    return pl.pallas_call(
        _l0_prompt_kernel,
        grid=(bsz, nt),
        in_specs=[const((4, HALF)), row(d), const((1, d)), const((d, L0_IN)),
                  const((HEADS, 2, TM, TM)), const((1, HEAD_DIM)), const((1, B_WIDTH)), const((1, B_WIDTH)),
                  const((GROUPS, B_CHUNK, B_CHUNK)), const((B_CHUNK, B_WIDTH)), const((A_WIDTH + B_WIDTH, d))],
        out_specs=[row(d), kv_row, kv_row],
        out_shape=[jax.ShapeDtypeStruct((bsz, t, d), F32),
                   jax.ShapeDtypeStruct((bsz, t, HEADS, HEAD_DIM), F32),
                   jax.ShapeDtypeStruct((bsz, t, HEADS, HEAD_DIM), F32)],
        scratch_shapes=[pltpu.VMEM((TM, L0_IN), F32),
                        pltpu.VMEM((nt, TM, A_WIDTH), BF16),
                        pltpu.VMEM((nt, HEADS, VT_ROWS, TM), BF16),
                        pltpu.VMEM((2 * HEADS, TM, HEAD_DIM), BF16),
                        pltpu.VMEM((2 * HEADS, 1, TM), F32),
                        pltpu.VMEM((2 * HEADS, VT_ROWS, TM), F32),
                        pltpu.VMEM((2, 2 * HEADS, TM, TM), F32),
                        pltpu.VMEM((2, 2 * HEADS, TM, TM), BF16),
                        pltpu.VMEM((TM, A_WIDTH + B_WIDTH), BF16)],
        compiler_params=pltpu.CompilerParams(dimension_semantics=("arbitrary", "arbitrary"),
                                             vmem_limit_bytes=VMEM_LIMIT),
        name="l0_prompt",
    )(lamv, x, g0, win, bias, subg, lng, lnb, wm, bs, wout)


def _conv_rows(buf_ref, c, base, rows, wdw_ref, bdw_ref):
    lanes = slice(c * 128, (c + 1) * 128)
    acc = jnp.broadcast_to(bdw_ref[:, lanes], (rows, 128))
    for tap in range(CONV_WIDTH):
        lo = base + tap + CONV_OFF
        acc = acc + wdw_ref[tap:tap + 1, lanes] * buf_ref[c, lo:lo + rows, :]
    return acc


def _l1_prompt_kernel(x_ref, g1_ref, win_ref, wdw_ref, bdw_ref, clng_ref, clnb_ref, wout_ref, fg_ref,
                      y_ref, tail_ref, z_ref, buf_ref, conv_ref):
    i = pl.program_id(1)
    d = x_ref.shape[-1]
    ngroups = d // 128

    @pl.when(i == 0)
    def _():
        buf_ref[:, 0:CONV_PAD, :] = jnp.zeros((ngroups, CONV_PAD, 128), F32)

    x = x_ref[0]
    z_ref[...] = _dot(_rms(x, g1_ref[...]).astype(BF16), win_ref[...])
    for c in range(ngroups):
        buf_ref[c, CONV_PAD:CONV_PAD + TM, :] = (z_ref[:, c * 128:(c + 1) * 128]
                                                 * jax.nn.sigmoid(z_ref[:, d + c * 128:d + (c + 1) * 128]))
    for c in range(ngroups):
        for r in range(TM // CONV_RB):
            conv_ref[r * CONV_RB:(r + 1) * CONV_RB, c * 128:(c + 1) * 128] = _conv_rows(
                buf_ref, c, r * CONV_RB, CONV_RB, wdw_ref, bdw_ref)

    cn = jax.nn.silu(_ln(conv_ref[...], clng_ref[...], clnb_ref[...])) * jax.nn.silu(z_ref[:, 2 * d:3 * d])
    y1 = x + _dot(cn.astype(BF16), wout_ref[...])
    y_ref[0] = _rms(y1, fg_ref[...])

    for c in range(ngroups):
        tail = buf_ref[c, TM:TM + CONV_PAD, :]
        tail_ref[0, :, c * 128:(c + 1) * 128] = tail
        buf_ref[c, 0:CONV_PAD, :] = tail


def _l1_prompt(x, g1, win, wdw, bdw, clng, clnb, wout, fg):
    bsz, t, d = x.shape
    nt = t // TM
    const = lambda shape: pl.BlockSpec(shape, lambda b, i: (0,) * len(shape))
    row = pl.BlockSpec((1, TM, d), lambda b, i: (b, i, 0))
    return pl.pallas_call(
        _l1_prompt_kernel,
        grid=(bsz, nt),
        in_specs=[row, const((1, d)), const((d, 3 * d)), const((CONV_WIDTH, d)), const((1, d)),
                  const((1, d)), const((1, d)), const((d, d)), const((1, d))],
        out_specs=[row, pl.BlockSpec((1, CONV_PAD, d), lambda b, i: (b, 0, 0))],
        out_shape=[jax.ShapeDtypeStruct((bsz, t, d), F32),
                   jax.ShapeDtypeStruct((bsz, CONV_PAD, d), F32)],
        scratch_shapes=[pltpu.VMEM((TM, 3 * d), F32),
                        pltpu.VMEM((d // 128, CONV_PAD + TM, 128), F32),
                        pltpu.VMEM((TM, d), F32)],
        compiler_params=pltpu.CompilerParams(dimension_semantics=("arbitrary", "arbitrary"),
                                             vmem_limit_bytes=VMEM_LIMIT),
        name="l1_prompt",
    )(x, g1, win, wdw, bdw, clng, clnb, wout, fg)


def _s_proj_kernel(x_ref, g0_ref, win_ref, z_ref):
    z_ref[...] = _dot(_rms(x_ref[...], g0_ref[...]).astype(BF16), win_ref[...])


def _s_attn_kernel(lamv_ref, z_ref, ck_ref, cv_ref, bias_ref, o_ref, *, dec_seq, pad_rows):
    ncol = HEADS * 2 * dec_seq
    q = z_ref[:, Q0:Q0 + A_WIDTH] * (HALF ** -0.5)
    q_rep = jnp.concatenate([q] * (2 * HEADS), axis=0)
    row = lax.broadcasted_iota(jnp.int32, (ncol, A_WIDTH), 0)
    col = lax.broadcasted_iota(jnp.int32, (ncol, A_WIDTH), 1)
    q_bd = jnp.where(col // HALF == row // dec_seq, q_rep, 0.0).astype(BF16)

    zeros = jnp.zeros((pad_rows - dec_seq, A_WIDTH), F32)
    ck = jnp.concatenate([ck_ref[0, :, h, :] for h in range(HEADS)], axis=1)
    cv = jnp.concatenate([cv_ref[0, :, h, :] for h in range(HEADS)], axis=1)
    k_all = jnp.concatenate([ck, z_ref[:, K0:K0 + A_WIDTH], zeros], axis=0).astype(BF16)
    v_all = jnp.concatenate([cv, z_ref[:, V0:V0 + A_WIDTH], zeros], axis=0).astype(BF16)

    s = _dot_nt(k_all, q_bd) + bias_ref[...]
    p = jnp.exp(s - jnp.max(s, axis=0, keepdims=True))
    p = p / jnp.sum(p, axis=0, keepdims=True)
    w = p - _lam(lamv_ref[...]) * pltpu.roll(p, ncol - dec_seq, axis=1)
    o_all = _dot(w.T.astype(BF16), v_all)
    for h in range(HEADS):
        o_ref[:, h * HEAD_DIM:(h + 1) * HEAD_DIM] = o_all[2 * h * dec_seq:(2 * h + 1) * dec_seq,
                                                          h * HEAD_DIM:(h + 1) * HEAD_DIM]


def _s_rest_kernel(x_ref, z_ref, o_ref, subg_ref, lng_ref, lnb_ref, wbd_ref, bs_ref, wout0_ref,
                   g1_ref, win1_ref, st_ref, wdw_ref, bdw_ref, clng_ref, clnb_ref, wout1_ref, fg_ref,
                   y_ref, zv_ref, u_ref, mix_ref, buf_ref, conv_ref, *, dec_seq):
    d = x_ref.shape[-1]
    nb = st_ref.shape[0]
    for h in range(HEADS):
        sl = slice(h * HEAD_DIM, (h + 1) * HEAD_DIM)
        o = _rms(o_ref[:, sl], subg_ref[...]) * (1.0 - LAMBDA_INIT)
        mix_ref[:, sl] = (o * jax.nn.silu(z_ref[:, GA0 + h * HEAD_DIM:GA0 + (h + 1) * HEAD_DIM])).astype(BF16)
    for g in range(GROUPS):
        sl = slice(g * GROUP_DIM, (g + 1) * GROUP_DIM)
        ub = z_ref[:, UB0 + g * GROUP_DIM:UB0 + (g + 1) * GROUP_DIM]
        vb = z_ref[:, VB0 + g * GROUP_DIM:VB0 + (g + 1) * GROUP_DIM]
        gb = z_ref[:, GB0 + g * GROUP_DIM:GB0 + (g + 1) * GROUP_DIM]
        zv = _ln(jax.nn.gelu(vb), lng_ref[:, sl], lnb_ref[:, sl])
        zv_ref[:, sl] = zv
        sg = _dot(wbd_ref[g], zv.astype(BF16)) + bs_ref[:, sl]
        mix_ref[:, A_WIDTH + g * GROUP_DIM:A_WIDTH + (g + 1) * GROUP_DIM] = (
            jax.nn.gelu(ub) * sg * jax.nn.silu(gb)).astype(BF16)
    y0 = x_ref[...] + _dot(mix_ref[...], wout0_ref[...])

    z1 = _dot(_rms(y0, g1_ref[...]).astype(BF16), win1_ref[...])
    u = z1[:, 0:d] * jax.nn.sigmoid(z1[:, d:2 * d])
    u_ref[...] = u
    ngroups = d // 128
    for b in range(nb):
        for c in range(ngroups):
            lanes = slice(c * 128, (c + 1) * 128)
            buf_ref[b * ngroups + c, 0:CONV_PAD, :] = st_ref[b, :, lanes]
            buf_ref[b * ngroups + c, CONV_PAD:CONV_PAD + dec_seq, :] = u[b * dec_seq:(b + 1) * dec_seq, lanes]
    for b in range(nb):
        for c in range(ngroups):
            conv_ref[b * dec_seq:(b + 1) * dec_seq, c * 128:(c + 1) * 128] = _conv_rows(
                buf_ref.at[pl.ds(b * ngroups, ngroups)], c, 0, dec_seq, wdw_ref, bdw_ref)
    c = jax.nn.silu(_ln(conv_ref[...], clng_ref[...], clnb_ref[...])) * jax.nn.silu(z1[:, 2 * d:3 * d])
    y1 = y0 + _dot(c.astype(BF16), wout1_ref[...])
    y_ref[...] = _rms(y1, fg_ref[...])


def _sample(xs, cache_k, cache_v, state, lamv, g0, win0, bias_s, subg, lng, lnb, wbd, bs_s, wout0,
            g1, win1, wdw, bdw, clng, clnb, wout1, fg):
    nb, dec_seq, d = xs.shape
    rows = nb * dec_seq
    past = cache_k.shape[1]
    pad_rows = bias_s.shape[0] - past
    x2 = xs.reshape(rows, d)
    z = pl.pallas_call(_s_proj_kernel, out_shape=jax.ShapeDtypeStruct((rows, L0_IN), F32),
                       compiler_params=pltpu.CompilerParams(vmem_limit_bytes=VMEM_LIMIT),
                       name="s_proj")(x2, g0, win0)

    ncol = HEADS * 2 * dec_seq
    o = pl.pallas_call(
        functools.partial(_s_attn_kernel, dec_seq=dec_seq, pad_rows=pad_rows),
        grid=(nb,),
        in_specs=[pl.BlockSpec((4, HALF), lambda b: (0, 0)),
                  pl.BlockSpec((dec_seq, L0_IN), lambda b: (b, 0)),
                  pl.BlockSpec((1, past, HEADS, HEAD_DIM), lambda b: (b, 0, 0, 0)),
                  pl.BlockSpec((1, past, HEADS, HEAD_DIM), lambda b: (b, 0, 0, 0)),
                  pl.BlockSpec((past + pad_rows, ncol), lambda b: (0, 0))],
        out_specs=pl.BlockSpec((dec_seq, A_WIDTH), lambda b: (b, 0)),
        out_shape=jax.ShapeDtypeStruct((rows, A_WIDTH), F32),
        compiler_params=pltpu.CompilerParams(dimension_semantics=("arbitrary",), vmem_limit_bytes=VMEM_LIMIT),
        name="s_attn",
    )(lamv, z, cache_k, cache_v, bias_s)

    state_pad = jnp.pad(state, ((0, 0), (CONV_OFF, 0), (0, 0)))
    y, zv, u = pl.pallas_call(
        functools.partial(_s_rest_kernel, dec_seq=dec_seq),
        out_shape=[jax.ShapeDtypeStruct((rows, d), F32),
                   jax.ShapeDtypeStruct((rows, B_WIDTH), F32),
                   jax.ShapeDtypeStruct((rows, d), F32)],
        scratch_shapes=[pltpu.VMEM((rows, A_WIDTH + B_WIDTH), BF16),
                        pltpu.VMEM((nb * (d // 128), CONV_PAD + dec_seq, 128), F32),
                        pltpu.VMEM((rows, d), F32)],
        compiler_params=pltpu.CompilerParams(vmem_limit_bytes=VMEM_LIMIT),
        name="s_rest",
    )(x2, z, o, subg, lng, lnb, wbd, bs_s, wout0, g1, win1, state_pad, wdw, bdw, clng, clnb, wout1, fg)
    return z, y, zv, u


def kernel(x_prompt, x_sample, cache_k0, cache_v0, state_conv1, rel_bias, norm_g0, w_in0, lambda_q1, lambda_k1,
           lambda_q2, lambda_k2, subln_g0, gv_ln_g0, gv_ln_b0, w_s0, b_s0, w_out0, norm_g1, w_in1, w_dw1, b_dw1,
           conv_ln_g1, conv_ln_b1, w_out1, final_g):
    bsz, t, d = x_prompt.shape
    nb, dec_seq, _ = x_sample.shape
    past = cache_k0.shape[1]
    assert t % TM == 0 and TM % B_CHUNK == 0 and TM % CHUNK == 0 and TM >= MAX_DISTANCE
    assert dec_seq <= B_CHUNK and dec_seq % 8 == 0 and HEADS * 2 * dec_seq == 128
    assert past % CHUNK == 0 and (past + dec_seq - 1) // CHUNK == past // CHUNK

    row = lambda a: a.reshape(1, -1).astype(F32)
    lamv = jnp.stack([lambda_q1, lambda_k1, lambda_q2, lambda_k2]).astype(F32)
    win0, wout0 = w_in0.astype(BF16), w_out0.astype(BF16)
    win1, wout1 = w_in1.astype(BF16), w_out1.astype(BF16)
    g0, g1, fg, subg = row(norm_g0), row(norm_g1), row(final_g), row(subln_g0)
    lng, lnb = row(gv_ln_g0), row(gv_ln_b0)
    bdw, clng, clnb = row(b_dw1), row(conv_ln_g1), row(conv_ln_b1)
    wdw = w_dw1.astype(F32)

    rbs = (rel_bias - rel_bias[FAR_BUCKET:FAR_BUCKET + 1]).astype(F32)
    assert np.all(_np_bucket(-np.arange(TM + 1, 2 * max(t, past + dec_seq))) == FAR_BUCKET)
    kk = np.arange(TM)[:, None]
    qq = np.arange(TM)[None, :]
    own = jnp.where((kk // CHUNK <= qq // CHUNK)[..., None], _lookup(rbs, _np_bucket(kk - qq)), NEG)
    prev = _lookup(rbs, _np_bucket(kk - qq - TM))
    bias_p = jnp.transpose(jnp.stack([own, prev]), (3, 0, 1, 2)) * LOG2E

    tril = np.tril(np.ones((B_CHUNK, B_CHUNK), dtype=bool))
    wm = jnp.where(tril[None], w_s0, 0)
    bs = jnp.repeat(b_s0.T.astype(F32), GROUP_DIM, axis=1)

    y0p, k0p, v0p = _l0_prompt(x_prompt, lamv, g0, win0, bias_p, subg, lng, lnb, wm.astype(BF16), bs, wout0)
    y_prompt, tail_p = _l1_prompt(y0p, g1, win1, wdw, bdw, clng, clnb, wout1, fg)

    pad_rows = 128
    kpos = np.arange(past + dec_seq)[:, None]
    qpos = past + np.arange(dec_seq)[None, :]
    bsmp = jnp.transpose(_lookup(rbs, _np_bucket(kpos - qpos)), (0, 2, 1))
    bsmp = jnp.broadcast_to(bsmp[:, :, None, :], (past + dec_seq, HEADS, 2, dec_seq)).reshape(past + dec_seq, -1)
    bias_s = jnp.pad(bsmp, ((0, pad_rows - dec_seq), (0, 0)), constant_values=NEG)
    wbd = jnp.stack([jnp.kron(jnp.eye(nb, dtype=F32), wm[g, :dec_seq, :dec_seq]) for g in range(GROUPS)])
    bs_s = jnp.tile(bs[:dec_seq], (nb, 1))

    zs, ys, zvs, us = _sample(x_sample, cache_k0, cache_v0, state_conv1, lamv, g0, win0, bias_s, subg, lng, lnb,
                              wbd.astype(BF16), bs_s, wout0, g1, win1, wdw, bdw, clng, clnb, wout1, fg)

    keep = CONV_WIDTH - 1
    return (y_prompt,
            ys.reshape(nb, dec_seq, d),
            k0p,
            v0p,
            tail_p[:, CONV_OFF:],
            zs[:, K0:K0 + A_WIDTH].reshape(nb, dec_seq, HEADS, HEAD_DIM),
            zs[:, V0:V0 + A_WIDTH].reshape(nb, dec_seq, HEADS, HEAD_DIM),
            zvs.reshape(nb, dec_seq, B_WIDTH),
            jnp.concatenate([state_conv1, us.reshape(nb, dec_seq, d)], axis=1)[:, -keep:])
```

```python
import functools
import math

import numpy as np
import jax
import jax.numpy as jnp
from jax import lax
from jax.experimental import pallas as pl
from jax.experimental.pallas import tpu as pltpu

F32 = jnp.float32
BF16 = jnp.bfloat16

CHUNK = 64
HEADS = 4
HALF = 64
HEAD_DIM = 2 * HALF
A_WIDTH = HEADS * HEAD_DIM
LAMBDA_INIT = 0.2
N_BUCKETS = 32
MAX_DISTANCE = 128
GROUPS = 4
GROUP_DIM = 128
B_WIDTH = GROUPS * GROUP_DIM
B_CHUNK = 128
CONV_WIDTH = 31
RMS_EPS = 1e-6
LN_EPS = 1e-5

Q0, K0, V0, GA0, UB0, VB0, GB0 = (i * 512 for i in range(7))
L0_IN = 7 * 512
P_GMLP0 = 0
P_Q0, P_K0, P_V0, P_GA0 = (B_WIDTH * 3 + i * 512 for i in range(4))

TM = 256
CONV_PAD = 32
CONV_OFF = CONV_PAD - (CONV_WIDTH - 1)
CONV_RB = 64
EXP_ROWS = 64
VT_ROWS = HEAD_DIM + 16
LOG2E = 1.4426950408889634
Q_SCALE_LOG2 = HALF ** -0.5 * LOG2E
NEG = -1e30
VMEM_LIMIT = 56 * 1024 * 1024


def _np_bucket(rel):
    nb = N_BUCKETS // 2
    ret = np.where(rel > 0, nb, 0)
    n = np.abs(rel)
    max_exact = nb // 2
    nf = np.maximum(n, 1).astype(np.float32)
    large = max_exact + (np.log(nf / np.float32(max_exact)) / np.float32(math.log(MAX_DISTANCE / max_exact))
                         * np.float32(nb - max_exact)).astype(np.int32)
    large = np.minimum(large, nb - 1)
    return (ret + np.where(n < max_exact, n, large)).astype(np.int32)


FAR_BUCKET = N_BUCKETS // 2 - 1


def _lookup(table, idx):
    idx_c = jnp.asarray(idx)[..., None]
    out = jnp.zeros(idx.shape + (table.shape[1],), F32)
    for b in np.unique(idx):
        out = jnp.where(idx_c == int(b), table[int(b)], out)
    return out


def _rms(x, g):
    return x * lax.rsqrt(jnp.mean(x * x, axis=-1, keepdims=True) + RMS_EPS) * g


def _ln(x, g, b):
    mu = jnp.mean(x, axis=-1, keepdims=True)
    xc = x - mu
    var = jnp.mean(xc * xc, axis=-1, keepdims=True)
    return xc * lax.rsqrt(var + LN_EPS) * g + b


def _lam(lamv):
    s1 = jnp.sum(lamv[0:1] * lamv[1:2], axis=-1, keepdims=True)
    s2 = jnp.sum(lamv[2:3] * lamv[3:4], axis=-1, keepdims=True)
    return jnp.exp(s1) - jnp.exp(s2) + LAMBDA_INIT


def _dot(a, b):
    return jnp.dot(a, b, preferred_element_type=F32)


def _dot_nt(a, b):
    return lax.dot_general(a, b, (((1,), (1,)), ((), ())), preferred_element_type=F32)


def _l0_prompt_kernel(lamv_ref, x_ref, g0_ref, win_ref, bias_ref, subg_ref, lng_ref, lnb_ref, wm_ref, bs_ref,
                      wout_ref, y_ref, k_ref, v_ref,
                      z_ref, kbf_ref, vt_ref, qm_ref, m_ref, acc_ref, s_ref, p_ref, mix_ref):
    i = pl.program_id(1)
    x = x_ref[0]
    z_ref[...] = _dot(_rms(x, g0_ref[...]).astype(BF16), win_ref[...])

    for g in range(GROUPS):
        sl = slice(g * GROUP_DIM, (g + 1) * GROUP_DIM)
        ub = z_ref[:, P_GMLP0 + 3 * g * GROUP_DIM:P_GMLP0 + (3 * g + 1) * GROUP_DIM]
        vb = z_ref[:, P_GMLP0 + (3 * g + 1) * GROUP_DIM:P_GMLP0 + (3 * g + 2) * GROUP_DIM]
        gb = z_ref[:, P_GMLP0 + (3 * g + 2) * GROUP_DIM:P_GMLP0 + (3 * g + 3) * GROUP_DIM]
        zv = _ln(jax.nn.gelu(vb), lng_ref[:, sl], lnb_ref[:, sl]).astype(BF16)
        zu = jax.nn.gelu(ub)
        gate = jax.nn.silu(gb)
        for c in range(TM // B_CHUNK):
            rows = slice(c * B_CHUNK, (c + 1) * B_CHUNK)
            sg = _dot(wm_ref[g], zv[rows]) + bs_ref[:, sl]
            mix_ref[rows, A_WIDTH + g * GROUP_DIM:A_WIDTH + (g + 1) * GROUP_DIM] = (
                zu[rows] * sg * gate[rows]).astype(BF16)

    k = z_ref[:, P_K0:P_K0 + A_WIDTH]
    v = z_ref[:, P_V0:P_V0 + A_WIDTH]
    for h in range(HEADS):
        k_ref[0, :, h, :] = k[:, h * HEAD_DIM:(h + 1) * HEAD_DIM]
        v_ref[0, :, h, :] = v[:, h * HEAD_DIM:(h + 1) * HEAD_DIM]
    kbf_ref[i] = k.astype(BF16)
    for h in range(HEADS):
        vt_ref[i, h, 0:HEAD_DIM, :] = v[:, h * HEAD_DIM:(h + 1) * HEAD_DIM].T.astype(BF16)
        vt_ref[i, h, HEAD_DIM:VT_ROWS, :] = jnp.ones((VT_ROWS - HEAD_DIM, TM), BF16)

    lo = lax.broadcasted_iota(jnp.int32, (TM, HEAD_DIM), 1) < HALF
    for h in range(HEADS):
        q = z_ref[:, P_Q0 + h * HEAD_DIM:P_Q0 + (h + 1) * HEAD_DIM] * Q_SCALE_LOG2
        qm_ref[2 * h] = jnp.where(lo, q, 0.0).astype(BF16)
        qm_ref[2 * h + 1] = jnp.where(lo, 0.0, q).astype(BF16)

    units = range(2 * HEADS)

    def attend(tiles, first):
        m_new, alpha = {}, {}
        for u in units:
            h = u // 2
            mt = None
            for t, (j, bias_idx) in enumerate(tiles):
                s = _dot_nt(kbf_ref[j, :, h * HEAD_DIM:(h + 1) * HEAD_DIM], qm_ref[u])
                if bias_idx is not None:
                    s = s + bias_ref[h, bias_idx]
                s_ref[t, u] = s
                smax = jnp.max(s, axis=0, keepdims=True)
                mt = smax if mt is None else jnp.maximum(mt, smax)
            if first:
                m_new[u] = mt
            else:
                m_old = m_ref[u]
                m_new[u] = jnp.maximum(m_old, mt)
                alpha[u] = jnp.exp2(m_old - m_new[u])
            m_ref[u] = m_new[u]
        for u in units:
            for t in range(len(tiles)):
                for r in range(TM // EXP_ROWS):
                    rows = slice(r * EXP_ROWS, (r + 1) * EXP_ROWS)
                    p_ref[t, u, rows, :] = jnp.exp2(s_ref[t, u, rows, :] - m_new[u]).astype(BF16)
        for u in units:
            pv = _dot(vt_ref[tiles[0][0], u // 2], p_ref[0, u])
            for t in range(1, len(tiles)):
                pv = pv + _dot(vt_ref[tiles[t][0], u // 2], p_ref[t, u])
            acc_ref[u] = pv if first else alpha[u] * acc_ref[u] + pv

    @pl.when(i == 0)
    def _():
        attend([(i, 0)], True)

    @pl.when(i > 0)
    def _():
        attend([(i, 0), (i - 1, 1)], True)

    def far_pair(jj, carry):
        attend([(2 * jj, None), (2 * jj + 1, None)], False)
        return carry

    n_far = jnp.maximum(i - 1, 0)
    lax.fori_loop(0, n_far // 2, far_pair, 0)

    @pl.when(n_far % 2 == 1)
    def _():
        attend([(n_far - 1, None)], False)

    lam = _lam(lamv_ref[...])
    for h in range(HEADS):
        on1 = acc_ref[2 * h, 0:HEAD_DIM, :] / acc_ref[2 * h, HEAD_DIM:HEAD_DIM + 1, :]
        on2 = acc_ref[2 * h + 1, 0:HEAD_DIM, :] / acc_ref[2 * h + 1, HEAD_DIM:HEAD_DIM + 1, :]
        o = (on1 - lam * on2).T
        o = _rms(o, subg_ref[...]) * (1.0 - LAMBDA_INIT)
        ga = z_ref[:, P_GA0 + h * HEAD_DIM:P_GA0 + (h + 1) * HEAD_DIM]
        mix_ref[:, h * HEAD_DIM:(h + 1) * HEAD_DIM] = (o * jax.nn.silu(ga)).astype(BF16)

    y_ref[0] = x + _dot(mix_ref[...], wout_ref[...])


def _l0_prompt(x, lamv, g0, win, bias, subg, lng, lnb, wm, bs, wout):
    bsz, t, d = x.shape
    nt = t // TM
    const = lambda shape: pl.BlockSpec(shape, lambda b, i: (0,) * len(shape))
    row = lambda w: pl.BlockSpec((1, TM, w), lambda b, i: (b, i, 0))
    kv_row = pl.BlockSpec((1, TM, HEADS, HEAD_DIM), lambda b, i: (b, i, 0, 0))
    return pl.pallas_call(
        _l0_prompt_kernel,
        grid=(bsz, nt),
        in_specs=[const((4, HALF)), row(d), const((1, d)), const((d, L0_IN)),
                  const((HEADS, 2, TM, TM)), const((1, HEAD_DIM)), const((1, B_WIDTH)), const((1, B_WIDTH)),
                  const((GROUPS, B_CHUNK, B_CHUNK)), const((B_CHUNK, B_WIDTH)), const((A_WIDTH + B_WIDTH, d))],
        out_specs=[row(d), kv_row, kv_row],
        out_shape=[jax.ShapeDtypeStruct((bsz, t, d), F32),
                   jax.ShapeDtypeStruct((bsz, t, HEADS, HEAD_DIM), F32),
                   jax.ShapeDtypeStruct((bsz, t, HEADS, HEAD_DIM), F32)],
        scratch_shapes=[pltpu.VMEM((TM, L0_IN), F32),
                        pltpu.VMEM((nt, TM, A_WIDTH), BF16),
                        pltpu.VMEM((nt, HEADS, VT_ROWS, TM), BF16),
                        pltpu.VMEM((2 * HEADS, TM, HEAD_DIM), BF16),
                        pltpu.VMEM((2 * HEADS, 1, TM), F32),
                        pltpu.VMEM((2 * HEADS, VT_ROWS, TM), F32),
                        pltpu.VMEM((2, 2 * HEADS, TM, TM), F32),
                        pltpu.VMEM((2, 2 * HEADS, TM, TM), BF16),
                        pltpu.VMEM((TM, A_WIDTH + B_WIDTH), BF16)],
        compiler_params=pltpu.CompilerParams(dimension_semantics=("arbitrary", "arbitrary"),
                                             vmem_limit_bytes=VMEM_LIMIT),
        name="l0_prompt",
    )(lamv, x, g0, win, bias, subg, lng, lnb, wm, bs, wout)


def _conv_rows(buf_ref, c, base, rows, wdw_ref, bdw_ref):
    lanes = slice(c * 128, (c + 1) * 128)
    acc = jnp.broadcast_to(bdw_ref[:, lanes], (rows, 128))
    for tap in range(CONV_WIDTH):
        lo = base + tap + CONV_OFF
        acc = acc + wdw_ref[tap:tap + 1, lanes] * buf_ref[c, lo:lo + rows, :]
    return acc


def _l1_prompt_kernel(x_ref, g1_ref, win_ref, wdw_ref, bdw_ref, clng_ref, clnb_ref, wout_ref, fg_ref,
                      y_ref, tail_ref, z_ref, buf_ref, conv_ref):
    i = pl.program_id(1)
    d = x_ref.shape[-1]
    ngroups = d // 128

    @pl.when(i == 0)
    def _():
        buf_ref[:, 0:CONV_PAD, :] = jnp.zeros((ngroups, CONV_PAD, 128), F32)

    x = x_ref[0]
    xn = _rms(x, g1_ref[...]).astype(BF16)
    z_ref[:, 0:2 * d] = _dot(xn, win_ref[:, 0:2 * d])
    for c in range(ngroups):
        a0 = (c // 2) * 4 * 128 + (c % 2) * 128
        buf_ref[c, CONV_PAD:CONV_PAD + TM, :] = (z_ref[:, a0:a0 + 128]
                                                 * jax.nn.sigmoid(z_ref[:, a0 + 256:a0 + 384]))
    z_ref[:, 2 * d:3 * d] = _dot(xn, win_ref[:, 2 * d:3 * d])
    for c in range(ngroups):
        for r in range(TM // CONV_RB):
            conv_ref[r * CONV_RB:(r + 1) * CONV_RB, c * 128:(c + 1) * 128] = _conv_rows(
                buf_ref, c, r * CONV_RB, CONV_RB, wdw_ref, bdw_ref)

    cn = jax.nn.silu(_ln(conv_ref[...], clng_ref[...], clnb_ref[...])) * jax.nn.silu(z_ref[:, 2 * d:3 * d])
    y1 = x + _dot(cn.astype(BF16), wout_ref[...])
    y_ref[0] = _rms(y1, fg_ref[...])

    for c in range(ngroups):
        tail = buf_ref[c, TM:TM + CONV_PAD, :]
        tail_ref[0, :, c * 128:(c + 1) * 128] = tail
        buf_ref[c, 0:CONV_PAD, :] = tail


def _l1_prompt(x, g1, win, wdw, bdw, clng, clnb, wout, fg):
    bsz, t, d = x.shape
    nt = t // TM
    const = lambda shape: pl.BlockSpec(shape, lambda b, i: (0,) * len(shape))
    row = pl.BlockSpec((1, TM, d), lambda b, i: (b, i, 0))
    return pl.pallas_call(
        _l1_prompt_kernel,
        grid=(bsz, nt),
        in_specs=[row, const((1, d)), const((d, 3 * d)), const((CONV_WIDTH, d)), const((1, d)),
                  const((1, d)), const((1, d)), const((d, d)), const((1, d))],
        out_specs=[row, pl.BlockSpec((1, CONV_PAD, d), lambda b, i: (b, 0, 0))],
        out_shape=[jax.ShapeDtypeStruct((bsz, t, d), F32),
                   jax.ShapeDtypeStruct((bsz, CONV_PAD, d), F32)],
        scratch_shapes=[pltpu.VMEM((TM, 3 * d), F32),
                        pltpu.VMEM((d // 128, CONV_PAD + TM, 128), F32),
                        pltpu.VMEM((TM, d), F32)],
        compiler_params=pltpu.CompilerParams(dimension_semantics=("arbitrary", "arbitrary"),
                                             vmem_limit_bytes=VMEM_LIMIT),
        name="l1_prompt",
    )(x, g1, win, wdw, bdw, clng, clnb, wout, fg)


def _s_proj_kernel(x_ref, g0_ref, win_ref, z_ref):
    z_ref[...] = _dot(_rms(x_ref[...], g0_ref[...]).astype(BF16), win_ref[...])


def _s_attn_kernel(lamv_ref, z_ref, ck_ref, cv_ref, bias_ref, o_ref, *, dec_seq, pad_rows):
    ncol = HEADS * 2 * dec_seq
    q = z_ref[:, Q0:Q0 + A_WIDTH] * (HALF ** -0.5)
    q_rep = jnp.concatenate([q] * (2 * HEADS), axis=0)
    row = lax.broadcasted_iota(jnp.int32, (ncol, A_WIDTH), 0)
    col = lax.broadcasted_iota(jnp.int32, (ncol, A_WIDTH), 1)
    q_bd = jnp.where(col // HALF == row // dec_seq, q_rep, 0.0).astype(BF16)

    zeros = jnp.zeros((pad_rows - dec_seq, A_WIDTH), F32)
    ck = jnp.concatenate([ck_ref[0, :, h, :] for h in range(HEADS)], axis=1)
    cv = jnp.concatenate([cv_ref[0, :, h, :] for h in range(HEADS)], axis=1)
    k_all = jnp.concatenate([ck, z_ref[:, K0:K0 + A_WIDTH], zeros], axis=0).astype(BF16)
    v_all = jnp.concatenate([cv, z_ref[:, V0:V0 + A_WIDTH], zeros], axis=0).astype(BF16)

    s = _dot_nt(k_all, q_bd) + bias_ref[...]
    p = jnp.exp(s - jnp.max(s, axis=0, keepdims=True))
    p = p / jnp.sum(p, axis=0, keepdims=True)
    w = p - _lam(lamv_ref[...]) * pltpu.roll(p, ncol - dec_seq, axis=1)
    o_all = _dot(w.T.astype(BF16), v_all)
    for h in range(HEADS):
        o_ref[:, h * HEAD_DIM:(h + 1) * HEAD_DIM] = o_all[2 * h * dec_seq:(2 * h + 1) * dec_seq,
                                                          h * HEAD_DIM:(h + 1) * HEAD_DIM]


def _s_rest_kernel(x_ref, z_ref, o_ref, subg_ref, lng_ref, lnb_ref, wbd_ref, bs_ref, wout0_ref,
                   g1_ref, win1_ref, st_ref, wdw_ref, bdw_ref, clng_ref, clnb_ref, wout1_ref, fg_ref,
                   y_ref, zv_ref, u_ref, mix_ref, buf_ref, conv_ref, *, dec_seq):
    d = x_ref.shape[-1]
    nb = st_ref.shape[0]
    for h in range(HEADS):
        sl = slice(h * HEAD_DIM, (h + 1) * HEAD_DIM)
        o = _rms(o_ref[:, sl], subg_ref[...]) * (1.0 - LAMBDA_INIT)
        mix_ref[:, sl] = (o * jax.nn.silu(z_ref[:, GA0 + h * HEAD_DIM:GA0 + (h + 1) * HEAD_DIM])).astype(BF16)
    for g in range(GROUPS):
        sl = slice(g * GROUP_DIM, (g + 1) * GROUP_DIM)
        ub = z_ref[:, UB0 + g * GROUP_DIM:UB0 + (g + 1) * GROUP_DIM]
        vb = z_ref[:, VB0 + g * GROUP_DIM:VB0 + (g + 1) * GROUP_DIM]
        gb = z_ref[:, GB0 + g * GROUP_DIM:GB0 + (g + 1) * GROUP_DIM]
        zv = _ln(jax.nn.gelu(vb), lng_ref[:, sl], lnb_ref[:, sl])
        zv_ref[:, sl] = zv
        sg = _dot(wbd_ref[g], zv.astype(BF16)) + bs_ref[:, sl]
        mix_ref[:, A_WIDTH + g * GROUP_DIM:A_WIDTH + (g + 1) * GROUP_DIM] = (
            jax.nn.gelu(ub) * sg * jax.nn.silu(gb)).astype(BF16)
    y0 = x_ref[...] + _dot(mix_ref[...], wout0_ref[...])

    z1 = _dot(_rms(y0, g1_ref[...]).astype(BF16), win1_ref[...])
    u = z1[:, 0:d] * jax.nn.sigmoid(z1[:, d:2 * d])
    u_ref[...] = u
    ngroups = d // 128
    for b in range(nb):
        for c in range(ngroups):
            lanes = slice(c * 128, (c + 1) * 128)
            buf_ref[b * ngroups + c, 0:CONV_PAD, :] = st_ref[b, :, lanes]
            buf_ref[b * ngroups + c, CONV_PAD:CONV_PAD + dec_seq, :] = u[b * dec_seq:(b + 1) * dec_seq, lanes]
    for b in range(nb):
        for c in range(ngroups):
            conv_ref[b * dec_seq:(b + 1) * dec_seq, c * 128:(c + 1) * 128] = _conv_rows(
                buf_ref.at[pl.ds(b * ngroups, ngroups)], c, 0, dec_seq, wdw_ref, bdw_ref)
    c = jax.nn.silu(_ln(conv_ref[...], clng_ref[...], clnb_ref[...])) * jax.nn.silu(z1[:, 2 * d:3 * d])
    y1 = y0 + _dot(c.astype(BF16), wout1_ref[...])
    y_ref[...] = _rms(y1, fg_ref[...])


def _sample(xs, cache_k, cache_v, state, lamv, g0, win0, bias_s, subg, lng, lnb, wbd, bs_s, wout0,
            g1, win1, wdw, bdw, clng, clnb, wout1, fg):
    nb, dec_seq, d = xs.shape
    rows = nb * dec_seq
    past = cache_k.shape[1]
    pad_rows = bias_s.shape[0] - past
    x2 = xs.reshape(rows, d)
    z = pl.pallas_call(_s_proj_kernel, out_shape=jax.ShapeDtypeStruct((rows, L0_IN), F32),
                       compiler_params=pltpu.CompilerParams(vmem_limit_bytes=VMEM_LIMIT),
                       name="s_proj")(x2, g0, win0)

    ncol = HEADS * 2 * dec_seq
    o = pl.pallas_call(
        functools.partial(_s_attn_kernel, dec_seq=dec_seq, pad_rows=pad_rows),
        grid=(nb,),
        in_specs=[pl.BlockSpec((4, HALF), lambda b: (0, 0)),
                  pl.BlockSpec((dec_seq, L0_IN), lambda b: (b, 0)),
                  pl.BlockSpec((1, past, HEADS, HEAD_DIM), lambda b: (b, 0, 0, 0)),
                  pl.BlockSpec((1, past, HEADS, HEAD_DIM), lambda b: (b, 0, 0, 0)),
                  pl.BlockSpec((past + pad_rows, ncol), lambda b: (0, 0))],
        out_specs=pl.BlockSpec((dec_seq, A_WIDTH), lambda b: (b, 0)),
        out_shape=jax.ShapeDtypeStruct((rows, A_WIDTH), F32),
        compiler_params=pltpu.CompilerParams(dimension_semantics=("arbitrary",), vmem_limit_bytes=VMEM_LIMIT),
        name="s_attn",
    )(lamv, z, cache_k, cache_v, bias_s)

    state_pad = jnp.pad(state, ((0, 0), (CONV_OFF, 0), (0, 0)))
    y, zv, u = pl.pallas_call(
        functools.partial(_s_rest_kernel, dec_seq=dec_seq),
        out_shape=[jax.ShapeDtypeStruct((rows, d), F32),
                   jax.ShapeDtypeStruct((rows, B_WIDTH), F32),
                   jax.ShapeDtypeStruct((rows, d), F32)],
        scratch_shapes=[pltpu.VMEM((rows, A_WIDTH + B_WIDTH), BF16),
                        pltpu.VMEM((nb * (d // 128), CONV_PAD + dec_seq, 128), F32),
                        pltpu.VMEM((rows, d), F32)],
        compiler_params=pltpu.CompilerParams(vmem_limit_bytes=VMEM_LIMIT),
        name="s_rest",
    )(x2, z, o, subg, lng, lnb, wbd, bs_s, wout0, g1, win1, state_pad, wdw, bdw, clng, clnb, wout1, fg)
    return z, y, zv, u


def kernel(x_prompt, x_sample, cache_k0, cache_v0, state_conv1, rel_bias, norm_g0, w_in0, lambda_q1, lambda_k1,
           lambda_q2, lambda_k2, subln_g0, gv_ln_g0, gv_ln_b0, w_s0, b_s0, w_out0, norm_g1, w_in1, w_dw1, b_dw1,
           conv_ln_g1, conv_ln_b1, w_out1, final_g):
    bsz, t, d = x_prompt.shape
    nb, dec_seq, _ = x_sample.shape
    past = cache_k0.shape[1]
    assert t % TM == 0 and TM % B_CHUNK == 0 and TM % CHUNK == 0 and TM >= MAX_DISTANCE
    assert dec_seq <= B_CHUNK and dec_seq % 8 == 0 and HEADS * 2 * dec_seq == 128
    assert past % CHUNK == 0 and (past + dec_seq - 1) // CHUNK == past // CHUNK

    row = lambda a: a.reshape(1, -1).astype(F32)
    lamv = jnp.stack([lambda_q1, lambda_k1, lambda_q2, lambda_k2]).astype(F32)
    win0, wout0 = w_in0.astype(BF16), w_out0.astype(BF16)
    gmlp_cols = jnp.stack([win0[:, o:o + B_WIDTH].reshape(d, GROUPS, GROUP_DIM) for o in (UB0, VB0, GB0)], axis=2)
    win0_p = jnp.concatenate([gmlp_cols.reshape(d, 3 * B_WIDTH), win0[:, :UB0]], axis=1)
    win1, wout1 = w_in1.astype(BF16), w_out1.astype(BF16)
    win1_ab = win1[:, :2 * d].reshape(d, 2, d // 256, 256).transpose(0, 2, 1, 3).reshape(d, 2 * d)
    win1_p = jnp.concatenate([win1_ab, win1[:, 2 * d:]], axis=1)
    g0, g1, fg, subg = row(norm_g0), row(norm_g1), row(final_g), row(subln_g0)
    lng, lnb = row(gv_ln_g0), row(gv_ln_b0)
    bdw, clng, clnb = row(b_dw1), row(conv_ln_g1), row(conv_ln_b1)
    wdw = w_dw1.astype(F32)

    rbs = (rel_bias - rel_bias[FAR_BUCKET:FAR_BUCKET + 1]).astype(F32)
    assert np.all(_np_bucket(-np.arange(TM + 1, 2 * max(t, past + dec_seq))) == FAR_BUCKET)
    kk = np.arange(TM)[:, None]
    qq = np.arange(TM)[None, :]
    own = jnp.where((kk // CHUNK <= qq // CHUNK)[..., None], _lookup(rbs, _np_bucket(kk - qq)), NEG)
    prev = _lookup(rbs, _np_bucket(kk - qq - TM))
    bias_p = jnp.transpose(jnp.stack([own, prev]), (3, 0, 1, 2)) * LOG2E

    tril = np.tril(np.ones((B_CHUNK, B_CHUNK), dtype=bool))
    wm = jnp.where(tril[None], w_s0, 0)
    bs = jnp.repeat(b_s0.T.astype(F32), GROUP_DIM, axis=1)

    y0p, k0p, v0p = _l0_prompt(x_prompt, lamv, g0, win0_p, bias_p, subg, lng, lnb, wm.astype(BF16), bs, wout0)
    y_prompt, tail_p = _l1_prompt(y0p, g1, win1_p, wdw, bdw, clng, clnb, wout1, fg)

    pad_rows = 128
    kpos = np.arange(past + dec_seq)[:, None]
    qpos = past + np.arange(dec_seq)[None, :]
    bsmp = jnp.transpose(_lookup(rbs, _np_bucket(kpos - qpos)), (0, 2, 1))
    bsmp = jnp.broadcast_to(bsmp[:, :, None, :], (past + dec_seq, HEADS, 2, dec_seq)).reshape(past + dec_seq, -1)
    bias_s = jnp.pad(bsmp, ((0, pad_rows - dec_seq), (0, 0)), constant_values=NEG)
    wbd = jnp.stack([jnp.kron(jnp.eye(nb, dtype=F32), wm[g, :dec_seq, :dec_seq]) for g in range(GROUPS)])
    bs_s = jnp.tile(bs[:dec_seq], (nb, 1))

    zs, ys, zvs, us = _sample(x_sample, cache_k0, cache_v0, state_conv1, lamv, g0, win0, bias_s, subg, lng, lnb,
                              wbd.astype(BF16), bs_s, wout0, g1, win1, wdw, bdw, clng, clnb, wout1, fg)

    keep = CONV_WIDTH - 1
    return (y_prompt,
            ys.reshape(nb, dec_seq, d),
            k0p,
            v0p,
            tail_p[:, CONV_OFF:],
            zs[:, K0:K0 + A_WIDTH].reshape(nb, dec_seq, HEADS, HEAD_DIM),
            zs[:, V0:V0 + A_WIDTH].reshape(nb, dec_seq, HEADS, HEAD_DIM),
            zvs.reshape(nb, dec_seq, B_WIDTH),
            jnp.concatenate([state_conv1, us.reshape(nb, dec_seq, d)], axis=1)[:, -keep:])
```

```python
import functools
import math

import numpy as np
import jax
import jax.numpy as jnp
from jax import lax
from jax.experimental import pallas as pl
from jax.experimental.pallas import tpu as pltpu

F32 = jnp.float32
BF16 = jnp.bfloat16

CHUNK = 64
HEADS = 4
HALF = 64
HEAD_DIM = 2 * HALF
A_WIDTH = HEADS * HEAD_DIM
LAMBDA_INIT = 0.2
N_BUCKETS = 32
MAX_DISTANCE = 128
GROUPS = 4
GROUP_DIM = 128
B_WIDTH = GROUPS * GROUP_DIM
B_CHUNK = 128
CONV_WIDTH = 31
RMS_EPS = 1e-6
LN_EPS = 1e-5

Q0, K0, V0, GA0, UB0, VB0, GB0 = (i * 512 for i in range(7))
L0_IN = 7 * 512
P_GMLP0 = 0
P_Q0, P_K0, P_V0, P_GA0 = (B_WIDTH * 3 + i * 512 for i in range(4))

TM = 256
CONV_PAD = 32
CONV_OFF = CONV_PAD - (CONV_WIDTH - 1)
CONV_RB = 64
EXP_ROWS = 64
VT_ROWS = HEAD_DIM + 16
LOG2E = 1.4426950408889634
Q_SCALE_LOG2 = HALF ** -0.5 * LOG2E
NEG = -1e30
VMEM_LIMIT = 56 * 1024 * 1024


def _np_bucket(rel):
    nb = N_BUCKETS // 2
    ret = np.where(rel > 0, nb, 0)
    n = np.abs(rel)
    max_exact = nb // 2
    nf = np.maximum(n, 1).astype(np.float32)
    large = max_exact + (np.log(nf / np.float32(max_exact)) / np.float32(math.log(MAX_DISTANCE / max_exact))
                         * np.float32(nb - max_exact)).astype(np.int32)
    large = np.minimum(large, nb - 1)
    return (ret + np.where(n < max_exact, n, large)).astype(np.int32)


FAR_BUCKET = N_BUCKETS // 2 - 1


def _lookup(table, idx):
    idx_c = jnp.asarray(idx)[..., None]
    out = jnp.zeros(idx.shape + (table.shape[1],), F32)
    for b in np.unique(idx):
        out = jnp.where(idx_c == int(b), table[int(b)], out)
    return out


def _rms(x, g):
    return x * lax.rsqrt(jnp.mean(x * x, axis=-1, keepdims=True) + RMS_EPS) * g


def _ln(x, g, b):
    mu = jnp.mean(x, axis=-1, keepdims=True)
    xc = x - mu
    var = jnp.mean(xc * xc, axis=-1, keepdims=True)
    return xc * lax.rsqrt(var + LN_EPS) * g + b


def _lam(lamv):
    s1 = jnp.sum(lamv[0:1] * lamv[1:2], axis=-1, keepdims=True)
    s2 = jnp.sum(lamv[2:3] * lamv[3:4], axis=-1, keepdims=True)
    return jnp.exp(s1) - jnp.exp(s2) + LAMBDA_INIT


def _dot(a, b):
    return jnp.dot(a, b, preferred_element_type=F32)


def _dot_nt(a, b):
    return lax.dot_general(a, b, (((1,), (1,)), ((), ())), preferred_element_type=F32)


def _l0_prompt_kernel(lamv_ref, x_ref, g0_ref, win_ref, bias_ref, subg_ref, lng_ref, lnb_ref, wm_ref, bs_ref,
                      wout_ref, y_ref, k_ref, v_ref,
                      z_ref, kbf_ref, vt_ref, qm_ref, m_ref, acc_ref, s_ref, p_ref, mix_ref):
    i = pl.program_id(1)
    x = x_ref[0]
    z_ref[...] = _dot(_rms(x, g0_ref[...]).astype(BF16), win_ref[...])

    for g in range(GROUPS):
        sl = slice(g * GROUP_DIM, (g + 1) * GROUP_DIM)
        ub = z_ref[:, P_GMLP0 + 3 * g * GROUP_DIM:P_GMLP0 + (3 * g + 1) * GROUP_DIM]
        vb = z_ref[:, P_GMLP0 + (3 * g + 1) * GROUP_DIM:P_GMLP0 + (3 * g + 2) * GROUP_DIM]
        gb = z_ref[:, P_GMLP0 + (3 * g + 2) * GROUP_DIM:P_GMLP0 + (3 * g + 3) * GROUP_DIM]
        zv = _ln(jax.nn.gelu(vb), lng_ref[:, sl], lnb_ref[:, sl]).astype(BF16)
        zu = jax.nn.gelu(ub)
        gate = jax.nn.silu(gb)
        for c in range(TM // B_CHUNK):
            rows = slice(c * B_CHUNK, (c + 1) * B_CHUNK)
            sg = _dot(wm_ref[g], zv[rows]) + bs_ref[:, sl]
            mix_ref[rows, A_WIDTH + g * GROUP_DIM:A_WIDTH + (g + 1) * GROUP_DIM] = (
                zu[rows] * sg * gate[rows]).astype(BF16)

    k = z_ref[:, P_K0:P_K0 + A_WIDTH]
    v = z_ref[:, P_V0:P_V0 + A_WIDTH]
    k_ref[0] = k.reshape(TM, HEADS, HEAD_DIM)
    v_ref[0] = v.reshape(TM, HEADS, HEAD_DIM)
    kbf_ref[i] = k.astype(BF16)
    for h in range(HEADS):
        vt_ref[i, h, 0:HEAD_DIM, :] = v[:, h * HEAD_DIM:(h + 1) * HEAD_DIM].T.astype(BF16)
        vt_ref[i, h, HEAD_DIM:VT_ROWS, :] = jnp.ones((VT_ROWS - HEAD_DIM, TM), BF16)

    lo = lax.broadcasted_iota(jnp.int32, (TM, HEAD_DIM), 1) < HALF
    for h in range(HEADS):
        q = z_ref[:, P_Q0 + h * HEAD_DIM:P_Q0 + (h + 1) * HEAD_DIM] * Q_SCALE_LOG2
        qm_ref[2 * h] = jnp.where(lo, q, 0.0).astype(BF16)
        qm_ref[2 * h + 1] = jnp.where(lo, 0.0, q).astype(BF16)

    units = range(2 * HEADS)

    def attend(tiles, first):
        m_new, alpha = {}, {}
        for u in units:
            h = u // 2
            mt = None
            for t, (j, bias_idx) in enumerate(tiles):
                s = _dot_nt(kbf_ref[j, :, h * HEAD_DIM:(h + 1) * HEAD_DIM], qm_ref[u])
                if bias_idx is not None:
                    s = s + bias_ref[h, bias_idx]
                s_ref[t, u] = s
                smax = jnp.max(s, axis=0, keepdims=True)
                mt = smax if mt is None else jnp.maximum(mt, smax)
            if first:
                m_new[u] = mt
            else:
                m_old = m_ref[u]
                m_new[u] = jnp.maximum(m_old, mt)
                alpha[u] = jnp.exp2(m_old - m_new[u])
            m_ref[u] = m_new[u]
        for u in units:
            for t in range(len(tiles)):
                for r in range(TM // EXP_ROWS):
                    rows = slice(r * EXP_ROWS, (r + 1) * EXP_ROWS)
                    p_ref[t, u, rows, :] = jnp.exp2(s_ref[t, u, rows, :] - m_new[u]).astype(BF16)
        for u in units:
            pv = _dot(vt_ref[tiles[0][0], u // 2], p_ref[0, u])
            for t in range(1, len(tiles)):
                pv = pv + _dot(vt_ref[tiles[t][0], u // 2], p_ref[t, u])
            acc_ref[u] = pv if first else alpha[u] * acc_ref[u] + pv

    @pl.when(i == 0)
    def _():
        attend([(i, 0)], True)

    @pl.when(i > 0)
    def _():
        attend([(i, 0), (i - 1, 1)], True)

    def far_pair(jj, carry):
        attend([(2 * jj, None), (2 * jj + 1, None)], False)
        return carry

    n_far = jnp.maximum(i - 1, 0)
    lax.fori_loop(0, n_far // 2, far_pair, 0)

    @pl.when(n_far % 2 == 1)
    def _():
        attend([(n_far - 1, None)], False)

    lam = _lam(lamv_ref[...])
    for h in range(HEADS):
        on1 = acc_ref[2 * h, 0:HEAD_DIM, :] / acc_ref[2 * h, HEAD_DIM:HEAD_DIM + 1, :]
        on2 = acc_ref[2 * h + 1, 0:HEAD_DIM, :] / acc_ref[2 * h + 1, HEAD_DIM:HEAD_DIM + 1, :]
        o = (on1 - lam * on2).T
        o = _rms(o, subg_ref[...]) * (1.0 - LAMBDA_INIT)
        ga = z_ref[:, P_GA0 + h * HEAD_DIM:P_GA0 + (h + 1) * HEAD_DIM]
        mix_ref[:, h * HEAD_DIM:(h + 1) * HEAD_DIM] = (o * jax.nn.silu(ga)).astype(BF16)

    y_ref[0] = x + _dot(mix_ref[...], wout_ref[...])


def _l0_prompt(x, lamv, g0, win, bias, subg, lng, lnb, wm, bs, wout):
    bsz, t, d = x.shape
    nt = t // TM
    const = lambda shape: pl.BlockSpec(shape, lambda b, i: (0,) * len(shape))
    row = lambda w: pl.BlockSpec((1, TM, w), lambda b, i: (b, i, 0))
    kv_row = pl.BlockSpec((1, TM, HEADS, HEAD_DIM), lambda b, i: (b, i, 0, 0))
    return pl.pallas_call(
        _l0_prompt_kernel,
        grid=(bsz, nt),
        in_specs=[const((4, HALF)), row(d), const((1, d)), const((d, L0_IN)),
                  const((HEADS, 2, TM, TM)), const((1, HEAD_DIM)), const((1, B_WIDTH)), const((1, B_WIDTH)),
                  const((GROUPS, B_CHUNK, B_CHUNK)), const((B_CHUNK, B_WIDTH)), const((A_WIDTH + B_WIDTH, d))],
        out_specs=[row(d), kv_row, kv_row],
        out_shape=[jax.ShapeDtypeStruct((bsz, t, d), F32),
                   jax.ShapeDtypeStruct((bsz, t, HEADS, HEAD_DIM), F32),
                   jax.ShapeDtypeStruct((bsz, t, HEADS, HEAD_DIM), F32)],
        scratch_shapes=[pltpu.VMEM((TM, L0_IN), F32),
                        pltpu.VMEM((nt, TM, A_WIDTH), BF16),
                        pltpu.VMEM((nt, HEADS, VT_ROWS, TM), BF16),
                        pltpu.VMEM((2 * HEADS, TM, HEAD_DIM), BF16),
                        pltpu.VMEM((2 * HEADS, 1, TM), F32),
                        pltpu.VMEM((2 * HEADS, VT_ROWS, TM), F32),
                        pltpu.VMEM((2, 2 * HEADS, TM, TM), F32),
                        pltpu.VMEM((2, 2 * HEADS, TM, TM), BF16),
                        pltpu.VMEM((TM, A_WIDTH + B_WIDTH), BF16)],
        compiler_params=pltpu.CompilerParams(dimension_semantics=("arbitrary", "arbitrary"),
                                             vmem_limit_bytes=VMEM_LIMIT),
        name="l0_prompt",
    )(lamv, x, g0, win, bias, subg, lng, lnb, wm, bs, wout)


def _conv_rows(buf_ref, c, base, rows, wdw_ref, bdw_ref):
    lanes = slice(c * 128, (c + 1) * 128)
    acc = jnp.broadcast_to(bdw_ref[:, lanes], (rows, 128))
    for tap in range(CONV_WIDTH):
        lo = base + tap + CONV_OFF
        acc = acc + wdw_ref[tap:tap + 1, lanes] * buf_ref[c, lo:lo + rows, :]
    return acc


def _l1_prompt_kernel(x_ref, g1_ref, win_ref, wdw_ref, bdw_ref, clng_ref, clnb_ref, wout_ref, fg_ref,
                      y_ref, tail_ref, z_ref, buf_ref, conv_ref):
    i = pl.program_id(1)
    d = x_ref.shape[-1]
    ngroups = d // 128

    @pl.when(i == 0)
    def _():
        buf_ref[:, 0:CONV_PAD, :] = jnp.zeros((ngroups, CONV_PAD, 128), F32)

    x = x_ref[0]
    xn = _rms(x, g1_ref[...]).astype(BF16)
    z_ref[:, 0:2 * d] = _dot(xn, win_ref[:, 0:2 * d])
    for c in range(ngroups):
        a0 = (c // 2) * 4 * 128 + (c % 2) * 128
        buf_ref[c, CONV_PAD:CONV_PAD + TM, :] = (z_ref[:, a0:a0 + 128]
                                                 * jax.nn.sigmoid(z_ref[:, a0 + 256:a0 + 384]))
    z_ref[:, 2 * d:3 * d] = _dot(xn, win_ref[:, 2 * d:3 * d])
    for c in range(ngroups):
        for r in range(TM // CONV_RB):
            conv_ref[r * CONV_RB:(r + 1) * CONV_RB, c * 128:(c + 1) * 128] = _conv_rows(
                buf_ref, c, r * CONV_RB, CONV_RB, wdw_ref, bdw_ref)

    cn = jax.nn.silu(_ln(conv_ref[...], clng_ref[...], clnb_ref[...])) * jax.nn.silu(z_ref[:, 2 * d:3 * d])
    y1 = x + _dot(cn.astype(BF16), wout_ref[...])
    y_ref[0] = _rms(y1, fg_ref[...])

    for c in range(ngroups):
        tail = buf_ref[c, TM:TM + CONV_PAD, :]
        tail_ref[0, :, c * 128:(c + 1) * 128] = tail
        buf_ref[c, 0:CONV_PAD, :] = tail


def _l1_prompt(x, g1, win, wdw, bdw, clng, clnb, wout, fg):
    bsz, t, d = x.shape
    nt = t // TM
    const = lambda shape: pl.BlockSpec(shape, lambda b, i: (0,) * len(shape))
    row = pl.BlockSpec((1, TM, d), lambda b, i: (b, i, 0))
    return pl.pallas_call(
        _l1_prompt_kernel,
        grid=(bsz, nt),
        in_specs=[row, const((1, d)), const((d, 3 * d)), const((CONV_WIDTH, d)), const((1, d)),
                  const((1, d)), const((1, d)), const((d, d)), const((1, d))],
        out_specs=[row, pl.BlockSpec((1, CONV_PAD, d), lambda b, i: (b, 0, 0))],
        out_shape=[jax.ShapeDtypeStruct((bsz, t, d), F32),
                   jax.ShapeDtypeStruct((bsz, CONV_PAD, d), F32)],
        scratch_shapes=[pltpu.VMEM((TM, 3 * d), F32),
                        pltpu.VMEM((d // 128, CONV_PAD + TM, 128), F32),
                        pltpu.VMEM((TM, d), F32)],
        compiler_params=pltpu.CompilerParams(dimension_semantics=("arbitrary", "arbitrary"),
                                             vmem_limit_bytes=VMEM_LIMIT),
        name="l1_prompt",
    )(x, g1, win, wdw, bdw, clng, clnb, wout, fg)


def _s_proj_kernel(x_ref, g0_ref, win_ref, z_ref):
    z_ref[...] = _dot(_rms(x_ref[...], g0_ref[...]).astype(BF16), win_ref[...])


def _s_attn_kernel(lamv_ref, z_ref, ck_ref, cv_ref, bias_ref, o_ref, *, dec_seq, pad_rows):
    ncol = HEADS * 2 * dec_seq
    q = z_ref[:, Q0:Q0 + A_WIDTH] * (HALF ** -0.5)
    q_rep = jnp.concatenate([q] * (2 * HEADS), axis=0)
    row = lax.broadcasted_iota(jnp.int32, (ncol, A_WIDTH), 0)
    col = lax.broadcasted_iota(jnp.int32, (ncol, A_WIDTH), 1)
    q_bd = jnp.where(col // HALF == row // dec_seq, q_rep, 0.0).astype(BF16)

    zeros = jnp.zeros((pad_rows - dec_seq, A_WIDTH), F32)
    ck = ck_ref[0].reshape(ck_ref.shape[1], A_WIDTH)
    cv = cv_ref[0].reshape(cv_ref.shape[1], A_WIDTH)
    k_all = jnp.concatenate([ck, z_ref[:, K0:K0 + A_WIDTH], zeros], axis=0).astype(BF16)
    v_all = jnp.concatenate([cv, z_ref[:, V0:V0 + A_WIDTH], zeros], axis=0).astype(BF16)

    s = _dot_nt(k_all, q_bd) + bias_ref[...]
    p = jnp.exp(s - jnp.max(s, axis=0, keepdims=True))
    p = p / jnp.sum(p, axis=0, keepdims=True)
    w = p - _lam(lamv_ref[...]) * pltpu.roll(p, ncol - dec_seq, axis=1)
    o_all = _dot(w.T.astype(BF16), v_all)
    for h in range(HEADS):
        o_ref[:, h * HEAD_DIM:(h + 1) * HEAD_DIM] = o_all[2 * h * dec_seq:(2 * h + 1) * dec_seq,
                                                          h * HEAD_DIM:(h + 1) * HEAD_DIM]


def _s_rest_kernel(x_ref, z_ref, o_ref, subg_ref, lng_ref, lnb_ref, wbd_ref, bs_ref, wout0_ref,
                   g1_ref, win1_ref, st_ref, wdw_ref, bdw_ref, clng_ref, clnb_ref, wout1_ref, fg_ref,
                   y_ref, zv_ref, u_ref, mix_ref, buf_ref, conv_ref, *, dec_seq):
    d = x_ref.shape[-1]
    nb = st_ref.shape[0]
    for h in range(HEADS):
        sl = slice(h * HEAD_DIM, (h + 1) * HEAD_DIM)
        o = _rms(o_ref[:, sl], subg_ref[...]) * (1.0 - LAMBDA_INIT)
        mix_ref[:, sl] = (o * jax.nn.silu(z_ref[:, GA0 + h * HEAD_DIM:GA0 + (h + 1) * HEAD_DIM])).astype(BF16)
    for g in range(GROUPS):
        sl = slice(g * GROUP_DIM, (g + 1) * GROUP_DIM)
        ub = z_ref[:, UB0 + g * GROUP_DIM:UB0 + (g + 1) * GROUP_DIM]
        vb = z_ref[:, VB0 + g * GROUP_DIM:VB0 + (g + 1) * GROUP_DIM]
        gb = z_ref[:, GB0 + g * GROUP_DIM:GB0 + (g + 1) * GROUP_DIM]
        zv = _ln(jax.nn.gelu(vb), lng_ref[:, sl], lnb_ref[:, sl])
        zv_ref[:, sl] = zv
        sg = _dot(wbd_ref[g], zv.astype(BF16)) + bs_ref[:, sl]
        mix_ref[:, A_WIDTH + g * GROUP_DIM:A_WIDTH + (g + 1) * GROUP_DIM] = (
            jax.nn.gelu(ub) * sg * jax.nn.silu(gb)).astype(BF16)
    y0 = x_ref[...] + _dot(mix_ref[...], wout0_ref[...])

    z1 = _dot(_rms(y0, g1_ref[...]).astype(BF16), win1_ref[...])
    u = z1[:, 0:d] * jax.nn.sigmoid(z1[:, d:2 * d])
    u_ref[...] = u
    ngroups = d // 128
    for b in range(nb):
        for c in range(ngroups):
            lanes = slice(c * 128, (c + 1) * 128)
            buf_ref[b * ngroups + c, 0:CONV_PAD, :] = st_ref[b, :, lanes]
            buf_ref[b * ngroups + c, CONV_PAD:CONV_PAD + dec_seq, :] = u[b * dec_seq:(b + 1) * dec_seq, lanes]
    for b in range(nb):
        for c in range(ngroups):
            conv_ref[b * dec_seq:(b + 1) * dec_seq, c * 128:(c + 1) * 128] = _conv_rows(
                buf_ref.at[pl.ds(b * ngroups, ngroups)], c, 0, dec_seq, wdw_ref, bdw_ref)
    c = jax.nn.silu(_ln(conv_ref[...], clng_ref[...], clnb_ref[...])) * jax.nn.silu(z1[:, 2 * d:3 * d])
    y1 = y0 + _dot(c.astype(BF16), wout1_ref[...])
    y_ref[...] = _rms(y1, fg_ref[...])


def _sample(xs, cache_k, cache_v, state, lamv, g0, win0, bias_s, subg, lng, lnb, wbd, bs_s, wout0,
            g1, win1, wdw, bdw, clng, clnb, wout1, fg):
    nb, dec_seq, d = xs.shape
    rows = nb * dec_seq
    past = cache_k.shape[1]
    pad_rows = bias_s.shape[0] - past
    x2 = xs.reshape(rows, d)
    z = pl.pallas_call(_s_proj_kernel, out_shape=jax.ShapeDtypeStruct((rows, L0_IN), F32),
                       compiler_params=pltpu.CompilerParams(vmem_limit_bytes=VMEM_LIMIT),
                       name="s_proj")(x2, g0, win0)

    ncol = HEADS * 2 * dec_seq
    o = pl.pallas_call(
        functools.partial(_s_attn_kernel, dec_seq=dec_seq, pad_rows=pad_rows),
        grid=(nb,),
        in_specs=[pl.BlockSpec((4, HALF), lambda b: (0, 0)),
                  pl.BlockSpec((dec_seq, L0_IN), lambda b: (b, 0)),
                  pl.BlockSpec((1, past, HEADS, HEAD_DIM), lambda b: (b, 0, 0, 0)),
                  pl.BlockSpec((1, past, HEADS, HEAD_DIM), lambda b: (b, 0, 0, 0)),
                  pl.BlockSpec((past + pad_rows, ncol), lambda b: (0, 0))],
        out_specs=pl.BlockSpec((dec_seq, A_WIDTH), lambda b: (b, 0)),
        out_shape=jax.ShapeDtypeStruct((rows, A_WIDTH), F32),
        compiler_params=pltpu.CompilerParams(dimension_semantics=("arbitrary",), vmem_limit_bytes=VMEM_LIMIT),
        name="s_attn",
    )(lamv, z, cache_k, cache_v, bias_s)

    state_pad = jnp.pad(state, ((0, 0), (CONV_OFF, 0), (0, 0)))
    y, zv, u = pl.pallas_call(
        functools.partial(_s_rest_kernel, dec_seq=dec_seq),
        out_shape=[jax.ShapeDtypeStruct((rows, d), F32),
                   jax.ShapeDtypeStruct((rows, B_WIDTH), F32),
                   jax.ShapeDtypeStruct((rows, d), F32)],
        scratch_shapes=[pltpu.VMEM((rows, A_WIDTH + B_WIDTH), BF16),
                        pltpu.VMEM((nb * (d // 128), CONV_PAD + dec_seq, 128), F32),
                        pltpu.VMEM((rows, d), F32)],
        compiler_params=pltpu.CompilerParams(vmem_limit_bytes=VMEM_LIMIT),
        name="s_rest",
    )(x2, z, o, subg, lng, lnb, wbd, bs_s, wout0, g1, win1, state_pad, wdw, bdw, clng, clnb, wout1, fg)
    return z, y, zv, u


def kernel(x_prompt, x_sample, cache_k0, cache_v0, state_conv1, rel_bias, norm_g0, w_in0, lambda_q1, lambda_k1,
           lambda_q2, lambda_k2, subln_g0, gv_ln_g0, gv_ln_b0, w_s0, b_s0, w_out0, norm_g1, w_in1, w_dw1, b_dw1,
           conv_ln_g1, conv_ln_b1, w_out1, final_g):
    bsz, t, d = x_prompt.shape
    nb, dec_seq, _ = x_sample.shape
    past = cache_k0.shape[1]
    assert t % TM == 0 and TM % B_CHUNK == 0 and TM % CHUNK == 0 and TM >= MAX_DISTANCE
    assert dec_seq <= B_CHUNK and dec_seq % 8 == 0 and HEADS * 2 * dec_seq == 128
    assert past % CHUNK == 0 and (past + dec_seq - 1) // CHUNK == past // CHUNK

    row = lambda a: a.reshape(1, -1).astype(F32)
    lamv = jnp.stack([lambda_q1, lambda_k1, lambda_q2, lambda_k2]).astype(F32)
    win0, wout0 = w_in0.astype(BF16), w_out0.astype(BF16)
    gmlp_cols = jnp.stack([win0[:, o:o + B_WIDTH].reshape(d, GROUPS, GROUP_DIM) for o in (UB0, VB0, GB0)], axis=2)
    win0_p = jnp.concatenate([gmlp_cols.reshape(d, 3 * B_WIDTH), win0[:, :UB0]], axis=1)
    win1, wout1 = w_in1.astype(BF16), w_out1.astype(BF16)
    win1_ab = win1[:, :2 * d].reshape(d, 2, d // 256, 256).transpose(0, 2, 1, 3).reshape(d, 2 * d)
    win1_p = jnp.concatenate([win1_ab, win1[:, 2 * d:]], axis=1)
    g0, g1, fg, subg = row(norm_g0), row(norm_g1), row(final_g), row(subln_g0)
    lng, lnb = row(gv_ln_g0), row(gv_ln_b0)
    bdw, clng, clnb = row(b_dw1), row(conv_ln_g1), row(conv_ln_b1)
    wdw = w_dw1.astype(F32)

    rbs = (rel_bias - rel_bias[FAR_BUCKET:FAR_BUCKET + 1]).astype(F32)
    assert np.all(_np_bucket(-np.arange(TM + 1, 2 * max(t, past + dec_seq))) == FAR_BUCKET)
    kk = np.arange(TM)[:, None]
    qq = np.arange(TM)[None, :]
    own = jnp.where((kk // CHUNK <= qq // CHUNK)[..., None], _lookup(rbs, _np_bucket(kk - qq)), NEG)
    prev = _lookup(rbs, _np_bucket(kk - qq - TM))
    bias_p = jnp.transpose(jnp.stack([own, prev]), (3, 0, 1, 2)) * LOG2E

    tril = np.tril(np.ones((B_CHUNK, B_CHUNK), dtype=bool))
    wm = jnp.where(tril[None], w_s0, 0)
    bs = jnp.repeat(b_s0.T.astype(F32), GROUP_DIM, axis=1)

    y0p, k0p, v0p = _l0_prompt(x_prompt, lamv, g0, win0_p, bias_p, subg, lng, lnb, wm.astype(BF16), bs, wout0)
    y_prompt, tail_p = _l1_prompt(y0p, g1, win1_p, wdw, bdw, clng, clnb, wout1, fg)

    pad_rows = 128
    kpos = np.arange(past + dec_seq)[:, None]
    qpos = past + np.arange(dec_seq)[None, :]
    bsmp = jnp.transpose(_lookup(rbs, _np_bucket(kpos - qpos)), (0, 2, 1))
    bsmp = jnp.broadcast_to(bsmp[:, :, None, :], (past + dec_seq, HEADS, 2, dec_seq)).reshape(past + dec_seq, -1)
    bias_s = jnp.pad(bsmp, ((0, pad_rows - dec_seq), (0, 0)), constant_values=NEG)
    wbd = jnp.stack([jnp.kron(jnp.eye(nb, dtype=F32), wm[g, :dec_seq, :dec_seq]) for g in range(GROUPS)])
    bs_s = jnp.tile(bs[:dec_seq], (nb, 1))

    zs, ys, zvs, us = _sample(x_sample, cache_k0, cache_v0, state_conv1, lamv, g0, win0, bias_s, subg, lng, lnb,
                              wbd.astype(BF16), bs_s, wout0, g1, win1, wdw, bdw, clng, clnb, wout1, fg)

    keep = CONV_WIDTH - 1
    return (y_prompt,
            ys.reshape(nb, dec_seq, d),
            k0p,
            v0p,
            tail_p[:, CONV_OFF:],
            zs[:, K0:K0 + A_WIDTH].reshape(nb, dec_seq, HEADS, HEAD_DIM),
            zs[:, V0:V0 + A_WIDTH].reshape(nb, dec_seq, HEADS, HEAD_DIM),
            zvs.reshape(nb, dec_seq, B_WIDTH),
            jnp.concatenate([state_conv1, us.reshape(nb, dec_seq, d)], axis=1)[:, -keep:])
```

```python
import functools
import math

import numpy as np
import jax
import jax.numpy as jnp
from jax import lax
from jax.experimental import pallas as pl
from jax.experimental.pallas import tpu as pltpu

F32 = jnp.float32
BF16 = jnp.bfloat16

CHUNK = 64
HEADS = 4
HALF = 64
HEAD_DIM = 2 * HALF
A_WIDTH = HEADS * HEAD_DIM
LAMBDA_INIT = 0.2
N_BUCKETS = 32
MAX_DISTANCE = 128
GROUPS = 4
GROUP_DIM = 128
B_WIDTH = GROUPS * GROUP_DIM
B_CHUNK = 128
CONV_WIDTH = 31
RMS_EPS = 1e-6
LN_EPS = 1e-5

Q0, K0, V0, GA0, UB0, VB0, GB0 = (i * 512 for i in range(7))
L0_IN = 7 * 512
P_GMLP0 = 0
P_Q0, P_K0, P_V0, P_GA0 = (B_WIDTH * 3 + i * 512 for i in range(4))

TM = 256
CONV_PAD = 32
CONV_OFF = CONV_PAD - (CONV_WIDTH - 1)
CONV_RB = 64
EXP_ROWS = 64
VT_ROWS = HEAD_DIM + 16
LOG2E = 1.4426950408889634
Q_SCALE_LOG2 = HALF ** -0.5 * LOG2E
NEG = -1e30
VMEM_LIMIT = 56 * 1024 * 1024


def _np_bucket(rel):
    nb = N_BUCKETS // 2
    ret = np.where(rel > 0, nb, 0)
    n = np.abs(rel)
    max_exact = nb // 2
    nf = np.maximum(n, 1).astype(np.float32)
    large = max_exact + (np.log(nf / np.float32(max_exact)) / np.float32(math.log(MAX_DISTANCE / max_exact))
                         * np.float32(nb - max_exact)).astype(np.int32)
    large = np.minimum(large, nb - 1)
    return (ret + np.where(n < max_exact, n, large)).astype(np.int32)


FAR_BUCKET = N_BUCKETS // 2 - 1


def _lookup(table, idx):
    idx_c = jnp.asarray(idx)[..., None]
    out = jnp.zeros(idx.shape + (table.shape[1],), F32)
    for b in np.unique(idx):
        out = jnp.where(idx_c == int(b), table[int(b)], out)
    return out


def _rms(x, g):
    return x * lax.rsqrt(jnp.mean(x * x, axis=-1, keepdims=True) + RMS_EPS) * g


def _ln(x, g, b):
    mu = jnp.mean(x, axis=-1, keepdims=True)
    xc = x - mu
    var = jnp.mean(xc * xc, axis=-1, keepdims=True)
    return xc * lax.rsqrt(var + LN_EPS) * g + b


def _lam(lamv):
    s1 = jnp.sum(lamv[0:1] * lamv[1:2], axis=-1, keepdims=True)
    s2 = jnp.sum(lamv[2:3] * lamv[3:4], axis=-1, keepdims=True)
    return jnp.exp(s1) - jnp.exp(s2) + LAMBDA_INIT


def _gmlp_cols(z_ref, g):
    base = P_GMLP0 + 3 * g * GROUP_DIM
    return tuple(z_ref[:, base + j * GROUP_DIM:base + (j + 1) * GROUP_DIM] for j in range(3))


def _dot(a, b):
    return jnp.dot(a, b, preferred_element_type=F32)


def _dot_nt(a, b):
    return lax.dot_general(a, b, (((1,), (1,)), ((), ())), preferred_element_type=F32)


def _l0_prompt_kernel(lamv_ref, x_ref, g0_ref, win_ref, bias_ref, subg_ref, lng_ref, lnb_ref, wm_ref, bs_ref,
                      wout_ref, y_ref, k_ref, v_ref,
                      z_ref, kbf_ref, vt_ref, qm_ref, m_ref, acc_ref, s_ref, p_ref, mix_ref):
    i = pl.program_id(1)
    x = x_ref[0]
    z_ref[...] = _dot(_rms(x, g0_ref[...]).astype(BF16), win_ref[...])

    for g in range(GROUPS):
        sl = slice(g * GROUP_DIM, (g + 1) * GROUP_DIM)
        ub, vb, gb = _gmlp_cols(z_ref, g)
        zv = _ln(jax.nn.gelu(vb), lng_ref[:, sl], lnb_ref[:, sl]).astype(BF16)
        zu = jax.nn.gelu(ub)
        gate = jax.nn.silu(gb)
        for c in range(TM // B_CHUNK):
            rows = slice(c * B_CHUNK, (c + 1) * B_CHUNK)
            sg = _dot(wm_ref[g], zv[rows]) + bs_ref[:, sl]
            mix_ref[rows, A_WIDTH + g * GROUP_DIM:A_WIDTH + (g + 1) * GROUP_DIM] = (
                zu[rows] * sg * gate[rows]).astype(BF16)

    k = z_ref[:, P_K0:P_K0 + A_WIDTH]
    v = z_ref[:, P_V0:P_V0 + A_WIDTH]
    k_ref[0] = k.reshape(TM, HEADS, HEAD_DIM)
    v_ref[0] = v.reshape(TM, HEADS, HEAD_DIM)
    kbf_ref[i] = k.astype(BF16)
    for h in range(HEADS):
        vt_ref[i, h, 0:HEAD_DIM, :] = v[:, h * HEAD_DIM:(h + 1) * HEAD_DIM].T.astype(BF16)
        vt_ref[i, h, HEAD_DIM:VT_ROWS, :] = jnp.ones((VT_ROWS - HEAD_DIM, TM), BF16)

    lo = lax.broadcasted_iota(jnp.int32, (TM, HEAD_DIM), 1) < HALF
    for h in range(HEADS):
        q = z_ref[:, P_Q0 + h * HEAD_DIM:P_Q0 + (h + 1) * HEAD_DIM] * Q_SCALE_LOG2
        qm_ref[2 * h] = jnp.where(lo, q, 0.0).astype(BF16)
        qm_ref[2 * h + 1] = jnp.where(lo, 0.0, q).astype(BF16)

    units = range(2 * HEADS)

    def attend(tiles, first):
        m_new, alpha = {}, {}
        for u in units:
            h = u // 2
            mt = None
            for t, (j, bias_idx) in enumerate(tiles):
                s = _dot_nt(kbf_ref[j, :, h * HEAD_DIM:(h + 1) * HEAD_DIM], qm_ref[u])
                if bias_idx is not None:
                    s = s + bias_ref[h, bias_idx]
                s_ref[t, u] = s
                smax = jnp.max(s, axis=0, keepdims=True)
                mt = smax if mt is None else jnp.maximum(mt, smax)
            if first:
                m_new[u] = mt
            else:
                m_old = m_ref[u]
                m_new[u] = jnp.maximum(m_old, mt)
                alpha[u] = jnp.exp2(m_old - m_new[u])
            m_ref[u] = m_new[u]
        for u in units:
            for t in range(len(tiles)):
                for r in range(TM // EXP_ROWS):
                    rows = slice(r * EXP_ROWS, (r + 1) * EXP_ROWS)
                    p_ref[t, u, rows, :] = jnp.exp2(s_ref[t, u, rows, :] - m_new[u]).astype(BF16)
        for u in units:
            pv = _dot(vt_ref[tiles[0][0], u // 2], p_ref[0, u])
            for t in range(1, len(tiles)):
                pv = pv + _dot(vt_ref[tiles[t][0], u // 2], p_ref[t, u])
            acc_ref[u] = pv if first else alpha[u] * acc_ref[u] + pv

    @pl.when(i == 0)
    def _():
        attend([(i, 0)], True)

    @pl.when(i > 0)
    def _():
        attend([(i, 0), (i - 1, 1)], True)

    def far_pair(jj, carry):
        attend([(2 * jj, None), (2 * jj + 1, None)], False)
        return carry

    n_far = jnp.maximum(i - 1, 0)
    lax.fori_loop(0, n_far // 2, far_pair, 0)

    @pl.when(n_far % 2 == 1)
    def _():
        attend([(n_far - 1, None)], False)

    y_ref[0] = x + _dot(mix_ref[:, A_WIDTH:], wout_ref[A_WIDTH:, :])

    lam = _lam(lamv_ref[...])
    for h in range(HEADS):
        on1 = acc_ref[2 * h, 0:HEAD_DIM, :] / acc_ref[2 * h, HEAD_DIM:HEAD_DIM + 1, :]
        on2 = acc_ref[2 * h + 1, 0:HEAD_DIM, :] / acc_ref[2 * h + 1, HEAD_DIM:HEAD_DIM + 1, :]
        o = (on1 - lam * on2).T
        o = _rms(o, subg_ref[...]) * (1.0 - LAMBDA_INIT)
        ga = z_ref[:, P_GA0 + h * HEAD_DIM:P_GA0 + (h + 1) * HEAD_DIM]
        mix_ref[:, h * HEAD_DIM:(h + 1) * HEAD_DIM] = (o * jax.nn.silu(ga)).astype(BF16)

    y_ref[0] = y_ref[0] + _dot(mix_ref[:, 0:A_WIDTH], wout_ref[0:A_WIDTH, :])


def _l0_prompt(x, lamv, g0, win, bias, subg, lng, lnb, wm, bs, wout):
    bsz, t, d = x.shape
    nt = t // TM
    const = lambda shape: pl.BlockSpec(shape, lambda b, i: (0,) * len(shape))
    row = lambda w: pl.BlockSpec((1, TM, w), lambda b, i: (b, i, 0))
    kv_row = pl.BlockSpec((1, TM, HEADS, HEAD_DIM), lambda b, i: (b, i, 0, 0))
    return pl.pallas_call(
        _l0_prompt_kernel,
        grid=(bsz, nt),
        in_specs=[const((4, HALF)), row(d), const((1, d)), const((d, L0_IN)),
                  const((HEADS, 2, TM, TM)), const((1, HEAD_DIM)), const((1, B_WIDTH)), const((1, B_WIDTH)),
                  const((GROUPS, B_CHUNK, B_CHUNK)), const((B_CHUNK, B_WIDTH)), const((A_WIDTH + B_WIDTH, d))],
        out_specs=[row(d), kv_row, kv_row],
        out_shape=[jax.ShapeDtypeStruct((bsz, t, d), F32),
                   jax.ShapeDtypeStruct((bsz, t, HEADS, HEAD_DIM), F32),
                   jax.ShapeDtypeStruct((bsz, t, HEADS, HEAD_DIM), F32)],
        scratch_shapes=[pltpu.VMEM((TM, L0_IN), F32),
                        pltpu.VMEM((nt, TM, A_WIDTH), BF16),
                        pltpu.VMEM((nt, HEADS, VT_ROWS, TM), BF16),
                        pltpu.VMEM((2 * HEADS, TM, HEAD_DIM), BF16),
                        pltpu.VMEM((2 * HEADS, 1, TM), F32),
                        pltpu.VMEM((2 * HEADS, VT_ROWS, TM), F32),
                        pltpu.VMEM((2, 2 * HEADS, TM, TM), F32),
                        pltpu.VMEM((2, 2 * HEADS, TM, TM), BF16),
                        pltpu.VMEM((TM, A_WIDTH + B_WIDTH), BF16)],
        compiler_params=pltpu.CompilerParams(dimension_semantics=("arbitrary", "arbitrary"),
                                             vmem_limit_bytes=VMEM_LIMIT),
        name="l0_prompt",
    )(lamv, x, g0, win, bias, subg, lng, lnb, wm, bs, wout)


def _conv_rows(buf_ref, c, base, rows, wdw_ref, bdw_ref):
    lanes = slice(c * 128, (c + 1) * 128)
    acc = jnp.broadcast_to(bdw_ref[:, lanes], (rows, 128))
    for tap in range(CONV_WIDTH):
        lo = base + tap + CONV_OFF
        acc = acc + wdw_ref[tap:tap + 1, lanes] * buf_ref[c, lo:lo + rows, :]
    return acc


def _l1_prompt_kernel(x_ref, g1_ref, win_ref, wdw_ref, bdw_ref, clng_ref, clnb_ref, wout_ref, fg_ref,
                      y_ref, tail_ref, z_ref, buf_ref, conv_ref):
    i = pl.program_id(1)
    d = x_ref.shape[-1]
    ngroups = d // 128

    @pl.when(i == 0)
    def _():
        buf_ref[:, 0:CONV_PAD, :] = jnp.zeros((ngroups, CONV_PAD, 128), F32)

    x = x_ref[0]
    xn = _rms(x, g1_ref[...]).astype(BF16)
    z_ref[:, 0:2 * d] = _dot(xn, win_ref[:, 0:2 * d])
    for c in range(ngroups):
        a0 = (c // 2) * 4 * 128 + (c % 2) * 128
        buf_ref[c, CONV_PAD:CONV_PAD + TM, :] = (z_ref[:, a0:a0 + 128]
                                                 * jax.nn.sigmoid(z_ref[:, a0 + 256:a0 + 384]))
    z_ref[:, 2 * d:3 * d] = _dot(xn, win_ref[:, 2 * d:3 * d])
    for c in range(ngroups):
        for r in range(TM // CONV_RB):
            conv_ref[r * CONV_RB:(r + 1) * CONV_RB, c * 128:(c + 1) * 128] = _conv_rows(
                buf_ref, c, r * CONV_RB, CONV_RB, wdw_ref, bdw_ref)

    cn = jax.nn.silu(_ln(conv_ref[...], clng_ref[...], clnb_ref[...])) * jax.nn.silu(z_ref[:, 2 * d:3 * d])
    y1 = x + _dot(cn.astype(BF16), wout_ref[...])
    y_ref[0] = _rms(y1, fg_ref[...])

    for c in range(ngroups):
        tail = buf_ref[c, TM:TM + CONV_PAD, :]
        tail_ref[0, :, c * 128:(c + 1) * 128] = tail
        buf_ref[c, 0:CONV_PAD, :] = tail


def _l1_prompt(x, g1, win, wdw, bdw, clng, clnb, wout, fg):
    bsz, t, d = x.shape
    nt = t // TM
    const = lambda shape: pl.BlockSpec(shape, lambda b, i: (0,) * len(shape))
    row = pl.BlockSpec((1, TM, d), lambda b, i: (b, i, 0))
    return pl.pallas_call(
        _l1_prompt_kernel,
        grid=(bsz, nt),
        in_specs=[row, const((1, d)), const((d, 3 * d)), const((CONV_WIDTH, d)), const((1, d)),
                  const((1, d)), const((1, d)), const((d, d)), const((1, d))],
        out_specs=[row, pl.BlockSpec((1, CONV_PAD, d), lambda b, i: (b, 0, 0))],
        out_shape=[jax.ShapeDtypeStruct((bsz, t, d), F32),
                   jax.ShapeDtypeStruct((bsz, CONV_PAD, d), F32)],
        scratch_shapes=[pltpu.VMEM((TM, 3 * d), F32),
                        pltpu.VMEM((d // 128, CONV_PAD + TM, 128), F32),
                        pltpu.VMEM((TM, d), F32)],
        compiler_params=pltpu.CompilerParams(dimension_semantics=("arbitrary", "arbitrary"),
                                             vmem_limit_bytes=VMEM_LIMIT),
        name="l1_prompt",
    )(x, g1, win, wdw, bdw, clng, clnb, wout, fg)


def _s_proj_kernel(x_ref, g0_ref, win_ref, z_ref):
    z_ref[...] = _dot(_rms(x_ref[...], g0_ref[...]).astype(BF16), win_ref[...])


def _s_attn_kernel(lamv_ref, z_ref, ck_ref, cv_ref, bias_ref, o_ref, *, dec_seq, pad_rows):
    ncol = HEADS * 2 * dec_seq
    q = z_ref[:, P_Q0:P_Q0 + A_WIDTH] * (HALF ** -0.5)
    q_rep = jnp.concatenate([q] * (2 * HEADS), axis=0)
    row = lax.broadcasted_iota(jnp.int32, (ncol, A_WIDTH), 0)
    col = lax.broadcasted_iota(jnp.int32, (ncol, A_WIDTH), 1)
    q_bd = jnp.where(col // HALF == row // dec_seq, q_rep, 0.0).astype(BF16)

    zeros = jnp.zeros((pad_rows - dec_seq, A_WIDTH), F32)
    ck = ck_ref[0].reshape(ck_ref.shape[1], A_WIDTH)
    cv = cv_ref[0].reshape(cv_ref.shape[1], A_WIDTH)
    k_all = jnp.concatenate([ck, z_ref[:, P_K0:P_K0 + A_WIDTH], zeros], axis=0).astype(BF16)
    v_all = jnp.concatenate([cv, z_ref[:, P_V0:P_V0 + A_WIDTH], zeros], axis=0).astype(BF16)

    s = _dot_nt(k_all, q_bd) + bias_ref[...]
    p = jnp.exp(s - jnp.max(s, axis=0, keepdims=True))
    p = p / jnp.sum(p, axis=0, keepdims=True)
    w = p - _lam(lamv_ref[...]) * pltpu.roll(p, ncol - dec_seq, axis=1)
    o_all = _dot(w.T.astype(BF16), v_all)
    for h in range(HEADS):
        o_ref[:, h * HEAD_DIM:(h + 1) * HEAD_DIM] = o_all[2 * h * dec_seq:(2 * h + 1) * dec_seq,
                                                          h * HEAD_DIM:(h + 1) * HEAD_DIM]


def _s_rest_kernel(x_ref, z_ref, o_ref, subg_ref, lng_ref, lnb_ref, wbd_ref, bs_ref, wout0_ref,
                   g1_ref, win1_ref, st_ref, wdw_ref, bdw_ref, clng_ref, clnb_ref, wout1_ref, fg_ref,
                   y_ref, zv_ref, u_ref, mix_ref, buf_ref, conv_ref, *, dec_seq):
    d = x_ref.shape[-1]
    nb = st_ref.shape[0]
    for h in range(HEADS):
        sl = slice(h * HEAD_DIM, (h + 1) * HEAD_DIM)
        o = _rms(o_ref[:, sl], subg_ref[...]) * (1.0 - LAMBDA_INIT)
        mix_ref[:, sl] = (o * jax.nn.silu(z_ref[:, P_GA0 + h * HEAD_DIM:P_GA0 + (h + 1) * HEAD_DIM])).astype(BF16)
    for g in range(GROUPS):
        sl = slice(g * GROUP_DIM, (g + 1) * GROUP_DIM)
        ub, vb, gb = _gmlp_cols(z_ref, g)
        zv = _ln(jax.nn.gelu(vb), lng_ref[:, sl], lnb_ref[:, sl])
        zv_ref[:, sl] = zv
        sg = _dot(wbd_ref[g], zv.astype(BF16)) + bs_ref[:, sl]
        mix_ref[:, A_WIDTH + g * GROUP_DIM:A_WIDTH + (g + 1) * GROUP_DIM] = (
            jax.nn.gelu(ub) * sg * jax.nn.silu(gb)).astype(BF16)
    y0 = x_ref[...] + _dot(mix_ref[...], wout0_ref[...])

    z1 = _dot(_rms(y0, g1_ref[...]).astype(BF16), win1_ref[...])
    u = jnp.concatenate([z1[:, a0:a0 + 256] * jax.nn.sigmoid(z1[:, a0 + 256:a0 + 512])
                         for a0 in range(0, 2 * d, 512)], axis=1)
    u_ref[...] = u
    ngroups = d // 128
    for b in range(nb):
        for c in range(ngroups):
            lanes = slice(c * 128, (c + 1) * 128)
            buf_ref[b * ngroups + c, 0:CONV_PAD, :] = st_ref[b, :, lanes]
            buf_ref[b * ngroups + c, CONV_PAD:CONV_PAD + dec_seq, :] = u[b * dec_seq:(b + 1) * dec_seq, lanes]
    for b in range(nb):
        for c in range(ngroups):
            conv_ref[b * dec_seq:(b + 1) * dec_seq, c * 128:(c + 1) * 128] = _conv_rows(
                buf_ref.at[pl.ds(b * ngroups, ngroups)], c, 0, dec_seq, wdw_ref, bdw_ref)
    c = jax.nn.silu(_ln(conv_ref[...], clng_ref[...], clnb_ref[...])) * jax.nn.silu(z1[:, 2 * d:3 * d])
    y1 = y0 + _dot(c.astype(BF16), wout1_ref[...])
    y_ref[...] = _rms(y1, fg_ref[...])


def _sample(xs, cache_k, cache_v, state, lamv, g0, win0, bias_s, subg, lng, lnb, wbd, bs_s, wout0,
            g1, win1, wdw, bdw, clng, clnb, wout1, fg):
    nb, dec_seq, d = xs.shape
    rows = nb * dec_seq
    past = cache_k.shape[1]
    pad_rows = bias_s.shape[0] - past
    x2 = xs.reshape(rows, d)
    z = pl.pallas_call(_s_proj_kernel, out_shape=jax.ShapeDtypeStruct((rows, L0_IN), F32),
                       compiler_params=pltpu.CompilerParams(vmem_limit_bytes=VMEM_LIMIT),
                       name="s_proj")(x2, g0, win0)

    ncol = HEADS * 2 * dec_seq
    o = pl.pallas_call(
        functools.partial(_s_attn_kernel, dec_seq=dec_seq, pad_rows=pad_rows),
        grid=(nb,),
        in_specs=[pl.BlockSpec((4, HALF), lambda b: (0, 0)),
                  pl.BlockSpec((dec_seq, L0_IN), lambda b: (b, 0)),
                  pl.BlockSpec((1, past, HEADS, HEAD_DIM), lambda b: (b, 0, 0, 0)),
                  pl.BlockSpec((1, past, HEADS, HEAD_DIM), lambda b: (b, 0, 0, 0)),
                  pl.BlockSpec((past + pad_rows, ncol), lambda b: (0, 0))],
        out_specs=pl.BlockSpec((dec_seq, A_WIDTH), lambda b: (b, 0)),
        out_shape=jax.ShapeDtypeStruct((rows, A_WIDTH), F32),
        compiler_params=pltpu.CompilerParams(dimension_semantics=("arbitrary",), vmem_limit_bytes=VMEM_LIMIT),
        name="s_attn",
    )(lamv, z, cache_k, cache_v, bias_s)

    state_pad = jnp.pad(state, ((0, 0), (CONV_OFF, 0), (0, 0)))
    y, zv, u = pl.pallas_call(
        functools.partial(_s_rest_kernel, dec_seq=dec_seq),
        out_shape=[jax.ShapeDtypeStruct((rows, d), F32),
                   jax.ShapeDtypeStruct((rows, B_WIDTH), F32),
                   jax.ShapeDtypeStruct((rows, d), F32)],
        scratch_shapes=[pltpu.VMEM((rows, A_WIDTH + B_WIDTH), BF16),
                        pltpu.VMEM((nb * (d // 128), CONV_PAD + dec_seq, 128), F32),
                        pltpu.VMEM((rows, d), F32)],
        compiler_params=pltpu.CompilerParams(vmem_limit_bytes=VMEM_LIMIT),
        name="s_rest",
    )(x2, z, o, subg, lng, lnb, wbd, bs_s, wout0, g1, win1, state_pad, wdw, bdw, clng, clnb, wout1, fg)
    return z, y, zv, u


def kernel(x_prompt, x_sample, cache_k0, cache_v0, state_conv1, rel_bias, norm_g0, w_in0, lambda_q1, lambda_k1,
           lambda_q2, lambda_k2, subln_g0, gv_ln_g0, gv_ln_b0, w_s0, b_s0, w_out0, norm_g1, w_in1, w_dw1, b_dw1,
           conv_ln_g1, conv_ln_b1, w_out1, final_g):
    bsz, t, d = x_prompt.shape
    nb, dec_seq, _ = x_sample.shape
    past = cache_k0.shape[1]
    assert t % TM == 0 and TM % B_CHUNK == 0 and TM % CHUNK == 0 and TM >= MAX_DISTANCE
    assert dec_seq <= B_CHUNK and dec_seq % 8 == 0 and HEADS * 2 * dec_seq == 128
    assert past % CHUNK == 0 and (past + dec_seq - 1) // CHUNK == past // CHUNK

    row = lambda a: a.reshape(1, -1).astype(F32)
    lamv = jnp.stack([lambda_q1, lambda_k1, lambda_q2, lambda_k2]).astype(F32)
    wout0, wout1 = w_out0.astype(BF16), w_out1.astype(BF16)
    cols0 = [w_in0[:, o + g * GROUP_DIM:o + (g + 1) * GROUP_DIM] for g in range(GROUPS) for o in (UB0, VB0, GB0)]
    win0 = jnp.concatenate(cols0 + [w_in0[:, :UB0]], axis=1).astype(BF16)
    cols1 = [w_in1[:, o + p * 256:o + (p + 1) * 256] for p in range(d // 256) for o in (0, d)]
    win1 = jnp.concatenate(cols1 + [w_in1[:, 2 * d:]], axis=1).astype(BF16)
    g0, g1, fg, subg = row(norm_g0), row(norm_g1), row(final_g), row(subln_g0)
    lng, lnb = row(gv_ln_g0), row(gv_ln_b0)
    bdw, clng, clnb = row(b_dw1), row(conv_ln_g1), row(conv_ln_b1)
    wdw = w_dw1.astype(F32)

    rbs = (rel_bias - rel_bias[FAR_BUCKET:FAR_BUCKET + 1]).astype(F32)
    assert np.all(_np_bucket(-np.arange(TM + 1, 2 * max(t, past + dec_seq))) == FAR_BUCKET)
    kk = np.arange(TM)[:, None]
    qq = np.arange(TM)[None, :]
    own = jnp.where((kk // CHUNK <= qq // CHUNK)[..., None], _lookup(rbs, _np_bucket(kk - qq)), NEG)
    prev = _lookup(rbs, _np_bucket(kk - qq - TM))
    bias_p = jnp.transpose(jnp.stack([own, prev]), (3, 0, 1, 2)) * LOG2E

    tril = np.tril(np.ones((B_CHUNK, B_CHUNK), dtype=bool))
    wm = jnp.where(tril[None], w_s0, 0)
    bs = jnp.repeat(b_s0.T.astype(F32), GROUP_DIM, axis=1)

    y0p, k0p, v0p = _l0_prompt(x_prompt, lamv, g0, win0, bias_p, subg, lng, lnb, wm.astype(BF16), bs, wout0)
    y_prompt, tail_p = _l1_prompt(y0p, g1, win1, wdw, bdw, clng, clnb, wout1, fg)

    pad_rows = 128
    kpos = np.arange(past + dec_seq)[:, None]
    qpos = past + np.arange(dec_seq)[None, :]
    bsmp = jnp.transpose(_lookup(rbs, _np_bucket(kpos - qpos)), (0, 2, 1))
    bsmp = jnp.broadcast_to(bsmp[:, :, None, :], (past + dec_seq, HEADS, 2, dec_seq)).reshape(past + dec_seq, -1)
    bias_s = jnp.pad(bsmp, ((0, pad_rows - dec_seq), (0, 0)), constant_values=NEG)
    wbd = jnp.stack([jnp.kron(jnp.eye(nb, dtype=F32), wm[g, :dec_seq, :dec_seq]) for g in range(GROUPS)])
    bs_s = jnp.tile(bs[:dec_seq], (nb, 1))

    zs, ys, zvs, us = _sample(x_sample, cache_k0, cache_v0, state_conv1, lamv, g0, win0, bias_s, subg, lng, lnb,
                              wbd.astype(BF16), bs_s, wout0, g1, win1, wdw, bdw, clng, clnb, wout1, fg)

    keep = CONV_WIDTH - 1
    return (y_prompt,
            ys.reshape(nb, dec_seq, d),
            k0p,
            v0p,
            tail_p[:, CONV_OFF:],
            zs[:, P_K0:P_K0 + A_WIDTH].reshape(nb, dec_seq, HEADS, HEAD_DIM),
            zs[:, P_V0:P_V0 + A_WIDTH].reshape(nb, dec_seq, HEADS, HEAD_DIM),
            zvs.reshape(nb, dec_seq, B_WIDTH),
            jnp.concatenate([state_conv1, us.reshape(nb, dec_seq, d)], axis=1)[:, -keep:])
```

```python
import functools
import math

import numpy as np
import jax
import jax.numpy as jnp
from jax import lax
from jax.experimental import pallas as pl
from jax.experimental.pallas import tpu as pltpu

F32 = jnp.float32
BF16 = jnp.bfloat16

CHUNK = 64
HEADS = 4
HALF = 64
HEAD_DIM = 2 * HALF
A_WIDTH = HEADS * HEAD_DIM
LAMBDA_INIT = 0.2
N_BUCKETS = 32
MAX_DISTANCE = 128
GROUPS = 4
GROUP_DIM = 128
B_WIDTH = GROUPS * GROUP_DIM
B_CHUNK = 128
CONV_WIDTH = 31
RMS_EPS = 1e-6
LN_EPS = 1e-5

Q0, K0, V0, GA0, UB0, VB0, GB0 = (i * 512 for i in range(7))
L0_IN = 7 * 512
P_GMLP0 = 0
P_Q0, P_K0, P_V0, P_GA0 = (B_WIDTH * 3 + i * 512 for i in range(4))

TM = 256
TM1 = 512
CONV_PAD = 32
CONV_OFF = CONV_PAD - (CONV_WIDTH - 1)
CONV_RB = 64
EXP_ROWS = 64
VT_ROWS = HEAD_DIM + 16
LOG2E = 1.4426950408889634
Q_SCALE_LOG2 = HALF ** -0.5 * LOG2E
NEG = -1e30
VMEM_LIMIT = 56 * 1024 * 1024


def _np_bucket(rel):
    nb = N_BUCKETS // 2
    ret = np.where(rel > 0, nb, 0)
    n = np.abs(rel)
    max_exact = nb // 2
    nf = np.maximum(n, 1).astype(np.float32)
    large = max_exact + (np.log(nf / np.float32(max_exact)) / np.float32(math.log(MAX_DISTANCE / max_exact))
                         * np.float32(nb - max_exact)).astype(np.int32)
    large = np.minimum(large, nb - 1)
    return (ret + np.where(n < max_exact, n, large)).astype(np.int32)


FAR_BUCKET = N_BUCKETS // 2 - 1


def _lookup(table, idx):
    idx_c = jnp.asarray(idx)[..., None]
    out = jnp.zeros(idx.shape + (table.shape[1],), F32)
    for b in np.unique(idx):
        out = jnp.where(idx_c == int(b), table[int(b)], out)
    return out


def _rms(x, g):
    return x * lax.rsqrt(jnp.mean(x * x, axis=-1, keepdims=True) + RMS_EPS) * g


def _ln(x, g, b):
    mu = jnp.mean(x, axis=-1, keepdims=True)
    xc = x - mu
    var = jnp.mean(xc * xc, axis=-1, keepdims=True)
    return xc * lax.rsqrt(var + LN_EPS) * g + b


def _lam(lamv):
    s1 = jnp.sum(lamv[0:1] * lamv[1:2], axis=-1, keepdims=True)
    s2 = jnp.sum(lamv[2:3] * lamv[3:4], axis=-1, keepdims=True)
    return jnp.exp(s1) - jnp.exp(s2) + LAMBDA_INIT


def _gmlp_cols(z_ref, g):
    base = P_GMLP0 + 3 * g * GROUP_DIM
    return tuple(z_ref[:, base + j * GROUP_DIM:base + (j + 1) * GROUP_DIM] for j in range(3))


def _dot(a, b):
    return jnp.dot(a, b, preferred_element_type=F32)


def _dot_nt(a, b):
    return lax.dot_general(a, b, (((1,), (1,)), ((), ())), preferred_element_type=F32)


def _l0_prompt_kernel(lamv_ref, x_ref, g0_ref, win_ref, bias_ref, subg_ref, lng_ref, lnb_ref, wm_ref, bs_ref,
                      wout_ref, y_ref, k_ref, v_ref,
                      z_ref, kbf_ref, vt_ref, qm_ref, m_ref, acc_ref, s_ref, p_ref, mix_ref):
    i = pl.program_id(1)
    x = x_ref[0]
    z_ref[...] = _dot(_rms(x, g0_ref[...]).astype(BF16), win_ref[...])

    for g in range(GROUPS):
        sl = slice(g * GROUP_DIM, (g + 1) * GROUP_DIM)
        ub, vb, gb = _gmlp_cols(z_ref, g)
        zv = _ln(jax.nn.gelu(vb), lng_ref[:, sl], lnb_ref[:, sl]).astype(BF16)
        zu = jax.nn.gelu(ub)
        gate = jax.nn.silu(gb)
        for c in range(TM // B_CHUNK):
            rows = slice(c * B_CHUNK, (c + 1) * B_CHUNK)
            sg = _dot(wm_ref[g], zv[rows]) + bs_ref[:, sl]
            mix_ref[rows, A_WIDTH + g * GROUP_DIM:A_WIDTH + (g + 1) * GROUP_DIM] = (
                zu[rows] * sg * gate[rows]).astype(BF16)

    k = z_ref[:, P_K0:P_K0 + A_WIDTH]
    v = z_ref[:, P_V0:P_V0 + A_WIDTH]
    k_ref[0] = k.reshape(TM, HEADS, HEAD_DIM)
    v_ref[0] = v.reshape(TM, HEADS, HEAD_DIM)
    kbf_ref[i] = k.astype(BF16)
    for h in range(HEADS):
        vt_ref[i, h, 0:HEAD_DIM, :] = v[:, h * HEAD_DIM:(h + 1) * HEAD_DIM].T.astype(BF16)
        vt_ref[i, h, HEAD_DIM:VT_ROWS, :] = jnp.ones((VT_ROWS - HEAD_DIM, TM), BF16)

    lo = lax.broadcasted_iota(jnp.int32, (TM, HEAD_DIM), 1) < HALF
    for h in range(HEADS):
        q = z_ref[:, P_Q0 + h * HEAD_DIM:P_Q0 + (h + 1) * HEAD_DIM] * Q_SCALE_LOG2
        qm_ref[2 * h] = jnp.where(lo, q, 0.0).astype(BF16)
        qm_ref[2 * h + 1] = jnp.where(lo, 0.0, q).astype(BF16)

    units = range(2 * HEADS)

    def attend(tiles, first):
        m_new, alpha = {}, {}
        for u in units:
            h = u // 2
            mt = None
            for t, (j, bias_idx) in enumerate(tiles):
                s = _dot_nt(kbf_ref[j, :, h * HEAD_DIM:(h + 1) * HEAD_DIM], qm_ref[u])
                if bias_idx is not None:
                    s = s + bias_ref[h, bias_idx]
                s_ref[t, u] = s
                smax = jnp.max(s, axis=0, keepdims=True)
                mt = smax if mt is None else jnp.maximum(mt, smax)
            if first:
                m_new[u] = mt
            else:
                m_old = m_ref[u]
                m_new[u] = jnp.maximum(m_old, mt)
                alpha[u] = jnp.exp2(m_old - m_new[u])
            m_ref[u] = m_new[u]
        for u in units:
            for t in range(len(tiles)):
                for r in range(TM // EXP_ROWS):
                    rows = slice(r * EXP_ROWS, (r + 1) * EXP_ROWS)
                    p_ref[t, u, rows, :] = jnp.exp2(s_ref[t, u, rows, :] - m_new[u]).astype(BF16)
        for u in units:
            pv = _dot(vt_ref[tiles[0][0], u // 2], p_ref[0, u])
            for t in range(1, len(tiles)):
                pv = pv + _dot(vt_ref[tiles[t][0], u // 2], p_ref[t, u])
            acc_ref[u] = pv if first else alpha[u] * acc_ref[u] + pv

    @pl.when(i == 0)
    def _():
        attend([(i, 0)], True)

    @pl.when(i > 0)
    def _():
        attend([(i, 0), (i - 1, 1)], True)

    def far_pair(jj, carry):
        attend([(2 * jj, None), (2 * jj + 1, None)], False)
        return carry

    n_far = jnp.maximum(i - 1, 0)
    lax.fori_loop(0, n_far // 2, far_pair, 0)

    @pl.when(n_far % 2 == 1)
    def _():
        attend([(n_far - 1, None)], False)

    y_ref[0] = x + _dot(mix_ref[:, A_WIDTH:], wout_ref[A_WIDTH:, :])

    lam = _lam(lamv_ref[...])
    for h in range(HEADS):
        on1 = acc_ref[2 * h, 0:HEAD_DIM, :] / acc_ref[2 * h, HEAD_DIM:HEAD_DIM + 1, :]
        on2 = acc_ref[2 * h + 1, 0:HEAD_DIM, :] / acc_ref[2 * h + 1, HEAD_DIM:HEAD_DIM + 1, :]
        o = (on1 - lam * on2).T
        o = _rms(o, subg_ref[...]) * (1.0 - LAMBDA_INIT)
        ga = z_ref[:, P_GA0 + h * HEAD_DIM:P_GA0 + (h + 1) * HEAD_DIM]
        mix_ref[:, h * HEAD_DIM:(h + 1) * HEAD_DIM] = (o * jax.nn.silu(ga)).astype(BF16)

    y_ref[0] = y_ref[0] + _dot(mix_ref[:, 0:A_WIDTH], wout_ref[0:A_WIDTH, :])


def _l0_prompt(x, lamv, g0, win, bias, subg, lng, lnb, wm, bs, wout):
    bsz, t, d = x.shape
    nt = t // TM
    const = lambda shape: pl.BlockSpec(shape, lambda b, i: (0,) * len(shape))
    row = lambda w: pl.BlockSpec((1, TM, w), lambda b, i: (b, i, 0))
    kv_row = pl.BlockSpec((1, TM, HEADS, HEAD_DIM), lambda b, i: (b, i, 0, 0))
    return pl.pallas_call(
        _l0_prompt_kernel,
        grid=(bsz, nt),
        in_specs=[const((4, HALF)), row(d), const((1, d)), const((d, L0_IN)),
                  const((HEADS, 2, TM, TM)), const((1, HEAD_DIM)), const((1, B_WIDTH)), const((1, B_WIDTH)),
                  const((GROUPS, B_CHUNK, B_CHUNK)), const((B_CHUNK, B_WIDTH)), const((A_WIDTH + B_WIDTH, d))],
        out_specs=[row(d), kv_row, kv_row],
        out_shape=[jax.ShapeDtypeStruct((bsz, t, d), F32),
                   jax.ShapeDtypeStruct((bsz, t, HEADS, HEAD_DIM), F32),
                   jax.ShapeDtypeStruct((bsz, t, HEADS, HEAD_DIM), F32)],
        scratch_shapes=[pltpu.VMEM((TM, L0_IN), F32),
                        pltpu.VMEM((nt, TM, A_WIDTH), BF16),
                        pltpu.VMEM((nt, HEADS, VT_ROWS, TM), BF16),
                        pltpu.VMEM((2 * HEADS, TM, HEAD_DIM), BF16),
                        pltpu.VMEM((2 * HEADS, 1, TM), F32),
                        pltpu.VMEM((2 * HEADS, VT_ROWS, TM), F32),
                        pltpu.VMEM((2, 2 * HEADS, TM, TM), F32),
                        pltpu.VMEM((2, 2 * HEADS, TM, TM), BF16),
                        pltpu.VMEM((TM, A_WIDTH + B_WIDTH), BF16)],
        compiler_params=pltpu.CompilerParams(dimension_semantics=("arbitrary", "arbitrary"),
                                             vmem_limit_bytes=VMEM_LIMIT),
        name="l0_prompt",
    )(lamv, x, g0, win, bias, subg, lng, lnb, wm, bs, wout)


def _conv_rows(buf_ref, c, base, rows, wdw_ref, bdw_ref):
    lanes = slice(c * 128, (c + 1) * 128)
    acc = jnp.broadcast_to(bdw_ref[:, lanes], (rows, 128))
    for tap in range(CONV_WIDTH):
        lo = base + tap + CONV_OFF
        acc = acc + wdw_ref[tap:tap + 1, lanes] * buf_ref[c, lo:lo + rows, :]
    return acc


def _l1_prompt_kernel(x_ref, g1_ref, win_ref, wdw_ref, bdw_ref, clng_ref, clnb_ref, wout_ref, fg_ref,
                      y_ref, tail_ref, z_ref, buf_ref, conv_ref):
    i = pl.program_id(1)
    d = x_ref.shape[-1]
    ngroups = d // 128

    @pl.when(i == 0)
    def _():
        buf_ref[:, 0:CONV_PAD, :] = jnp.zeros((ngroups, CONV_PAD, 128), F32)

    x = x_ref[0]
    xn = _rms(x, g1_ref[...]).astype(BF16)
    z_ref[:, 0:2 * d] = _dot(xn, win_ref[:, 0:2 * d])
    for c in range(ngroups):
        a0 = (c // 2) * 4 * 128 + (c % 2) * 128
        buf_ref[c, CONV_PAD:CONV_PAD + TM1, :] = (z_ref[:, a0:a0 + 128]
                                                  * jax.nn.sigmoid(z_ref[:, a0 + 256:a0 + 384]))
    z_ref[:, 2 * d:3 * d] = _dot(xn, win_ref[:, 2 * d:3 * d])
    for c in range(ngroups):
        for r in range(TM1 // CONV_RB):
            conv_ref[r * CONV_RB:(r + 1) * CONV_RB, c * 128:(c + 1) * 128] = _conv_rows(
                buf_ref, c, r * CONV_RB, CONV_RB, wdw_ref, bdw_ref)

    cn = jax.nn.silu(_ln(conv_ref[...], clng_ref[...], clnb_ref[...])) * jax.nn.silu(z_ref[:, 2 * d:3 * d])
    y1 = x + _dot(cn.astype(BF16), wout_ref[...])
    y_ref[0] = _rms(y1, fg_ref[...])

    for c in range(ngroups):
        tail = buf_ref[c, TM1:TM1 + CONV_PAD, :]
        tail_ref[0, :, c * 128:(c + 1) * 128] = tail
        buf_ref[c, 0:CONV_PAD, :] = tail


def _l1_prompt(x, g1, win, wdw, bdw, clng, clnb, wout, fg):
    bsz, t, d = x.shape
    nt = t // TM1
    const = lambda shape: pl.BlockSpec(shape, lambda b, i: (0,) * len(shape))
    row = pl.BlockSpec((1, TM1, d), lambda b, i: (b, i, 0))
    return pl.pallas_call(
        _l1_prompt_kernel,
        grid=(bsz, nt),
        in_specs=[row, const((1, d)), const((d, 3 * d)), const((CONV_WIDTH, d)), const((1, d)),
                  const((1, d)), const((1, d)), const((d, d)), const((1, d))],
        out_specs=[row, pl.BlockSpec((1, CONV_PAD, d), lambda b, i: (b, 0, 0))],
        out_shape=[jax.ShapeDtypeStruct((bsz, t, d), F32),
                   jax.ShapeDtypeStruct((bsz, CONV_PAD, d), F32)],
        scratch_shapes=[pltpu.VMEM((TM1, 3 * d), F32),
                        pltpu.VMEM((d // 128, CONV_PAD + TM1, 128), F32),
                        pltpu.VMEM((TM1, d), F32)],
        compiler_params=pltpu.CompilerParams(dimension_semantics=("arbitrary", "arbitrary"),
                                             vmem_limit_bytes=VMEM_LIMIT),
        name="l1_prompt",
    )(x, g1, win, wdw, bdw, clng, clnb, wout, fg)


def _s_proj_kernel(x_ref, g0_ref, win_ref, z_ref):
    z_ref[...] = _dot(_rms(x_ref[...], g0_ref[...]).astype(BF16), win_ref[...])


def _s_attn_kernel(lamv_ref, z_ref, ck_ref, cv_ref, bias_ref, o_ref, *, dec_seq, pad_rows):
    ncol = HEADS * 2 * dec_seq
    q = z_ref[:, P_Q0:P_Q0 + A_WIDTH] * (HALF ** -0.5)
    q_rep = jnp.concatenate([q] * (2 * HEADS), axis=0)
    row = lax.broadcasted_iota(jnp.int32, (ncol, A_WIDTH), 0)
    col = lax.broadcasted_iota(jnp.int32, (ncol, A_WIDTH), 1)
    q_bd = jnp.where(col // HALF == row // dec_seq, q_rep, 0.0).astype(BF16)

    zeros = jnp.zeros((pad_rows - dec_seq, A_WIDTH), F32)
    ck = ck_ref[0].reshape(ck_ref.shape[1], A_WIDTH)
    cv = cv_ref[0].reshape(cv_ref.shape[1], A_WIDTH)
    k_all = jnp.concatenate([ck, z_ref[:, P_K0:P_K0 + A_WIDTH], zeros], axis=0).astype(BF16)
    v_all = jnp.concatenate([cv, z_ref[:, P_V0:P_V0 + A_WIDTH], zeros], axis=0).astype(BF16)

    s = _dot_nt(k_all, q_bd) + bias_ref[...]
    p = jnp.exp(s - jnp.max(s, axis=0, keepdims=True))
    p = p / jnp.sum(p, axis=0, keepdims=True)
    w = p - _lam(lamv_ref[...]) * pltpu.roll(p, ncol - dec_seq, axis=1)
    o_all = _dot(w.T.astype(BF16), v_all)
    for h in range(HEADS):
        o_ref[:, h * HEAD_DIM:(h + 1) * HEAD_DIM] = o_all[2 * h * dec_seq:(2 * h + 1) * dec_seq,
                                                          h * HEAD_DIM:(h + 1) * HEAD_DIM]


def _s_rest_kernel(x_ref, z_ref, o_ref, subg_ref, lng_ref, lnb_ref, wbd_ref, bs_ref, wout0_ref,
                   g1_ref, win1_ref, st_ref, wdw_ref, bdw_ref, clng_ref, clnb_ref, wout1_ref, fg_ref,
                   y_ref, zv_ref, u_ref, mix_ref, buf_ref, conv_ref, *, dec_seq):
    d = x_ref.shape[-1]
    nb = st_ref.shape[0]
    for h in range(HEADS):
        sl = slice(h * HEAD_DIM, (h + 1) * HEAD_DIM)
        o = _rms(o_ref[:, sl], subg_ref[...]) * (1.0 - LAMBDA_INIT)
        mix_ref[:, sl] = (o * jax.nn.silu(z_ref[:, P_GA0 + h * HEAD_DIM:P_GA0 + (h + 1) * HEAD_DIM])).astype(BF16)
    for g in range(GROUPS):
        sl = slice(g * GROUP_DIM, (g + 1) * GROUP_DIM)
        ub, vb, gb = _gmlp_cols(z_ref, g)
        zv = _ln(jax.nn.gelu(vb), lng_ref[:, sl], lnb_ref[:, sl])
        zv_ref[:, sl] = zv
        sg = _dot(wbd_ref[g], zv.astype(BF16)) + bs_ref[:, sl]
        mix_ref[:, A_WIDTH + g * GROUP_DIM:A_WIDTH + (g + 1) * GROUP_DIM] = (
            jax.nn.gelu(ub) * sg * jax.nn.silu(gb)).astype(BF16)
    y0 = x_ref[...] + _dot(mix_ref[...], wout0_ref[...])

    z1 = _dot(_rms(y0, g1_ref[...]).astype(BF16), win1_ref[...])
    u = jnp.concatenate([z1[:, a0:a0 + 256] * jax.nn.sigmoid(z1[:, a0 + 256:a0 + 512])
                         for a0 in range(0, 2 * d, 512)], axis=1)
    u_ref[...] = u
    ngroups = d // 128
    for b in range(nb):
        for c in range(ngroups):
            lanes = slice(c * 128, (c + 1) * 128)
            buf_ref[b * ngroups + c, 0:CONV_PAD, :] = st_ref[b, :, lanes]
            buf_ref[b * ngroups + c, CONV_PAD:CONV_PAD + dec_seq, :] = u[b * dec_seq:(b + 1) * dec_seq, lanes]
    for b in range(nb):
        for c in range(ngroups):
            conv_ref[b * dec_seq:(b + 1) * dec_seq, c * 128:(c + 1) * 128] = _conv_rows(
                buf_ref.at[pl.ds(b * ngroups, ngroups)], c, 0, dec_seq, wdw_ref, bdw_ref)
    c = jax.nn.silu(_ln(conv_ref[...], clng_ref[...], clnb_ref[...])) * jax.nn.silu(z1[:, 2 * d:3 * d])
    y1 = y0 + _dot(c.astype(BF16), wout1_ref[...])
    y_ref[...] = _rms(y1, fg_ref[...])


def _sample(xs, cache_k, cache_v, state, lamv, g0, win0, bias_s, subg, lng, lnb, wbd, bs_s, wout0,
            g1, win1, wdw, bdw, clng, clnb, wout1, fg):
    nb, dec_seq, d = xs.shape
    rows = nb * dec_seq
    past = cache_k.shape[1]
    pad_rows = bias_s.shape[0] - past
    x2 = xs.reshape(rows, d)
    z = pl.pallas_call(_s_proj_kernel, out_shape=jax.ShapeDtypeStruct((rows, L0_IN), F32),
                       compiler_params=pltpu.CompilerParams(vmem_limit_bytes=VMEM_LIMIT),
                       name="s_proj")(x2, g0, win0)

    ncol = HEADS * 2 * dec_seq
    o = pl.pallas_call(
        functools.partial(_s_attn_kernel, dec_seq=dec_seq, pad_rows=pad_rows),
        grid=(nb,),
        in_specs=[pl.BlockSpec((4, HALF), lambda b: (0, 0)),
                  pl.BlockSpec((dec_seq, L0_IN), lambda b: (b, 0)),
                  pl.BlockSpec((1, past, HEADS, HEAD_DIM), lambda b: (b, 0, 0, 0)),
                  pl.BlockSpec((1, past, HEADS, HEAD_DIM), lambda b: (b, 0, 0, 0)),
                  pl.BlockSpec((past + pad_rows, ncol), lambda b: (0, 0))],
        out_specs=pl.BlockSpec((dec_seq, A_WIDTH), lambda b: (b, 0)),
        out_shape=jax.ShapeDtypeStruct((rows, A_WIDTH), F32),
        compiler_params=pltpu.CompilerParams(dimension_semantics=("arbitrary",), vmem_limit_bytes=VMEM_LIMIT),
        name="s_attn",
    )(lamv, z, cache_k, cache_v, bias_s)

    state_pad = jnp.pad(state, ((0, 0), (CONV_OFF, 0), (0, 0)))
    y, zv, u = pl.pallas_call(
        functools.partial(_s_rest_kernel, dec_seq=dec_seq),
        out_shape=[jax.ShapeDtypeStruct((rows, d), F32),
                   jax.ShapeDtypeStruct((rows, B_WIDTH), F32),
                   jax.ShapeDtypeStruct((rows, d), F32)],
        scratch_shapes=[pltpu.VMEM((rows, A_WIDTH + B_WIDTH), BF16),
                        pltpu.VMEM((nb * (d // 128), CONV_PAD + dec_seq, 128), F32),
                        pltpu.VMEM((rows, d), F32)],
        compiler_params=pltpu.CompilerParams(vmem_limit_bytes=VMEM_LIMIT),
        name="s_rest",
    )(x2, z, o, subg, lng, lnb, wbd, bs_s, wout0, g1, win1, state_pad, wdw, bdw, clng, clnb, wout1, fg)
    return z, y, zv, u


def kernel(x_prompt, x_sample, cache_k0, cache_v0, state_conv1, rel_bias, norm_g0, w_in0, lambda_q1, lambda_k1,
           lambda_q2, lambda_k2, subln_g0, gv_ln_g0, gv_ln_b0, w_s0, b_s0, w_out0, norm_g1, w_in1, w_dw1, b_dw1,
           conv_ln_g1, conv_ln_b1, w_out1, final_g):
    bsz, t, d = x_prompt.shape
    nb, dec_seq, _ = x_sample.shape
    past = cache_k0.shape[1]
    assert t % TM == 0 and t % TM1 == 0 and TM % B_CHUNK == 0 and TM % CHUNK == 0 and TM >= MAX_DISTANCE
    assert dec_seq <= B_CHUNK and dec_seq % 8 == 0 and HEADS * 2 * dec_seq == 128
    assert past % CHUNK == 0 and (past + dec_seq - 1) // CHUNK == past // CHUNK

    row = lambda a: a.reshape(1, -1).astype(F32)
    lamv = jnp.stack([lambda_q1, lambda_k1, lambda_q2, lambda_k2]).astype(F32)
    wout0, wout1 = w_out0.astype(BF16), w_out1.astype(BF16)
    cols0 = [w_in0[:, o + g * GROUP_DIM:o + (g + 1) * GROUP_DIM] for g in range(GROUPS) for o in (UB0, VB0, GB0)]
    win0 = jnp.concatenate(cols0 + [w_in0[:, :UB0]], axis=1).astype(BF16)
    cols1 = [w_in1[:, o + p * 256:o + (p + 1) * 256] for p in range(d // 256) for o in (0, d)]
    win1 = jnp.concatenate(cols1 + [w_in1[:, 2 * d:]], axis=1).astype(BF16)
    g0, g1, fg, subg = row(norm_g0), row(norm_g1), row(final_g), row(subln_g0)
    lng, lnb = row(gv_ln_g0), row(gv_ln_b0)
    bdw, clng, clnb = row(b_dw1), row(conv_ln_g1), row(conv_ln_b1)
    wdw = w_dw1.astype(F32)

    rbs = (rel_bias - rel_bias[FAR_BUCKET:FAR_BUCKET + 1]).astype(F32)
    assert np.all(_np_bucket(-np.arange(TM + 1, 2 * max(t, past + dec_seq))) == FAR_BUCKET)
    kk = np.arange(TM)[:, None]
    qq = np.arange(TM)[None, :]
    own = jnp.where((kk // CHUNK <= qq // CHUNK)[..., None], _lookup(rbs, _np_bucket(kk - qq)), NEG)
    prev = _lookup(rbs, _np_bucket(kk - qq - TM))
    bias_p = jnp.transpose(jnp.stack([own, prev]), (3, 0, 1, 2)) * LOG2E

    tril = np.tril(np.ones((B_CHUNK, B_CHUNK), dtype=bool))
    wm = jnp.where(tril[None], w_s0, 0)
    bs = jnp.repeat(b_s0.T.astype(F32), GROUP_DIM, axis=1)

    y0p, k0p, v0p = _l0_prompt(x_prompt, lamv, g0, win0, bias_p, subg, lng, lnb, wm.astype(BF16), bs, wout0)
    y_prompt, tail_p = _l1_prompt(y0p, g1, win1, wdw, bdw, clng, clnb, wout1, fg)

    pad_rows = 128
    kpos = np.arange(past + dec_seq)[:, None]
    qpos = past + np.arange(dec_seq)[None, :]
    bsmp = jnp.transpose(_lookup(rbs, _np_bucket(kpos - qpos)), (0, 2, 1))
    bsmp = jnp.broadcast_to(bsmp[:, :, None, :], (past + dec_seq, HEADS, 2, dec_seq)).reshape(past + dec_seq, -1)
    bias_s = jnp.pad(bsmp, ((0, pad_rows - dec_seq), (0, 0)), constant_values=NEG)
    wbd = jnp.stack([jnp.kron(jnp.eye(nb, dtype=F32), wm[g, :dec_seq, :dec_seq]) for g in range(GROUPS)])
    bs_s = jnp.tile(bs[:dec_seq], (nb, 1))

    zs, ys, zvs, us = _sample(x_sample, cache_k0, cache_v0, state_conv1, lamv, g0, win0, bias_s, subg, lng, lnb,
                              wbd.astype(BF16), bs_s, wout0, g1, win1, wdw, bdw, clng, clnb, wout1, fg)

    keep = CONV_WIDTH - 1
    return (y_prompt,
            ys.reshape(nb, dec_seq, d),
            k0p,
            v0p,
            tail_p[:, CONV_OFF:],
            zs[:, P_K0:P_K0 + A_WIDTH].reshape(nb, dec_seq, HEADS, HEAD_DIM),
            zs[:, P_V0:P_V0 + A_WIDTH].reshape(nb, dec_seq, HEADS, HEAD_DIM),
            zvs.reshape(nb, dec_seq, B_WIDTH),
            jnp.concatenate([state_conv1, us.reshape(nb, dec_seq, d)], axis=1)[:, -keep:])
```

```python
import functools
import math

import numpy as np
import jax
import jax.numpy as jnp
from jax import lax
from jax.experimental import pallas as pl
from jax.experimental.pallas import tpu as pltpu

F32 = jnp.float32
BF16 = jnp.bfloat16

CHUNK = 64
HEADS = 4
HALF = 64
HEAD_DIM = 2 * HALF
A_WIDTH = HEADS * HEAD_DIM
LAMBDA_INIT = 0.2
N_BUCKETS = 32
MAX_DISTANCE = 128
GROUPS = 4
GROUP_DIM = 128
B_WIDTH = GROUPS * GROUP_DIM
B_CHUNK = 128
CONV_WIDTH = 31
RMS_EPS = 1e-6
LN_EPS = 1e-5

Q0, K0, V0, GA0, UB0, VB0, GB0 = (i * 512 for i in range(7))
L0_IN = 7 * 512
P_GMLP0 = 0
P_Q0, P_K0, P_V0, P_GA0 = (B_WIDTH * 3 + i * 512 for i in range(4))

TM = 256
CONV_PAD = 32
CONV_OFF = CONV_PAD - (CONV_WIDTH - 1)
CONV_RB = 64
EXP_ROWS = 64
LN_ROWS = 64
PROJ_COLS = 256
VT_ROWS = HEAD_DIM + 16
LOG2E = 1.4426950408889634
Q_SCALE_LOG2 = HALF ** -0.5 * LOG2E
NEG = -1e30
VMEM_LIMIT = 56 * 1024 * 1024


def _np_bucket(rel):
    nb = N_BUCKETS // 2
    ret = np.where(rel > 0, nb, 0)
    n = np.abs(rel)
    max_exact = nb // 2
    nf = np.maximum(n, 1).astype(np.float32)
    large = max_exact + (np.log(nf / np.float32(max_exact)) / np.float32(math.log(MAX_DISTANCE / max_exact))
                         * np.float32(nb - max_exact)).astype(np.int32)
    large = np.minimum(large, nb - 1)
    return (ret + np.where(n < max_exact, n, large)).astype(np.int32)


FAR_BUCKET = N_BUCKETS // 2 - 1


def _lookup(table, idx):
    idx_c = jnp.asarray(idx)[..., None]
    out = jnp.zeros(idx.shape + (table.shape[1],), F32)
    for b in np.unique(idx):
        out = jnp.where(idx_c == int(b), table[int(b)], out)
    return out


def _rms(x, g):
    return x * lax.rsqrt(jnp.mean(x * x, axis=-1, keepdims=True) + RMS_EPS) * g


def _ln(x, g, b):
    mu = jnp.mean(x, axis=-1, keepdims=True)
    xc = x - mu
    var = jnp.mean(xc * xc, axis=-1, keepdims=True)
    return xc * lax.rsqrt(var + LN_EPS) * g + b


def _lam(lamv):
    s1 = jnp.sum(lamv[0:1] * lamv[1:2], axis=-1, keepdims=True)
    s2 = jnp.sum(lamv[2:3] * lamv[3:4], axis=-1, keepdims=True)
    return jnp.exp(s1) - jnp.exp(s2) + LAMBDA_INIT


def _gmlp_cols(z_ref, g):
    base = P_GMLP0 + 3 * g * GROUP_DIM
    return tuple(z_ref[:, base + j * GROUP_DIM:base + (j + 1) * GROUP_DIM] for j in range(3))


def _dot(a, b):
    return jnp.dot(a, b, preferred_element_type=F32)


def _dot_nt(a, b):
    return lax.dot_general(a, b, (((1,), (1,)), ((), ())), preferred_element_type=F32)


def _conv_rows(buf_ref, c, base, rows, wdw_ref, bdw_ref):
    lanes = slice(c * 128, (c + 1) * 128)
    acc = jnp.broadcast_to(bdw_ref[:, lanes], (rows, 128))
    for tap in range(CONV_WIDTH):
        lo = base + tap + CONV_OFF
        acc = acc + wdw_ref[tap:tap + 1, lanes] * buf_ref[c, lo:lo + rows, :]
    return acc


def _prompt_kernel(lamv_ref, x_ref, g0_ref, win_ref, bias_ref, subg_ref, lng_ref, lnb_ref, wm_ref, bs_ref,
                   wout_ref, g1_ref, win1_ref, wdw_ref, bdw_ref, clng_ref, clnb_ref, wout1_ref, fg_ref,
                   y_ref, k_ref, v_ref, tail_ref,
                   z_ref, kbf_ref, vt_ref, qm_ref, m_ref, acc_ref, s_ref, p_ref, mix_ref,
                   buf_ref, conv_ref, gate_ref, y0_ref, y0p_ref, cn_ref, *, nt, ntiles):
    step = pl.program_id(0)
    i = jnp.minimum(step, ntiles - 1) % nt
    d = x_ref.shape[-1]
    ngroups = d // 128

    @pl.when(step == 0)
    def _():
        buf_ref[...] = jnp.zeros(buf_ref.shape, F32)
        gate_ref[...] = jnp.zeros(gate_ref.shape, F32)
        y0p_ref[...] = jnp.zeros(y0p_ref.shape, F32)

    x = x_ref[0]
    xn = _rms(x, g0_ref[...]).astype(BF16)
    blocks = [(c, r) for c in range(ngroups) for r in range(TM // CONV_RB)]
    nchunks = L0_IN // PROJ_COLS
    for j in range(nchunks):
        cols = slice(j * PROJ_COLS, (j + 1) * PROJ_COLS)
        z_ref[:, cols] = _dot(xn, win_ref[:, cols])
        for c, r in blocks[j * len(blocks) // nchunks:(j + 1) * len(blocks) // nchunks]:
            conv_ref[r * CONV_RB:(r + 1) * CONV_RB, c * 128:(c + 1) * 128] = _conv_rows(
                buf_ref, c, r * CONV_RB, CONV_RB, wdw_ref, bdw_ref)

    for g in range(GROUPS):
        sl = slice(g * GROUP_DIM, (g + 1) * GROUP_DIM)
        ub, vb, gb = _gmlp_cols(z_ref, g)
        zv = _ln(jax.nn.gelu(vb), lng_ref[:, sl], lnb_ref[:, sl]).astype(BF16)
        zu = jax.nn.gelu(ub)
        gate = jax.nn.silu(gb)
        for c in range(TM // B_CHUNK):
            rows = slice(c * B_CHUNK, (c + 1) * B_CHUNK)
            sg = _dot(wm_ref[g], zv[rows]) + bs_ref[:, sl]
            mix_ref[rows, A_WIDTH + g * GROUP_DIM:A_WIDTH + (g + 1) * GROUP_DIM] = (
                zu[rows] * sg * gate[rows]).astype(BF16)

    k = z_ref[:, P_K0:P_K0 + A_WIDTH]
    v = z_ref[:, P_V0:P_V0 + A_WIDTH]
    k_ref[0] = k.reshape(TM, HEADS, HEAD_DIM)
    v_ref[0] = v.reshape(TM, HEADS, HEAD_DIM)
    kbf_ref[i] = k.astype(BF16)
    for h in range(HEADS):
        vt_ref[i, h, 0:HEAD_DIM, :] = v[:, h * HEAD_DIM:(h + 1) * HEAD_DIM].T.astype(BF16)
        vt_ref[i, h, HEAD_DIM:VT_ROWS, :] = jnp.ones((VT_ROWS - HEAD_DIM, TM), BF16)

    lo = lax.broadcasted_iota(jnp.int32, (TM, HEAD_DIM), 1) < HALF
    for h in range(HEADS):
        q = z_ref[:, P_Q0 + h * HEAD_DIM:P_Q0 + (h + 1) * HEAD_DIM] * Q_SCALE_LOG2
        qm_ref[2 * h] = jnp.where(lo, q, 0.0).astype(BF16)
        qm_ref[2 * h + 1] = jnp.where(lo, 0.0, q).astype(BF16)

    units = range(2 * HEADS)

    def attend(tiles, first):
        m_new, alpha = {}, {}
        for u in units:
            h = u // 2
            mt = None
            for t, (j, bias_idx) in enumerate(tiles):
                s = _dot_nt(kbf_ref[j, :, h * HEAD_DIM:(h + 1) * HEAD_DIM], qm_ref[u])
                if bias_idx is not None:
                    s = s + bias_ref[h, bias_idx]
                s_ref[t, u] = s
                smax = jnp.max(s, axis=0, keepdims=True)
                mt = smax if mt is None else jnp.maximum(mt, smax)
            if first:
                m_new[u] = mt
            else:
                m_old = m_ref[u]
                m_new[u] = jnp.maximum(m_old, mt)
                alpha[u] = jnp.exp2(m_old - m_new[u])
            m_ref[u] = m_new[u]
        for u in units:
            for t in range(len(tiles)):
                for r in range(TM // EXP_ROWS):
                    rows = slice(r * EXP_ROWS, (r + 1) * EXP_ROWS)
                    p_ref[t, u, rows, :] = jnp.exp2(s_ref[t, u, rows, :] - m_new[u]).astype(BF16)
        for u in units:
            pv = _dot(vt_ref[tiles[0][0], u // 2], p_ref[0, u])
            for t in range(1, len(tiles)):
                pv = pv + _dot(vt_ref[tiles[t][0], u // 2], p_ref[t, u])
            acc_ref[u] = pv if first else alpha[u] * acc_ref[u] + pv

    @pl.when(i == 0)
    def _():
        attend([(i, 0)], True)

    @pl.when(i > 0)
    def _():
        attend([(i, 0), (i - 1, 1)], True)

    def far_pair(jj, carry):
        attend([(2 * jj, None), (2 * jj + 1, None)], False)
        return carry

    n_far = jnp.maximum(i - 1, 0)
    lax.fori_loop(0, n_far // 2, far_pair, 0)

    @pl.when(n_far % 2 == 1)
    def _():
        attend([(n_far - 1, None)], False)

    y0_ref[...] = x + _dot(mix_ref[:, A_WIDTH:], wout_ref[A_WIDTH:, :])

    lam = _lam(lamv_ref[...])
    for h in range(HEADS):
        on1 = acc_ref[2 * h, 0:HEAD_DIM, :] / acc_ref[2 * h, HEAD_DIM:HEAD_DIM + 1, :]
        on2 = acc_ref[2 * h + 1, 0:HEAD_DIM, :] / acc_ref[2 * h + 1, HEAD_DIM:HEAD_DIM + 1, :]
        o = (on1 - lam * on2).T
        o = _rms(o, subg_ref[...]) * (1.0 - LAMBDA_INIT)
        ga = z_ref[:, P_GA0 + h * HEAD_DIM:P_GA0 + (h + 1) * HEAD_DIM]
        mix_ref[:, h * HEAD_DIM:(h + 1) * HEAD_DIM] = (o * jax.nn.silu(ga)).astype(BF16)

    y0 = y0_ref[...] + _dot(mix_ref[:, 0:A_WIDTH], wout_ref[0:A_WIDTH, :])
    y0_ref[...] = y0

    xn1 = _rms(y0, g1_ref[...]).astype(BF16)
    starts_stream = i == 0
    nrow = TM // LN_ROWS
    n_glu = 2 * d // PROJ_COLS
    for j in range(3 * d // PROJ_COLS):
        cols = slice(j * PROJ_COLS, (j + 1) * PROJ_COLS)
        z_ref[:, cols] = _dot(xn1, win1_ref[:, cols])
        if j < nrow:
            rows = slice(j * LN_ROWS, (j + 1) * LN_ROWS)
            cn_ref[rows, :] = (jax.nn.silu(_ln(conv_ref[rows, :], clng_ref[...], clnb_ref[...]))
                               * gate_ref[rows, :]).astype(BF16)
        if j == nrow:
            y_ref[0] = _rms(y0p_ref[...] + _dot(cn_ref[...], wout1_ref[...]), fg_ref[...])
        if j < n_glu and j % 2 == 1:
            for c in (j - 1, j):
                a0 = (j - 1) * PROJ_COLS + (c % 2) * 128
                tail = buf_ref[c, TM:TM + CONV_PAD, :]
                tail_ref[0, :, c * 128:(c + 1) * 128] = tail
                buf_ref[c, 0:CONV_PAD, :] = jnp.where(starts_stream, 0.0, tail)
                buf_ref[c, CONV_PAD:CONV_PAD + TM, :] = (z_ref[:, a0:a0 + 128]
                                                         * jax.nn.sigmoid(z_ref[:, a0 + 256:a0 + 384]))
        if j >= n_glu:
            gcols = slice((j - n_glu) * PROJ_COLS, (j - n_glu + 1) * PROJ_COLS)
            gate_ref[:, gcols] = jax.nn.silu(z_ref[:, cols])
    y0p_ref[...] = y0


def _prompt(x, lamv, g0, win, bias, subg, lng, lnb, wm, bs, wout, g1, win1, wdw, bdw, clng, clnb, wout1, fg):
    bsz, t, d = x.shape
    nt = t // TM
    ntiles = bsz * nt
    const = lambda shape: pl.BlockSpec(shape, lambda s: (0,) * len(shape))
    cur = lambda s: jnp.minimum(s, ntiles - 1)
    done = lambda s: jnp.maximum(s - 1, 0)
    kv_row = pl.BlockSpec((1, TM, HEADS, HEAD_DIM), lambda s: (cur(s) // nt, cur(s) % nt, 0, 0))
    return pl.pallas_call(
        functools.partial(_prompt_kernel, nt=nt, ntiles=ntiles),
        grid=(ntiles + 1,),
        in_specs=[const((4, HALF)), pl.BlockSpec((1, TM, d), lambda s: (cur(s) // nt, cur(s) % nt, 0)),
                  const((1, d)), const((d, L0_IN)),
                  const((HEADS, 2, TM, TM)), const((1, HEAD_DIM)), const((1, B_WIDTH)), const((1, B_WIDTH)),
                  const((GROUPS, B_CHUNK, B_CHUNK)), const((B_CHUNK, B_WIDTH)), const((A_WIDTH + B_WIDTH, d)),
                  const((1, d)), const((d, 3 * d)), const((CONV_WIDTH, d)), const((1, d)),
                  const((1, d)), const((1, d)), const((d, d)), const((1, d))],
        out_specs=[pl.BlockSpec((1, TM, d), lambda s: (done(s) // nt, done(s) % nt, 0)), kv_row, kv_row,
                   pl.BlockSpec((1, CONV_PAD, d), lambda s: (done(s) // nt, 0, 0))],
        out_shape=[jax.ShapeDtypeStruct((bsz, t, d), F32),
                   jax.ShapeDtypeStruct((bsz, t, HEADS, HEAD_DIM), F32),
                   jax.ShapeDtypeStruct((bsz, t, HEADS, HEAD_DIM), F32),
                   jax.ShapeDtypeStruct((bsz, CONV_PAD, d), F32)],
        scratch_shapes=[pltpu.VMEM((TM, L0_IN), F32),
                        pltpu.VMEM((nt, TM, A_WIDTH), BF16),
                        pltpu.VMEM((nt, HEADS, VT_ROWS, TM), BF16),
                        pltpu.VMEM((2 * HEADS, TM, HEAD_DIM), BF16),
                        pltpu.VMEM((2 * HEADS, 1, TM), F32),
                        pltpu.VMEM((2 * HEADS, VT_ROWS, TM), F32),
                        pltpu.VMEM((2, 2 * HEADS, TM, TM), F32),
                        pltpu.VMEM((2, 2 * HEADS, TM, TM), BF16),
                        pltpu.VMEM((TM, A_WIDTH + B_WIDTH), BF16),
                        pltpu.VMEM((d // 128, CONV_PAD + TM, 128), F32),
                        pltpu.VMEM((TM, d), F32),
                        pltpu.VMEM((TM, d), F32),
                        pltpu.VMEM((TM, d), F32),
                        pltpu.VMEM((TM, d), F32),
                        pltpu.VMEM((TM, d), BF16)],
        compiler_params=pltpu.CompilerParams(dimension_semantics=("arbitrary",), vmem_limit_bytes=VMEM_LIMIT),
        name="prompt",
    )(lamv, x, g0, win, bias, subg, lng, lnb, wm, bs, wout, g1, win1, wdw, bdw, clng, clnb, wout1, fg)


def _s_proj_kernel(x_ref, g0_ref, win_ref, z_ref):
    z_ref[...] = _dot(_rms(x_ref[...], g0_ref[...]).astype(BF16), win_ref[...])


def _s_attn_kernel(lamv_ref, z_ref, ck_ref, cv_ref, bias_ref, o_ref, *, dec_seq, pad_rows):
    ncol = HEADS * 2 * dec_seq
    q = z_ref[:, P_Q0:P_Q0 + A_WIDTH] * (HALF ** -0.5)
    q_rep = jnp.concatenate([q] * (2 * HEADS), axis=0)
    row = lax.broadcasted_iota(jnp.int32, (ncol, A_WIDTH), 0)
    col = lax.broadcasted_iota(jnp.int32, (ncol, A_WIDTH), 1)
    q_bd = jnp.where(col // HALF == row // dec_seq, q_rep, 0.0).astype(BF16)

    zeros = jnp.zeros((pad_rows - dec_seq, A_WIDTH), F32)
    ck = ck_ref[0].reshape(ck_ref.shape[1], A_WIDTH)
    cv = cv_ref[0].reshape(cv_ref.shape[1], A_WIDTH)
    k_all = jnp.concatenate([ck, z_ref[:, P_K0:P_K0 + A_WIDTH], zeros], axis=0).astype(BF16)
    v_all = jnp.concatenate([cv, z_ref[:, P_V0:P_V0 + A_WIDTH], zeros], axis=0).astype(BF16)

    s = _dot_nt(k_all, q_bd) + bias_ref[...]
    p = jnp.exp(s - jnp.max(s, axis=0, keepdims=True))
    p = p / jnp.sum(p, axis=0, keepdims=True)
    w = p - _lam(lamv_ref[...]) * pltpu.roll(p, ncol - dec_seq, axis=1)
    o_all = _dot(w.T.astype(BF16), v_all)
    for h in range(HEADS):
        o_ref[:, h * HEAD_DIM:(h + 1) * HEAD_DIM] = o_all[2 * h * dec_seq:(2 * h + 1) * dec_seq,
                                                          h * HEAD_DIM:(h + 1) * HEAD_DIM]


def _s_rest_kernel(x_ref, z_ref, o_ref, subg_ref, lng_ref, lnb_ref, wbd_ref, bs_ref, wout0_ref,
                   g1_ref, win1_ref, st_ref, wdw_ref, bdw_ref, clng_ref, clnb_ref, wout1_ref, fg_ref,
                   y_ref, zv_ref, u_ref, mix_ref, buf_ref, conv_ref, *, dec_seq):
    d = x_ref.shape[-1]
    nb = st_ref.shape[0]
    for h in range(HEADS):
        sl = slice(h * HEAD_DIM, (h + 1) * HEAD_DIM)
        o = _rms(o_ref[:, sl], subg_ref[...]) * (1.0 - LAMBDA_INIT)
        mix_ref[:, sl] = (o * jax.nn.silu(z_ref[:, P_GA0 + h * HEAD_DIM:P_GA0 + (h + 1) * HEAD_DIM])).astype(BF16)
    for g in range(GROUPS):
        sl = slice(g * GROUP_DIM, (g + 1) * GROUP_DIM)
        ub, vb, gb = _gmlp_cols(z_ref, g)
        zv = _ln(jax.nn.gelu(vb), lng_ref[:, sl], lnb_ref[:, sl])
        zv_ref[:, sl] = zv
        sg = _dot(wbd_ref[g], zv.astype(BF16)) + bs_ref[:, sl]
        mix_ref[:, A_WIDTH + g * GROUP_DIM:A_WIDTH + (g + 1) * GROUP_DIM] = (
            jax.nn.gelu(ub) * sg * jax.nn.silu(gb)).astype(BF16)
    y0 = x_ref[...] + _dot(mix_ref[...], wout0_ref[...])

    z1 = _dot(_rms(y0, g1_ref[...]).astype(BF16), win1_ref[...])
    u = jnp.concatenate([z1[:, a0:a0 + 256] * jax.nn.sigmoid(z1[:, a0 + 256:a0 + 512])
                         for a0 in range(0, 2 * d, 512)], axis=1)
    u_ref[...] = u
    ngroups = d // 128
    for b in range(nb):
        for c in range(ngroups):
            lanes = slice(c * 128, (c + 1) * 128)
            buf_ref[b * ngroups + c, 0:CONV_PAD, :] = st_ref[b, :, lanes]
            buf_ref[b * ngroups + c, CONV_PAD:CONV_PAD + dec_seq, :] = u[b * dec_seq:(b + 1) * dec_seq, lanes]
    for b in range(nb):
        for c in range(ngroups):
            conv_ref[b * dec_seq:(b + 1) * dec_seq, c * 128:(c + 1) * 128] = _conv_rows(
                buf_ref.at[pl.ds(b * ngroups, ngroups)], c, 0, dec_seq, wdw_ref, bdw_ref)
    c = jax.nn.silu(_ln(conv_ref[...], clng_ref[...], clnb_ref[...])) * jax.nn.silu(z1[:, 2 * d:3 * d])
    y1 = y0 + _dot(c.astype(BF16), wout1_ref[...])
    y_ref[...] = _rms(y1, fg_ref[...])


def _sample(xs, cache_k, cache_v, state, lamv, g0, win0, bias_s, subg, lng, lnb, wbd, bs_s, wout0,
            g1, win1, wdw, bdw, clng, clnb, wout1, fg):
    nb, dec_seq, d = xs.shape
    rows = nb * dec_seq
    past = cache_k.shape[1]
    pad_rows = bias_s.shape[0] - past
    x2 = xs.reshape(rows, d)
    z = pl.pallas_call(_s_proj_kernel, out_shape=jax.ShapeDtypeStruct((rows, L0_IN), F32),
                       compiler_params=pltpu.CompilerParams(vmem_limit_bytes=VMEM_LIMIT),
                       name="s_proj")(x2, g0, win0)

    ncol = HEADS * 2 * dec_seq
    o = pl.pallas_call(
        functools.partial(_s_attn_kernel, dec_seq=dec_seq, pad_rows=pad_rows),
        grid=(nb,),
        in_specs=[pl.BlockSpec((4, HALF), lambda b: (0, 0)),
                  pl.BlockSpec((dec_seq, L0_IN), lambda b: (b, 0)),
                  pl.BlockSpec((1, past, HEADS, HEAD_DIM), lambda b: (b, 0, 0, 0)),
                  pl.BlockSpec((1, past, HEADS, HEAD_DIM), lambda b: (b, 0, 0, 0)),
                  pl.BlockSpec((past + pad_rows, ncol), lambda b: (0, 0))],
        out_specs=pl.BlockSpec((dec_seq, A_WIDTH), lambda b: (b, 0)),
        out_shape=jax.ShapeDtypeStruct((rows, A_WIDTH), F32),
        compiler_params=pltpu.CompilerParams(dimension_semantics=("arbitrary",), vmem_limit_bytes=VMEM_LIMIT),
        name="s_attn",
    )(lamv, z, cache_k, cache_v, bias_s)

    state_pad = jnp.pad(state, ((0, 0), (CONV_OFF, 0), (0, 0)))
    y, zv, u = pl.pallas_call(
        functools.partial(_s_rest_kernel, dec_seq=dec_seq),
        out_shape=[jax.ShapeDtypeStruct((rows, d), F32),
                   jax.ShapeDtypeStruct((rows, B_WIDTH), F32),
                   jax.ShapeDtypeStruct((rows, d), F32)],
        scratch_shapes=[pltpu.VMEM((rows, A_WIDTH + B_WIDTH), BF16),
                        pltpu.VMEM((nb * (d // 128), CONV_PAD + dec_seq, 128), F32),
                        pltpu.VMEM((rows, d), F32)],
        compiler_params=pltpu.CompilerParams(vmem_limit_bytes=VMEM_LIMIT),
        name="s_rest",
    )(x2, z, o, subg, lng, lnb, wbd, bs_s, wout0, g1, win1, state_pad, wdw, bdw, clng, clnb, wout1, fg)
    return z, y, zv, u


def kernel(x_prompt, x_sample, cache_k0, cache_v0, state_conv1, rel_bias, norm_g0, w_in0, lambda_q1, lambda_k1,
           lambda_q2, lambda_k2, subln_g0, gv_ln_g0, gv_ln_b0, w_s0, b_s0, w_out0, norm_g1, w_in1, w_dw1, b_dw1,
           conv_ln_g1, conv_ln_b1, w_out1, final_g):
    bsz, t, d = x_prompt.shape
    nb, dec_seq, _ = x_sample.shape
    past = cache_k0.shape[1]
    assert t % TM == 0 and TM % B_CHUNK == 0 and TM % CHUNK == 0 and TM >= MAX_DISTANCE
    assert dec_seq <= B_CHUNK and dec_seq % 8 == 0 and HEADS * 2 * dec_seq == 128
    assert past % CHUNK == 0 and (past + dec_seq - 1) // CHUNK == past // CHUNK

    row = lambda a: a.reshape(1, -1).astype(F32)
    lamv = jnp.stack([lambda_q1, lambda_k1, lambda_q2, lambda_k2]).astype(F32)
    wout0, wout1 = w_out0.astype(BF16), w_out1.astype(BF16)
    cols0 = [w_in0[:, o + g * GROUP_DIM:o + (g + 1) * GROUP_DIM] for g in range(GROUPS) for o in (UB0, VB0, GB0)]
    win0 = jnp.concatenate(cols0 + [w_in0[:, :UB0]], axis=1).astype(BF16)
    cols1 = [w_in1[:, o + p * 256:o + (p + 1) * 256] for p in range(d // 256) for o in (0, d)]
    win1 = jnp.concatenate(cols1 + [w_in1[:, 2 * d:]], axis=1).astype(BF16)
    g0, g1, fg, subg = row(norm_g0), row(norm_g1), row(final_g), row(subln_g0)
    lng, lnb = row(gv_ln_g0), row(gv_ln_b0)
    bdw, clng, clnb = row(b_dw1), row(conv_ln_g1), row(conv_ln_b1)
    wdw = w_dw1.astype(F32)

    rbs = (rel_bias - rel_bias[FAR_BUCKET:FAR_BUCKET + 1]).astype(F32)
    assert np.all(_np_bucket(-np.arange(TM + 1, 2 * max(t, past + dec_seq))) == FAR_BUCKET)
    kk = np.arange(TM)[:, None]
    qq = np.arange(TM)[None, :]
    own = jnp.where((kk // CHUNK <= qq // CHUNK)[..., None], _lookup(rbs, _np_bucket(kk - qq)), NEG)
    prev = _lookup(rbs, _np_bucket(kk - qq - TM))
    bias_p = jnp.transpose(jnp.stack([own, prev]), (3, 0, 1, 2)) * LOG2E

    tril = np.tril(np.ones((B_CHUNK, B_CHUNK), dtype=bool))
    wm = jnp.where(tril[None], w_s0, 0)
    bs = jnp.repeat(b_s0.T.astype(F32), GROUP_DIM, axis=1)

    y_prompt, k0p, v0p, tail_p = _prompt(x_prompt, lamv, g0, win0, bias_p, subg, lng, lnb, wm.astype(BF16), bs, wout0,
                                         g1, win1, wdw, bdw, clng, clnb, wout1, fg)

    pad_rows = 128
    kpos = np.arange(past + dec_seq)[:, None]
    qpos = past + np.arange(dec_seq)[None, :]
    bsmp = jnp.transpose(_lookup(rbs, _np_bucket(kpos - qpos)), (0, 2, 1))
    bsmp = jnp.broadcast_to(bsmp[:, :, None, :], (past + dec_seq, HEADS, 2, dec_seq)).reshape(past + dec_seq, -1)
    bias_s = jnp.pad(bsmp, ((0, pad_rows - dec_seq), (0, 0)), constant_values=NEG)
    wbd = jnp.stack([jnp.kron(jnp.eye(nb, dtype=F32), wm[g, :dec_seq, :dec_seq]) for g in range(GROUPS)])
    bs_s = jnp.tile(bs[:dec_seq], (nb, 1))

    zs, ys, zvs, us = _sample(x_sample, cache_k0, cache_v0, state_conv1, lamv, g0, win0, bias_s, subg, lng, lnb,
                              wbd.astype(BF16), bs_s, wout0, g1, win1, wdw, bdw, clng, clnb, wout1, fg)

    keep = CONV_WIDTH - 1
    return (y_prompt,
            ys.reshape(nb, dec_seq, d),
            k0p,
            v0p,
            tail_p[:, CONV_OFF:],
            zs[:, P_K0:P_K0 + A_WIDTH].reshape(nb, dec_seq, HEADS, HEAD_DIM),
            zs[:, P_V0:P_V0 + A_WIDTH].reshape(nb, dec_seq, HEADS, HEAD_DIM),
            zvs.reshape(nb, dec_seq, B_WIDTH),
            jnp.concatenate([state_conv1, us.reshape(nb, dec_seq, d)], axis=1)[:, -keep:])
```

```python
import functools
import math

import numpy as np
import jax
import jax.numpy as jnp
from jax import lax
from jax.experimental import pallas as pl
from jax.experimental.pallas import tpu as pltpu

F32 = jnp.float32
BF16 = jnp.bfloat16

CHUNK = 64
HEADS = 4
HALF = 64
HEAD_DIM = 2 * HALF
A_WIDTH = HEADS * HEAD_DIM
LAMBDA_INIT = 0.2
N_BUCKETS = 32
MAX_DISTANCE = 128
GROUPS = 4
GROUP_DIM = 128
B_WIDTH = GROUPS * GROUP_DIM
B_CHUNK = 128
CONV_WIDTH = 31
RMS_EPS = 1e-6
LN_EPS = 1e-5

LANE = 128
MXU_COLS = 256
VMEM_LIMIT = 56 * 1024 * 1024

UB0, VB0, GB0 = 4 * 512, 5 * 512, 6 * 512
L0_IN = 7 * 512
P_GMLP0 = 0
P_Q0, P_K0, P_V0, P_GA0 = (3 * B_WIDTH + i * 512 for i in range(4))

TM = 256
CONV_PAD = 32
CONV_OFF = CONV_PAD - (CONV_WIDTH - 1)
CONV_RB = 64
EXP_ROWS = 64
LN_ROWS = 64
PROJ_COLS = MXU_COLS
VT_ROWS = HEAD_DIM + 16
LOG2E = 1.4426950408889634
Q_SCALE_LOG2 = HALF ** -0.5 * LOG2E
NEG = -1e30


def _np_bucket(rel):
    nb = N_BUCKETS // 2
    ret = np.where(rel > 0, nb, 0)
    n = np.abs(rel)
    max_exact = nb // 2
    nf = np.maximum(n, 1).astype(np.float32)
    large = max_exact + (np.log(nf / np.float32(max_exact)) / np.float32(math.log(MAX_DISTANCE / max_exact))
                         * np.float32(nb - max_exact)).astype(np.int32)
    large = np.minimum(large, nb - 1)
    return (ret + np.where(n < max_exact, n, large)).astype(np.int32)


FAR_BUCKET = N_BUCKETS // 2 - 1


def _lookup(table, idx):
    idx_c = jnp.asarray(idx)[..., None]
    out = jnp.zeros(idx.shape + (table.shape[1],), F32)
    for b in np.unique(idx):
        out = jnp.where(idx_c == int(b), table[int(b)], out)
    return out


def _rms(x, g):
    return x * lax.rsqrt(jnp.mean(x * x, axis=-1, keepdims=True) + RMS_EPS) * g


def _ln(x, g, b):
    mu = jnp.mean(x, axis=-1, keepdims=True)
    xc = x - mu
    var = jnp.mean(xc * xc, axis=-1, keepdims=True)
    return xc * lax.rsqrt(var + LN_EPS) * g + b


def _lam(lamv):
    s1 = jnp.sum(lamv[0:1] * lamv[1:2], axis=-1, keepdims=True)
    s2 = jnp.sum(lamv[2:3] * lamv[3:4], axis=-1, keepdims=True)
    return jnp.exp(s1) - jnp.exp(s2) + LAMBDA_INIT


def _gmlp_cols(z_ref, g):
    base = P_GMLP0 + 3 * g * GROUP_DIM
    return tuple(z_ref[:, base + j * GROUP_DIM:base + (j + 1) * GROUP_DIM] for j in range(3))


def _dot(a, b):
    return jnp.dot(a, b, preferred_element_type=F32)


def _dot_nt(a, b):
    return lax.dot_general(a, b, (((1,), (1,)), ((), ())), preferred_element_type=F32)


def _conv_rows(buf_ref, c, base, rows, wdw_ref, bdw_ref):
    lanes = slice(c * LANE, (c + 1) * LANE)
    acc = jnp.broadcast_to(bdw_ref[:, lanes], (rows, LANE))
    for tap in range(CONV_WIDTH):
        lo = base + tap + CONV_OFF
        acc = acc + wdw_ref[tap:tap + 1, lanes] * buf_ref[c, lo:lo + rows, :]
    return acc


def _prompt_kernel(lamv_ref, x_ref, g0_ref, win_ref, bias_ref, subg_ref, lng_ref, lnb_ref, wm_ref, bs_ref,
                   wout_ref, g1_ref, win1_ref, wdw_ref, bdw_ref, clng_ref, clnb_ref, wout1_ref, fg_ref,
                   y_ref, k_ref, v_ref, tail_ref,
                   z_ref, kbf_ref, vt_ref, qm_ref, m_ref, acc_ref, s_ref, p_ref, mix_ref,
                   buf_ref, conv_ref, gate_ref, y0_ref, y0p_ref, cn_ref, *, nt, ntiles):
    step = pl.program_id(0)
    i = jnp.minimum(step, ntiles - 1) % nt
    d = x_ref.shape[-1]
    ngroups = d // LANE

    @pl.when(step == 0)
    def _():
        buf_ref[...] = jnp.zeros(buf_ref.shape, F32)
        gate_ref[...] = jnp.zeros(gate_ref.shape, F32)
        y0p_ref[...] = jnp.zeros(y0p_ref.shape, F32)

    x = x_ref[0]
    xn = _rms(x, g0_ref[...]).astype(BF16)
    blocks = [(c, r) for c in range(ngroups) for r in range(TM // CONV_RB)]
    nchunks = L0_IN // PROJ_COLS
    for j in range(nchunks):
        cols = slice(j * PROJ_COLS, (j + 1) * PROJ_COLS)
        z_ref[:, cols] = _dot(xn, win_ref[:, cols])
        for c, r in blocks[j * len(blocks) // nchunks:(j + 1) * len(blocks) // nchunks]:
            conv_ref[r * CONV_RB:(r + 1) * CONV_RB, c * LANE:(c + 1) * LANE] = _conv_rows(
                buf_ref, c, r * CONV_RB, CONV_RB, wdw_ref, bdw_ref)

    for g in range(GROUPS):
        sl = slice(g * GROUP_DIM, (g + 1) * GROUP_DIM)
        ub, vb, gb = _gmlp_cols(z_ref, g)
        zv = _ln(jax.nn.gelu(vb), lng_ref[:, sl], lnb_ref[:, sl]).astype(BF16)
        zu = jax.nn.gelu(ub)
        gate = jax.nn.silu(gb)
        for c in range(TM // B_CHUNK):
            rows = slice(c * B_CHUNK, (c + 1) * B_CHUNK)
            sg = _dot(wm_ref[g], zv[rows]) + bs_ref[:, sl]
            mix_ref[rows, A_WIDTH + g * GROUP_DIM:A_WIDTH + (g + 1) * GROUP_DIM] = (
                zu[rows] * sg * gate[rows]).astype(BF16)

    k = z_ref[:, P_K0:P_K0 + A_WIDTH]
    v = z_ref[:, P_V0:P_V0 + A_WIDTH]
    k_ref[0] = k.reshape(TM, HEADS, HEAD_DIM)
    v_ref[0] = v.reshape(TM, HEADS, HEAD_DIM)
    kbf_ref[i] = k.astype(BF16)
    for h in range(HEADS):
        vt_ref[i, h, 0:HEAD_DIM, :] = v[:, h * HEAD_DIM:(h + 1) * HEAD_DIM].T.astype(BF16)
        vt_ref[i, h, HEAD_DIM:VT_ROWS, :] = jnp.ones((VT_ROWS - HEAD_DIM, TM), BF16)

    lo = lax.broadcasted_iota(jnp.int32, (TM, HEAD_DIM), 1) < HALF
    for h in range(HEADS):
        q = z_ref[:, P_Q0 + h * HEAD_DIM:P_Q0 + (h + 1) * HEAD_DIM] * Q_SCALE_LOG2
        qm_ref[2 * h] = jnp.where(lo, q, 0.0).astype(BF16)
        qm_ref[2 * h + 1] = jnp.where(lo, 0.0, q).astype(BF16)

    units = range(2 * HEADS)

    def attend(tiles, first):
        m_new, alpha = {}, {}
        for u in units:
            h = u // 2
            mt = None
            for t, (j, bias_idx) in enumerate(tiles):
                s = _dot_nt(kbf_ref[j, :, h * HEAD_DIM:(h + 1) * HEAD_DIM], qm_ref[u])
                if bias_idx is not None:
                    s = s + bias_ref[h, bias_idx]
                s_ref[t, u] = s
                smax = jnp.max(s, axis=0, keepdims=True)
                mt = smax if mt is None else jnp.maximum(mt, smax)
            if first:
                m_new[u] = mt
            else:
                m_old = m_ref[u]
                m_new[u] = jnp.maximum(m_old, mt)
                alpha[u] = jnp.exp2(m_old - m_new[u])
            m_ref[u] = m_new[u]
        for u in units:
            for t in range(len(tiles)):
                for r in range(TM // EXP_ROWS):
                    rows = slice(r * EXP_ROWS, (r + 1) * EXP_ROWS)
                    p_ref[t, u, rows, :] = jnp.exp2(s_ref[t, u, rows, :] - m_new[u]).astype(BF16)
        for u in units:
            pv = _dot(vt_ref[tiles[0][0], u // 2], p_ref[0, u])
            for t in range(1, len(tiles)):
                pv = pv + _dot(vt_ref[tiles[t][0], u // 2], p_ref[t, u])
            acc_ref[u] = pv if first else alpha[u] * acc_ref[u] + pv

    @pl.when(i == 0)
    def _():
        attend([(i, 0)], True)

    @pl.when(i > 0)
    def _():
        attend([(i, 0), (i - 1, 1)], True)

    def far_pair(jj, carry):
        attend([(2 * jj, None), (2 * jj + 1, None)], False)
        return carry

    n_far = jnp.maximum(i - 1, 0)
    lax.fori_loop(0, n_far // 2, far_pair, 0)

    @pl.when(n_far % 2 == 1)
    def _():
        attend([(n_far - 1, None)], False)

    y0_ref[...] = x + _dot(mix_ref[:, A_WIDTH:], wout_ref[A_WIDTH:, :])

    lam = _lam(lamv_ref[...])
    for h in range(HEADS):
        on1 = acc_ref[2 * h, 0:HEAD_DIM, :] / acc_ref[2 * h, HEAD_DIM:HEAD_DIM + 1, :]
        on2 = acc_ref[2 * h + 1, 0:HEAD_DIM, :] / acc_ref[2 * h + 1, HEAD_DIM:HEAD_DIM + 1, :]
        o = (on1 - lam * on2).T
        o = _rms(o, subg_ref[...]) * (1.0 - LAMBDA_INIT)
        ga = z_ref[:, P_GA0 + h * HEAD_DIM:P_GA0 + (h + 1) * HEAD_DIM]
        mix_ref[:, h * HEAD_DIM:(h + 1) * HEAD_DIM] = (o * jax.nn.silu(ga)).astype(BF16)

    y0 = y0_ref[...] + _dot(mix_ref[:, 0:A_WIDTH], wout_ref[0:A_WIDTH, :])
    y0_ref[...] = y0

    xn1 = _rms(y0, g1_ref[...]).astype(BF16)
    starts_stream = i == 0
    nrow = TM // LN_ROWS
    n_glu = 2 * d // PROJ_COLS
    for j in range(3 * d // PROJ_COLS):
        cols = slice(j * PROJ_COLS, (j + 1) * PROJ_COLS)
        z_ref[:, cols] = _dot(xn1, win1_ref[:, cols])
        if j < nrow:
            rows = slice(j * LN_ROWS, (j + 1) * LN_ROWS)
            cn_ref[rows, :] = (jax.nn.silu(_ln(conv_ref[rows, :], clng_ref[...], clnb_ref[...]))
                               * gate_ref[rows, :]).astype(BF16)
        if j == nrow:
            y_ref[0] = _rms(y0p_ref[...] + _dot(cn_ref[...], wout1_ref[...]), fg_ref[...])
        if j < n_glu and j % 2 == 1:
            for c in (j - 1, j):
                a0 = (j - 1) * PROJ_COLS + (c % 2) * LANE
                tail = buf_ref[c, TM:TM + CONV_PAD, :]
                tail_ref[0, :, c * LANE:(c + 1) * LANE] = tail
                buf_ref[c, 0:CONV_PAD, :] = jnp.where(starts_stream, 0.0, tail)
                buf_ref[c, CONV_PAD:CONV_PAD + TM, :] = (z_ref[:, a0:a0 + LANE]
                                                         * jax.nn.sigmoid(z_ref[:, a0 + PROJ_COLS:a0 + PROJ_COLS + LANE]))
        if j >= n_glu:
            gcols = slice((j - n_glu) * PROJ_COLS, (j - n_glu + 1) * PROJ_COLS)
            gate_ref[:, gcols] = jax.nn.silu(z_ref[:, cols])
    y0p_ref[...] = y0


def _prompt(x, lamv, g0, win, bias, subg, lng, lnb, wm, bs, wout, g1, win1, wdw, bdw, clng, clnb, wout1, fg):
    bsz, t, d = x.shape
    nt = t // TM
    ntiles = bsz * nt
    const = lambda shape: pl.BlockSpec(shape, lambda s: (0,) * len(shape))
    cur = lambda s: jnp.minimum(s, ntiles - 1)
    done = lambda s: jnp.maximum(s - 1, 0)
    kv_row = pl.BlockSpec((1, TM, HEADS, HEAD_DIM), lambda s: (cur(s) // nt, cur(s) % nt, 0, 0))
    return pl.pallas_call(
        functools.partial(_prompt_kernel, nt=nt, ntiles=ntiles),
        grid=(ntiles + 1,),
        in_specs=[const((4, HALF)), pl.BlockSpec((1, TM, d), lambda s: (cur(s) // nt, cur(s) % nt, 0)),
                  const((1, d)), const((d, L0_IN)),
                  const((HEADS, 2, TM, TM)), const((1, HEAD_DIM)), const((1, B_WIDTH)), const((1, B_WIDTH)),
                  const((GROUPS, B_CHUNK, B_CHUNK)), const((B_CHUNK, B_WIDTH)), const((A_WIDTH + B_WIDTH, d)),
                  const((1, d)), const((d, 3 * d)), const((CONV_WIDTH, d)), const((1, d)),
                  const((1, d)), const((1, d)), const((d, d)), const((1, d))],
        out_specs=[pl.BlockSpec((1, TM, d), lambda s: (done(s) // nt, done(s) % nt, 0)), kv_row, kv_row,
                   pl.BlockSpec((1, CONV_PAD, d), lambda s: (done(s) // nt, 0, 0))],
        out_shape=[jax.ShapeDtypeStruct((bsz, t, d), F32),
                   jax.ShapeDtypeStruct((bsz, t, HEADS, HEAD_DIM), F32),
                   jax.ShapeDtypeStruct((bsz, t, HEADS, HEAD_DIM), F32),
                   jax.ShapeDtypeStruct((bsz, CONV_PAD, d), F32)],
        scratch_shapes=[pltpu.VMEM((TM, L0_IN), F32),
                        pltpu.VMEM((nt, TM, A_WIDTH), BF16),
                        pltpu.VMEM((nt, HEADS, VT_ROWS, TM), BF16),
                        pltpu.VMEM((2 * HEADS, TM, HEAD_DIM), BF16),
                        pltpu.VMEM((2 * HEADS, 1, TM), F32),
                        pltpu.VMEM((2 * HEADS, VT_ROWS, TM), F32),
                        pltpu.VMEM((2, 2 * HEADS, TM, TM), F32),
                        pltpu.VMEM((2, 2 * HEADS, TM, TM), BF16),
                        pltpu.VMEM((TM, A_WIDTH + B_WIDTH), BF16),
                        pltpu.VMEM((d // LANE, CONV_PAD + TM, LANE), F32),
                        pltpu.VMEM((TM, d), F32),
                        pltpu.VMEM((TM, d), F32),
                        pltpu.VMEM((TM, d), F32),
                        pltpu.VMEM((TM, d), F32),
                        pltpu.VMEM((TM, d), BF16)],
        compiler_params=pltpu.CompilerParams(dimension_semantics=("arbitrary",), vmem_limit_bytes=VMEM_LIMIT),
        name="prompt",
    )(lamv, x, g0, win, bias, subg, lng, lnb, wm, bs, wout, g1, win1, wdw, bdw, clng, clnb, wout1, fg)


def _s_proj_kernel(x_ref, g0_ref, win_ref, z_ref):
    z_ref[...] = _dot(_rms(x_ref[...], g0_ref[...]).astype(BF16), win_ref[...])


def _s_attn_kernel(lamv_ref, z_ref, ck_ref, cv_ref, bias_ref, o_ref, *, dec_seq, pad_rows):
    ncol = HEADS * 2 * dec_seq
    q = z_ref[:, P_Q0:P_Q0 + A_WIDTH] * (HALF ** -0.5)
    q_rep = jnp.concatenate([q] * (2 * HEADS), axis=0)
    row = lax.broadcasted_iota(jnp.int32, (ncol, A_WIDTH), 0)
    col = lax.broadcasted_iota(jnp.int32, (ncol, A_WIDTH), 1)
    q_bd = jnp.where(col // HALF == row // dec_seq, q_rep, 0.0).astype(BF16)

    zeros = jnp.zeros((pad_rows - dec_seq, A_WIDTH), F32)
    ck = ck_ref[0].reshape(ck_ref.shape[1], A_WIDTH)
    cv = cv_ref[0].reshape(cv_ref.shape[1], A_WIDTH)
    k_all = jnp.concatenate([ck, z_ref[:, P_K0:P_K0 + A_WIDTH], zeros], axis=0).astype(BF16)
    v_all = jnp.concatenate([cv, z_ref[:, P_V0:P_V0 + A_WIDTH], zeros], axis=0).astype(BF16)

    s = _dot_nt(k_all, q_bd) + bias_ref[...]
    p = jnp.exp(s - jnp.max(s, axis=0, keepdims=True))
    p = p / jnp.sum(p, axis=0, keepdims=True)
    w = p - _lam(lamv_ref[...]) * pltpu.roll(p, ncol - dec_seq, axis=1)
    o_all = _dot(w.T.astype(BF16), v_all)
    for h in range(HEADS):
        o_ref[:, h * HEAD_DIM:(h + 1) * HEAD_DIM] = o_all[2 * h * dec_seq:(2 * h + 1) * dec_seq,
                                                          h * HEAD_DIM:(h + 1) * HEAD_DIM]


def _s_rest_kernel(x_ref, z_ref, o_ref, subg_ref, lng_ref, lnb_ref, wbd_ref, bs_ref, wout0_ref,
                   g1_ref, win1_ref, st_ref, wdw_ref, bdw_ref, clng_ref, clnb_ref, wout1_ref, fg_ref,
                   y_ref, zv_ref, u_ref, mix_ref, buf_ref, conv_ref, *, dec_seq):
    d = x_ref.shape[-1]
    nb = st_ref.shape[0]
    for h in range(HEADS):
        sl = slice(h * HEAD_DIM, (h + 1) * HEAD_DIM)
        o = _rms(o_ref[:, sl], subg_ref[...]) * (1.0 - LAMBDA_INIT)
        mix_ref[:, sl] = (o * jax.nn.silu(z_ref[:, P_GA0 + h * HEAD_DIM:P_GA0 + (h + 1) * HEAD_DIM])).astype(BF16)
    for g in range(GROUPS):
        sl = slice(g * GROUP_DIM, (g + 1) * GROUP_DIM)
        ub, vb, gb = _gmlp_cols(z_ref, g)
        zv = _ln(jax.nn.gelu(vb), lng_ref[:, sl], lnb_ref[:, sl])
        zv_ref[:, sl] = zv
        sg = _dot(wbd_ref[g], zv.astype(BF16)) + bs_ref[:, sl]
        mix_ref[:, A_WIDTH + g * GROUP_DIM:A_WIDTH + (g + 1) * GROUP_DIM] = (
            jax.nn.gelu(ub) * sg * jax.nn.silu(gb)).astype(BF16)
    y0 = x_ref[...] + _dot(mix_ref[...], wout0_ref[...])

    z1 = _dot(_rms(y0, g1_ref[...]).astype(BF16), win1_ref[...])
    u = jnp.concatenate([z1[:, a0:a0 + PROJ_COLS] * jax.nn.sigmoid(z1[:, a0 + PROJ_COLS:a0 + 2 * PROJ_COLS])
                         for a0 in range(0, 2 * d, 2 * PROJ_COLS)], axis=1)
    u_ref[...] = u
    ngroups = d // LANE
    for b in range(nb):
        for c in range(ngroups):
            lanes = slice(c * LANE, (c + 1) * LANE)
            buf_ref[b * ngroups + c, 0:CONV_PAD, :] = st_ref[b, :, lanes]
            buf_ref[b * ngroups + c, CONV_PAD:CONV_PAD + dec_seq, :] = u[b * dec_seq:(b + 1) * dec_seq, lanes]
    for b in range(nb):
        for c in range(ngroups):
            conv_ref[b * dec_seq:(b + 1) * dec_seq, c * LANE:(c + 1) * LANE] = _conv_rows(
                buf_ref.at[pl.ds(b * ngroups, ngroups)], c, 0, dec_seq, wdw_ref, bdw_ref)
    c = jax.nn.silu(_ln(conv_ref[...], clng_ref[...], clnb_ref[...])) * jax.nn.silu(z1[:, 2 * d:3 * d])
    y1 = y0 + _dot(c.astype(BF16), wout1_ref[...])
    y_ref[...] = _rms(y1, fg_ref[...])


def _sample(xs, cache_k, cache_v, state, lamv, g0, win0, bias_s, subg, lng, lnb, wbd, bs_s, wout0,
            g1, win1, wdw, bdw, clng, clnb, wout1, fg):
    nb, dec_seq, d = xs.shape
    rows = nb * dec_seq
    past = cache_k.shape[1]
    pad_rows = bias_s.shape[0] - past
    x2 = xs.reshape(rows, d)
    z = pl.pallas_call(_s_proj_kernel, out_shape=jax.ShapeDtypeStruct((rows, L0_IN), F32),
                       compiler_params=pltpu.CompilerParams(vmem_limit_bytes=VMEM_LIMIT),
                       name="s_proj")(x2, g0, win0)

    ncol = HEADS * 2 * dec_seq
    o = pl.pallas_call(
        functools.partial(_s_attn_kernel, dec_seq=dec_seq, pad_rows=pad_rows),
        grid=(nb,),
        in_specs=[pl.BlockSpec((4, HALF), lambda b: (0, 0)),
                  pl.BlockSpec((dec_seq, L0_IN), lambda b: (b, 0)),
                  pl.BlockSpec((1, past, HEADS, HEAD_DIM), lambda b: (b, 0, 0, 0)),
                  pl.BlockSpec((1, past, HEADS, HEAD_DIM), lambda b: (b, 0, 0, 0)),
                  pl.BlockSpec((past + pad_rows, ncol), lambda b: (0, 0))],
        out_specs=pl.BlockSpec((dec_seq, A_WIDTH), lambda b: (b, 0)),
        out_shape=jax.ShapeDtypeStruct((rows, A_WIDTH), F32),
        compiler_params=pltpu.CompilerParams(dimension_semantics=("arbitrary",), vmem_limit_bytes=VMEM_LIMIT),
        name="s_attn",
    )(lamv, z, cache_k, cache_v, bias_s)

    state_pad = jnp.pad(state, ((0, 0), (CONV_OFF, 0), (0, 0)))
    y, zv, u = pl.pallas_call(
        functools.partial(_s_rest_kernel, dec_seq=dec_seq),
        out_shape=[jax.ShapeDtypeStruct((rows, d), F32),
                   jax.ShapeDtypeStruct((rows, B_WIDTH), F32),
                   jax.ShapeDtypeStruct((rows, d), F32)],
        scratch_shapes=[pltpu.VMEM((rows, A_WIDTH + B_WIDTH), BF16),
                        pltpu.VMEM((nb * (d // LANE), CONV_PAD + dec_seq, LANE), F32),
                        pltpu.VMEM((rows, d), F32)],
        compiler_params=pltpu.CompilerParams(vmem_limit_bytes=VMEM_LIMIT),
        name="s_rest",
    )(x2, z, o, subg, lng, lnb, wbd, bs_s, wout0, g1, win1, state_pad, wdw, bdw, clng, clnb, wout1, fg)
    return z, y, zv, u


def kernel(x_prompt, x_sample, cache_k0, cache_v0, state_conv1, rel_bias, norm_g0, w_in0, lambda_q1, lambda_k1,
           lambda_q2, lambda_k2, subln_g0, gv_ln_g0, gv_ln_b0, w_s0, b_s0, w_out0, norm_g1, w_in1, w_dw1, b_dw1,
           conv_ln_g1, conv_ln_b1, w_out1, final_g):
    bsz, t, d = x_prompt.shape
    nb, dec_seq, _ = x_sample.shape
    past = cache_k0.shape[1]
    assert t % TM == 0 and TM % B_CHUNK == 0 and TM % CHUNK == 0 and TM >= MAX_DISTANCE
    assert d % (2 * PROJ_COLS) == 0 and TM // LN_ROWS < 3 * d // PROJ_COLS
    assert dec_seq <= B_CHUNK and dec_seq % 8 == 0 and HEADS * 2 * dec_seq == LANE
    assert past % CHUNK == 0 and (past + dec_seq - 1) // CHUNK == past // CHUNK

    row = lambda a: a.reshape(1, -1).astype(F32)
    lamv = jnp.stack([lambda_q1, lambda_k1, lambda_q2, lambda_k2]).astype(F32)
    wout0, wout1 = w_out0.astype(BF16), w_out1.astype(BF16)
    cols0 = [w_in0[:, o + g * GROUP_DIM:o + (g + 1) * GROUP_DIM] for g in range(GROUPS) for o in (UB0, VB0, GB0)]
    win0 = jnp.concatenate(cols0 + [w_in0[:, :UB0]], axis=1).astype(BF16)
    cols1 = [w_in1[:, o + p * PROJ_COLS:o + (p + 1) * PROJ_COLS] for p in range(d // PROJ_COLS) for o in (0, d)]
    win1 = jnp.concatenate(cols1 + [w_in1[:, 2 * d:]], axis=1).astype(BF16)
    g0, g1, fg, subg = row(norm_g0), row(norm_g1), row(final_g), row(subln_g0)
    lng, lnb = row(gv_ln_g0), row(gv_ln_b0)
    bdw, clng, clnb = row(b_dw1), row(conv_ln_g1), row(conv_ln_b1)
    wdw = w_dw1.astype(F32)

    rbs = (rel_bias - rel_bias[FAR_BUCKET:FAR_BUCKET + 1]).astype(F32)
    assert np.all(_np_bucket(-np.arange(TM + 1, 2 * max(t, past + dec_seq))) == FAR_BUCKET)
    kk = np.arange(TM)[:, None]
    qq = np.arange(TM)[None, :]
    own = jnp.where((kk // CHUNK <= qq // CHUNK)[..., None], _lookup(rbs, _np_bucket(kk - qq)), NEG)
    prev = _lookup(rbs, _np_bucket(kk - qq - TM))
    bias_p = jnp.transpose(jnp.stack([own, prev]), (3, 0, 1, 2)) * LOG2E

    tril = np.tril(np.ones((B_CHUNK, B_CHUNK), dtype=bool))
    wm = jnp.where(tril[None], w_s0, 0)
    bs = jnp.repeat(b_s0.T.astype(F32), GROUP_DIM, axis=1)

    y_prompt, k0p, v0p, tail_p = _prompt(x_prompt, lamv, g0, win0, bias_p, subg, lng, lnb, wm.astype(BF16), bs, wout0,
                                         g1, win1, wdw, bdw, clng, clnb, wout1, fg)

    pad_rows = LANE
    kpos = np.arange(past + dec_seq)[:, None]
    qpos = past + np.arange(dec_seq)[None, :]
    bsmp = jnp.transpose(_lookup(rbs, _np_bucket(kpos - qpos)), (0, 2, 1))
    bsmp = jnp.broadcast_to(bsmp[:, :, None, :], (past + dec_seq, HEADS, 2, dec_seq)).reshape(past + dec_seq, -1)
    bias_s = jnp.pad(bsmp, ((0, pad_rows - dec_seq), (0, 0)), constant_values=NEG)
    wbd = jnp.stack([jnp.kron(jnp.eye(nb, dtype=F32), wm[g, :dec_seq, :dec_seq]) for g in range(GROUPS)])
    bs_s = jnp.tile(bs[:dec_seq], (nb, 1))

    zs, ys, zvs, us = _sample(x_sample, cache_k0, cache_v0, state_conv1, lamv, g0, win0, bias_s, subg, lng, lnb,
                              wbd.astype(BF16), bs_s, wout0, g1, win1, wdw, bdw, clng, clnb, wout1, fg)

    keep = CONV_WIDTH - 1
    return (y_prompt,
            ys.reshape(nb, dec_seq, d),
            k0p,
            v0p,
            tail_p[:, CONV_OFF:],
            zs[:, P_K0:P_K0 + A_WIDTH].reshape(nb, dec_seq, HEADS, HEAD_DIM),
            zs[:, P_V0:P_V0 + A_WIDTH].reshape(nb, dec_seq, HEADS, HEAD_DIM),
            zvs.reshape(nb, dec_seq, B_WIDTH),
            jnp.concatenate([state_conv1, us.reshape(nb, dec_seq, d)], axis=1)[:, -keep:])
```

```python
import functools
import math

import numpy as np
import jax
import jax.numpy as jnp
from jax import lax
from jax.experimental import pallas as pl
from jax.experimental.pallas import tpu as pltpu

F32 = jnp.float32
BF16 = jnp.bfloat16

CHUNK = 64
HEADS = 4
HALF = 64
HEAD_DIM = 2 * HALF
A_WIDTH = HEADS * HEAD_DIM
LAMBDA_INIT = 0.2
N_BUCKETS = 32
MAX_DISTANCE = 128
GROUPS = 4
GROUP_DIM = 128
B_WIDTH = GROUPS * GROUP_DIM
B_CHUNK = 128
CONV_WIDTH = 31
RMS_EPS = 1e-6
LN_EPS = 1e-5

LANE = 128
MXU_COLS = 256
VMEM_LIMIT = 56 * 1024 * 1024

UB0, VB0, GB0 = 4 * 512, 5 * 512, 6 * 512
L0_IN = 7 * 512
P_GMLP0 = 0
P_Q0, P_K0, P_V0, P_GA0 = (3 * B_WIDTH + i * 512 for i in range(4))

TM = 256
CONV_PAD = 32
CONV_OFF = CONV_PAD - (CONV_WIDTH - 1)
CONV_RB = 64
LN_ROWS = 64
PROJ_COLS = MXU_COLS
VT_ROWS = HEAD_DIM + 16
LOG2E = 1.4426950408889634
Q_SCALE_LOG2 = HALF ** -0.5 * LOG2E
NEG = -1e30


def _np_bucket(rel):
    nb = N_BUCKETS // 2
    ret = np.where(rel > 0, nb, 0)
    n = np.abs(rel)
    max_exact = nb // 2
    nf = np.maximum(n, 1).astype(np.float32)
    large = max_exact + (np.log(nf / np.float32(max_exact)) / np.float32(math.log(MAX_DISTANCE / max_exact))
                         * np.float32(nb - max_exact)).astype(np.int32)
    large = np.minimum(large, nb - 1)
    return (ret + np.where(n < max_exact, n, large)).astype(np.int32)


FAR_BUCKET = N_BUCKETS // 2 - 1


def _lookup(table, idx):
    idx_c = jnp.asarray(idx)[..., None]
    out = jnp.zeros(idx.shape + (table.shape[1],), F32)
    for b in np.unique(idx):
        out = jnp.where(idx_c == int(b), table[int(b)], out)
    return out


def _rms(x, g):
    return x * lax.rsqrt(jnp.mean(x * x, axis=-1, keepdims=True) + RMS_EPS) * g


def _ln(x, g, b):
    mu = jnp.mean(x, axis=-1, keepdims=True)
    xc = x - mu
    var = jnp.mean(xc * xc, axis=-1, keepdims=True)
    return xc * lax.rsqrt(var + LN_EPS) * g + b


def _lam(lamv):
    s1 = jnp.sum(lamv[0:1] * lamv[1:2], axis=-1, keepdims=True)
    s2 = jnp.sum(lamv[2:3] * lamv[3:4], axis=-1, keepdims=True)
    return jnp.exp(s1) - jnp.exp(s2) + LAMBDA_INIT


def _gmlp_cols(z_ref, g):
    base = P_GMLP0 + 3 * g * GROUP_DIM
    return tuple(z_ref[:, base + j * GROUP_DIM:base + (j + 1) * GROUP_DIM] for j in range(3))


def _dot(a, b):
    return jnp.dot(a, b, preferred_element_type=F32)


def _dot_nt(a, b):
    return lax.dot_general(a, b, (((1,), (1,)), ((), ())), preferred_element_type=F32)


def _conv_rows(buf_ref, c, base, rows, wdw_ref, bdw_ref):
    lanes = slice(c * LANE, (c + 1) * LANE)
    acc = jnp.broadcast_to(bdw_ref[:, lanes], (rows, LANE))
    for tap in range(CONV_WIDTH):
        lo = base + tap + CONV_OFF
        acc = acc + wdw_ref[tap:tap + 1, lanes] * buf_ref[c, lo:lo + rows, :]
    return acc


def _prompt_kernel(lamv_ref, x_ref, g0_ref, win_ref, bias_ref, subg_ref, lng_ref, lnb_ref, wm_ref, bs_ref,
                   wout_ref, g1_ref, win1_ref, wdw_ref, bdw_ref, clng_ref, clnb_ref, wout1_ref, fg_ref,
                   y_ref, k_ref, v_ref, tail_ref,
                   z_ref, kbf_ref, vt_ref, qm_ref, m_ref, acc_ref, s_ref, p_ref, mix_ref,
                   buf_ref, conv_ref, gate_ref, y0_ref, y0p_ref, cn_ref, *, nt, ntiles):
    step = pl.program_id(0)
    i = jnp.minimum(step, ntiles - 1) % nt
    d = x_ref.shape[-1]
    ngroups = d // LANE

    @pl.when(step == 0)
    def _():
        buf_ref[...] = jnp.zeros(buf_ref.shape, F32)
        gate_ref[...] = jnp.zeros(gate_ref.shape, F32)
        y0p_ref[...] = jnp.zeros(y0p_ref.shape, F32)

    x = x_ref[0]
    xn = _rms(x, g0_ref[...]).astype(BF16)
    blocks = [(c, r) for c in range(ngroups) for r in range(TM // CONV_RB)]
    nchunks = L0_IN // PROJ_COLS
    for j in range(nchunks):
        cols = slice(j * PROJ_COLS, (j + 1) * PROJ_COLS)
        z_ref[:, cols] = _dot(xn, win_ref[:, cols])
        for c, r in blocks[j * len(blocks) // nchunks:(j + 1) * len(blocks) // nchunks]:
            conv_ref[r * CONV_RB:(r + 1) * CONV_RB, c * LANE:(c + 1) * LANE] = _conv_rows(
                buf_ref, c, r * CONV_RB, CONV_RB, wdw_ref, bdw_ref)

    for g in range(GROUPS):
        sl = slice(g * GROUP_DIM, (g + 1) * GROUP_DIM)
        ub, vb, gb = _gmlp_cols(z_ref, g)
        zv = _ln(jax.nn.gelu(vb), lng_ref[:, sl], lnb_ref[:, sl]).astype(BF16)
        zu = jax.nn.gelu(ub)
        gate = jax.nn.silu(gb)
        for c in range(TM // B_CHUNK):
            rows = slice(c * B_CHUNK, (c + 1) * B_CHUNK)
            sg = _dot(wm_ref[g], zv[rows]) + bs_ref[:, sl]
            mix_ref[rows, A_WIDTH + g * GROUP_DIM:A_WIDTH + (g + 1) * GROUP_DIM] = (
                zu[rows] * sg * gate[rows]).astype(BF16)

    k = z_ref[:, P_K0:P_K0 + A_WIDTH]
    v = z_ref[:, P_V0:P_V0 + A_WIDTH]
    k_ref[0] = k.reshape(TM, HEADS, HEAD_DIM)
    v_ref[0] = v.reshape(TM, HEADS, HEAD_DIM)
    kbf_ref[i] = k.astype(BF16)
    for h in range(HEADS):
        vt_ref[i, h, 0:HEAD_DIM, :] = v[:, h * HEAD_DIM:(h + 1) * HEAD_DIM].T.astype(BF16)
        vt_ref[i, h, HEAD_DIM:VT_ROWS, :] = jnp.ones((VT_ROWS - HEAD_DIM, TM), BF16)

    lo = lax.broadcasted_iota(jnp.int32, (TM, HEAD_DIM), 1) < HALF
    for h in range(HEADS):
        q = z_ref[:, P_Q0 + h * HEAD_DIM:P_Q0 + (h + 1) * HEAD_DIM] * Q_SCALE_LOG2
        qm_ref[2 * h] = jnp.where(lo, q, 0.0).astype(BF16)
        qm_ref[2 * h + 1] = jnp.where(lo, 0.0, q).astype(BF16)

    units = range(2 * HEADS)

    def attend(tiles, first):
        m_new, alpha = {}, {}
        for u in units:
            h = u // 2
            mt = None
            for t, (j, bias_idx) in enumerate(tiles):
                s = _dot_nt(kbf_ref[j, :, h * HEAD_DIM:(h + 1) * HEAD_DIM], qm_ref[u])
                if bias_idx is not None:
                    s = s + bias_ref[h, bias_idx]
                s_ref[t, u] = s
                smax = jnp.max(s, axis=0, keepdims=True)
                mt = smax if mt is None else jnp.maximum(mt, smax)
            if first:
                m_new[u] = mt
            else:
                m_old = m_ref[u]
                m_new[u] = jnp.maximum(m_old, mt)
                alpha[u] = jnp.exp2(m_old - m_new[u])
            m_ref[u] = m_new[u]
        for u in units:
            for t in range(len(tiles)):
                p_ref[t, u] = jnp.exp2(s_ref[t, u] - m_new[u]).astype(BF16)
        for u in units:
            pv = _dot(vt_ref[tiles[0][0], u // 2], p_ref[0, u])
            for t in range(1, len(tiles)):
                pv = pv + _dot(vt_ref[tiles[t][0], u // 2], p_ref[t, u])
            acc_ref[u] = pv if first else alpha[u] * acc_ref[u] + pv

    @pl.when(i == 0)
    def _():
        attend([(i, 0)], True)

    @pl.when(i > 0)
    def _():
        attend([(i, 0), (i - 1, 1)], True)

    def far_pair(jj, carry):
        attend([(2 * jj, None), (2 * jj + 1, None)], False)
        return carry

    n_far = jnp.maximum(i - 1, 0)
    lax.fori_loop(0, n_far // 2, far_pair, 0)

    @pl.when(n_far % 2 == 1)
    def _():
        attend([(n_far - 1, None)], False)

    y0_ref[...] = x + _dot(mix_ref[:, A_WIDTH:], wout_ref[A_WIDTH:, :])

    lam = _lam(lamv_ref[...])
    for h in range(HEADS):
        on1 = acc_ref[2 * h, 0:HEAD_DIM, :] / acc_ref[2 * h, HEAD_DIM:HEAD_DIM + 1, :]
        on2 = acc_ref[2 * h + 1, 0:HEAD_DIM, :] / acc_ref[2 * h + 1, HEAD_DIM:HEAD_DIM + 1, :]
        o = (on1 - lam * on2).T
        o = _rms(o, subg_ref[...]) * (1.0 - LAMBDA_INIT)
        ga = z_ref[:, P_GA0 + h * HEAD_DIM:P_GA0 + (h + 1) * HEAD_DIM]
        mix_ref[:, h * HEAD_DIM:(h + 1) * HEAD_DIM] = (o * jax.nn.silu(ga)).astype(BF16)

    y0 = y0_ref[...] + _dot(mix_ref[:, 0:A_WIDTH], wout_ref[0:A_WIDTH, :])
    y0_ref[...] = y0

    xn1 = _rms(y0, g1_ref[...]).astype(BF16)
    starts_stream = i == 0
    nrow = TM // LN_ROWS
    n_glu = 2 * d // PROJ_COLS
    for j in range(3 * d // PROJ_COLS):
        cols = slice(j * PROJ_COLS, (j + 1) * PROJ_COLS)
        z_ref[:, cols] = _dot(xn1, win1_ref[:, cols])
        if j < nrow:
            rows = slice(j * LN_ROWS, (j + 1) * LN_ROWS)
            cn_ref[rows, :] = (jax.nn.silu(_ln(conv_ref[rows, :], clng_ref[...], clnb_ref[...]))
                               * gate_ref[rows, :]).astype(BF16)
        if j == nrow:
            y_ref[0] = _rms(y0p_ref[...] + _dot(cn_ref[...], wout1_ref[...]), fg_ref[...])
        if j < n_glu and j % 2 == 1:
            for c in (j - 1, j):
                a0 = (j - 1) * PROJ_COLS + (c % 2) * LANE
                tail = buf_ref[c, TM:TM + CONV_PAD, :]
                tail_ref[0, :, c * LANE:(c + 1) * LANE] = tail
                buf_ref[c, 0:CONV_PAD, :] = jnp.where(starts_stream, 0.0, tail)
                buf_ref[c, CONV_PAD:CONV_PAD + TM, :] = (z_ref[:, a0:a0 + LANE]
                                                         * jax.nn.sigmoid(z_ref[:, a0 + PROJ_COLS:a0 + PROJ_COLS + LANE]))
        if j >= n_glu:
            gcols = slice((j - n_glu) * PROJ_COLS, (j - n_glu + 1) * PROJ_COLS)
            gate_ref[:, gcols] = jax.nn.silu(z_ref[:, cols])
    y0p_ref[...] = y0


def _prompt(x, lamv, g0, win, bias, subg, lng, lnb, wm, bs, wout, g1, win1, wdw, bdw, clng, clnb, wout1, fg):
    bsz, t, d = x.shape
    nt = t // TM
    ntiles = bsz * nt
    const = lambda shape: pl.BlockSpec(shape, lambda s: (0,) * len(shape))
    cur = lambda s: jnp.minimum(s, ntiles - 1)
    done = lambda s: jnp.maximum(s - 1, 0)
    kv_row = pl.BlockSpec((1, TM, HEADS, HEAD_DIM), lambda s: (cur(s) // nt, cur(s) % nt, 0, 0))
    return pl.pallas_call(
        functools.partial(_prompt_kernel, nt=nt, ntiles=ntiles),
        grid=(ntiles + 1,),
        in_specs=[const((4, HALF)), pl.BlockSpec((1, TM, d), lambda s: (cur(s) // nt, cur(s) % nt, 0)),
                  const((1, d)), const((d, L0_IN)),
                  const((HEADS, 2, TM, TM)), const((1, HEAD_DIM)), const((1, B_WIDTH)), const((1, B_WIDTH)),
                  const((GROUPS, B_CHUNK, B_CHUNK)), const((B_CHUNK, B_WIDTH)), const((A_WIDTH + B_WIDTH, d)),
                  const((1, d)), const((d, 3 * d)), const((CONV_WIDTH, d)), const((1, d)),
                  const((1, d)), const((1, d)), const((d, d)), const((1, d))],
        out_specs=[pl.BlockSpec((1, TM, d), lambda s: (done(s) // nt, done(s) % nt, 0)), kv_row, kv_row,
                   pl.BlockSpec((1, CONV_PAD, d), lambda s: (done(s) // nt, 0, 0))],
        out_shape=[jax.ShapeDtypeStruct((bsz, t, d), F32),
                   jax.ShapeDtypeStruct((bsz, t, HEADS, HEAD_DIM), F32),
                   jax.ShapeDtypeStruct((bsz, t, HEADS, HEAD_DIM), F32),
                   jax.ShapeDtypeStruct((bsz, CONV_PAD, d), F32)],
        scratch_shapes=[pltpu.VMEM((TM, L0_IN), F32),
                        pltpu.VMEM((nt, TM, A_WIDTH), BF16),
                        pltpu.VMEM((nt, HEADS, VT_ROWS, TM), BF16),
                        pltpu.VMEM((2 * HEADS, TM, HEAD_DIM), BF16),
                        pltpu.VMEM((2 * HEADS, 1, TM), F32),
                        pltpu.VMEM((2 * HEADS, VT_ROWS, TM), F32),
                        pltpu.VMEM((2, 2 * HEADS, TM, TM), F32),
                        pltpu.VMEM((2, 2 * HEADS, TM, TM), BF16),
                        pltpu.VMEM((TM, A_WIDTH + B_WIDTH), BF16),
                        pltpu.VMEM((d // LANE, CONV_PAD + TM, LANE), F32),
                        pltpu.VMEM((TM, d), F32),
                        pltpu.VMEM((TM, d), F32),
                        pltpu.VMEM((TM, d), F32),
                        pltpu.VMEM((TM, d), F32),
                        pltpu.VMEM((TM, d), BF16)],
        compiler_params=pltpu.CompilerParams(dimension_semantics=("arbitrary",), vmem_limit_bytes=VMEM_LIMIT),
        name="prompt",
    )(lamv, x, g0, win, bias, subg, lng, lnb, wm, bs, wout, g1, win1, wdw, bdw, clng, clnb, wout1, fg)


def _s_proj_kernel(x_ref, g0_ref, win_ref, z_ref):
    z_ref[...] = _dot(_rms(x_ref[...], g0_ref[...]).astype(BF16), win_ref[...])


def _s_attn_kernel(lamv_ref, z_ref, ck_ref, cv_ref, bias_ref, o_ref, *, dec_seq, pad_rows):
    ncol = HEADS * 2 * dec_seq
    q = z_ref[:, P_Q0:P_Q0 + A_WIDTH] * (HALF ** -0.5)
    q_rep = jnp.concatenate([q] * (2 * HEADS), axis=0)
    row = lax.broadcasted_iota(jnp.int32, (ncol, A_WIDTH), 0)
    col = lax.broadcasted_iota(jnp.int32, (ncol, A_WIDTH), 1)
    q_bd = jnp.where(col // HALF == row // dec_seq, q_rep, 0.0).astype(BF16)

    zeros = jnp.zeros((pad_rows - dec_seq, A_WIDTH), F32)
    ck = ck_ref[0].reshape(ck_ref.shape[1], A_WIDTH)
    cv = cv_ref[0].reshape(cv_ref.shape[1], A_WIDTH)
    k_all = jnp.concatenate([ck, z_ref[:, P_K0:P_K0 + A_WIDTH], zeros], axis=0).astype(BF16)
    v_all = jnp.concatenate([cv, z_ref[:, P_V0:P_V0 + A_WIDTH], zeros], axis=0).astype(BF16)

    s = _dot_nt(k_all, q_bd) + bias_ref[...]
    p = jnp.exp(s - jnp.max(s, axis=0, keepdims=True))
    p = p / jnp.sum(p, axis=0, keepdims=True)
    w = p - _lam(lamv_ref[...]) * pltpu.roll(p, ncol - dec_seq, axis=1)
    o_all = _dot(w.T.astype(BF16), v_all)
    for h in range(HEADS):
        o_ref[:, h * HEAD_DIM:(h + 1) * HEAD_DIM] = o_all[2 * h * dec_seq:(2 * h + 1) * dec_seq,
                                                          h * HEAD_DIM:(h + 1) * HEAD_DIM]


def _s_rest_kernel(x_ref, z_ref, o_ref, subg_ref, lng_ref, lnb_ref, wbd_ref, bs_ref, wout0_ref,
                   g1_ref, win1_ref, st_ref, wdw_ref, bdw_ref, clng_ref, clnb_ref, wout1_ref, fg_ref,
                   y_ref, zv_ref, u_ref, mix_ref, buf_ref, conv_ref, *, dec_seq):
    d = x_ref.shape[-1]
    nb = st_ref.shape[0]
    for h in range(HEADS):
        sl = slice(h * HEAD_DIM, (h + 1) * HEAD_DIM)
        o = _rms(o_ref[:, sl], subg_ref[...]) * (1.0 - LAMBDA_INIT)
        mix_ref[:, sl] = (o * jax.nn.silu(z_ref[:, P_GA0 + h * HEAD_DIM:P_GA0 + (h + 1) * HEAD_DIM])).astype(BF16)
    for g in range(GROUPS):
        sl = slice(g * GROUP_DIM, (g + 1) * GROUP_DIM)
        ub, vb, gb = _gmlp_cols(z_ref, g)
        zv = _ln(jax.nn.gelu(vb), lng_ref[:, sl], lnb_ref[:, sl])
        zv_ref[:, sl] = zv
        sg = _dot(wbd_ref[g], zv.astype(BF16)) + bs_ref[:, sl]
        mix_ref[:, A_WIDTH + g * GROUP_DIM:A_WIDTH + (g + 1) * GROUP_DIM] = (
            jax.nn.gelu(ub) * sg * jax.nn.silu(gb)).astype(BF16)
    y0 = x_ref[...] + _dot(mix_ref[...], wout0_ref[...])

    z1 = _dot(_rms(y0, g1_ref[...]).astype(BF16), win1_ref[...])
    u = jnp.concatenate([z1[:, a0:a0 + PROJ_COLS] * jax.nn.sigmoid(z1[:, a0 + PROJ_COLS:a0 + 2 * PROJ_COLS])
                         for a0 in range(0, 2 * d, 2 * PROJ_COLS)], axis=1)
    u_ref[...] = u
    ngroups = d // LANE
    for b in range(nb):
        for c in range(ngroups):
            lanes = slice(c * LANE, (c + 1) * LANE)
            buf_ref[b * ngroups + c, 0:CONV_PAD, :] = st_ref[b, :, lanes]
            buf_ref[b * ngroups + c, CONV_PAD:CONV_PAD + dec_seq, :] = u[b * dec_seq:(b + 1) * dec_seq, lanes]
    for b in range(nb):
        for c in range(ngroups):
            conv_ref[b * dec_seq:(b + 1) * dec_seq, c * LANE:(c + 1) * LANE] = _conv_rows(
                buf_ref.at[pl.ds(b * ngroups, ngroups)], c, 0, dec_seq, wdw_ref, bdw_ref)
    c = jax.nn.silu(_ln(conv_ref[...], clng_ref[...], clnb_ref[...])) * jax.nn.silu(z1[:, 2 * d:3 * d])
    y1 = y0 + _dot(c.astype(BF16), wout1_ref[...])
    y_ref[...] = _rms(y1, fg_ref[...])


def _sample(xs, cache_k, cache_v, state, lamv, g0, win0, bias_s, subg, lng, lnb, wbd, bs_s, wout0,
            g1, win1, wdw, bdw, clng, clnb, wout1, fg):
    nb, dec_seq, d = xs.shape
    rows = nb * dec_seq
    past = cache_k.shape[1]
    pad_rows = bias_s.shape[0] - past
    x2 = xs.reshape(rows, d)
    z = pl.pallas_call(_s_proj_kernel, out_shape=jax.ShapeDtypeStruct((rows, L0_IN), F32),
                       compiler_params=pltpu.CompilerParams(vmem_limit_bytes=VMEM_LIMIT),
                       name="s_proj")(x2, g0, win0)

    ncol = HEADS * 2 * dec_seq
    o = pl.pallas_call(
        functools.partial(_s_attn_kernel, dec_seq=dec_seq, pad_rows=pad_rows),
        grid=(nb,),
        in_specs=[pl.BlockSpec((4, HALF), lambda b: (0, 0)),
                  pl.BlockSpec((dec_seq, L0_IN), lambda b: (b, 0)),
                  pl.BlockSpec((1, past, HEADS, HEAD_DIM), lambda b: (b, 0, 0, 0)),
                  pl.BlockSpec((1, past, HEADS, HEAD_DIM), lambda b: (b, 0, 0, 0)),
                  pl.BlockSpec((past + pad_rows, ncol), lambda b: (0, 0))],
        out_specs=pl.BlockSpec((dec_seq, A_WIDTH), lambda b: (b, 0)),
        out_shape=jax.ShapeDtypeStruct((rows, A_WIDTH), F32),
        compiler_params=pltpu.CompilerParams(dimension_semantics=("arbitrary",), vmem_limit_bytes=VMEM_LIMIT),
        name="s_attn",
    )(lamv, z, cache_k, cache_v, bias_s)

    state_pad = jnp.pad(state, ((0, 0), (CONV_OFF, 0), (0, 0)))
    y, zv, u = pl.pallas_call(
        functools.partial(_s_rest_kernel, dec_seq=dec_seq),
        out_shape=[jax.ShapeDtypeStruct((rows, d), F32),
                   jax.ShapeDtypeStruct((rows, B_WIDTH), F32),
                   jax.ShapeDtypeStruct((rows, d), F32)],
        scratch_shapes=[pltpu.VMEM((rows, A_WIDTH + B_WIDTH), BF16),
                        pltpu.VMEM((nb * (d // LANE), CONV_PAD + dec_seq, LANE), F32),
                        pltpu.VMEM((rows, d), F32)],
        compiler_params=pltpu.CompilerParams(vmem_limit_bytes=VMEM_LIMIT),
        name="s_rest",
    )(x2, z, o, subg, lng, lnb, wbd, bs_s, wout0, g1, win1, state_pad, wdw, bdw, clng, clnb, wout1, fg)
    return z, y, zv, u


def kernel(x_prompt, x_sample, cache_k0, cache_v0, state_conv1, rel_bias, norm_g0, w_in0, lambda_q1, lambda_k1,
           lambda_q2, lambda_k2, subln_g0, gv_ln_g0, gv_ln_b0, w_s0, b_s0, w_out0, norm_g1, w_in1, w_dw1, b_dw1,
           conv_ln_g1, conv_ln_b1, w_out1, final_g):
    bsz, t, d = x_prompt.shape
    nb, dec_seq, _ = x_sample.shape
    past = cache_k0.shape[1]
    assert t % TM == 0 and TM % B_CHUNK == 0 and TM % CHUNK == 0 and TM >= MAX_DISTANCE
    assert d % (2 * PROJ_COLS) == 0 and TM // LN_ROWS < 3 * d // PROJ_COLS
    assert dec_seq <= B_CHUNK and dec_seq % 8 == 0 and HEADS * 2 * dec_seq == LANE
    assert past % CHUNK == 0 and (past + dec_seq - 1) // CHUNK == past // CHUNK

    row = lambda a: a.reshape(1, -1).astype(F32)
    lamv = jnp.stack([lambda_q1, lambda_k1, lambda_q2, lambda_k2]).astype(F32)
    wout0, wout1 = w_out0.astype(BF16), w_out1.astype(BF16)
    cols0 = [w_in0[:, o + g * GROUP_DIM:o + (g + 1) * GROUP_DIM] for g in range(GROUPS) for o in (UB0, VB0, GB0)]
    win0 = jnp.concatenate(cols0 + [w_in0[:, :UB0]], axis=1).astype(BF16)
    cols1 = [w_in1[:, o + p * PROJ_COLS:o + (p + 1) * PROJ_COLS] for p in range(d // PROJ_COLS) for o in (0, d)]
    win1 = jnp.concatenate(cols1 + [w_in1[:, 2 * d:]], axis=1).astype(BF16)
    g0, g1, fg, subg = row(norm_g0), row(norm_g1), row(final_g), row(subln_g0)
    lng, lnb = row(gv_ln_g0), row(gv_ln_b0)
    bdw, clng, clnb = row(b_dw1), row(conv_ln_g1), row(conv_ln_b1)
    wdw = w_dw1.astype(F32)

    rbs = (rel_bias - rel_bias[FAR_BUCKET:FAR_BUCKET + 1]).astype(F32)
    assert np.all(_np_bucket(-np.arange(TM + 1, 2 * max(t, past + dec_seq))) == FAR_BUCKET)
    kk = np.arange(TM)[:, None]
    qq = np.arange(TM)[None, :]
    own = jnp.where((kk // CHUNK <= qq // CHUNK)[..., None], _lookup(rbs, _np_bucket(kk - qq)), NEG)
    prev = _lookup(rbs, _np_bucket(kk - qq - TM))
    bias_p = jnp.transpose(jnp.stack([own, prev]), (3, 0, 1, 2)) * LOG2E

    tril = np.tril(np.ones((B_CHUNK, B_CHUNK), dtype=bool))
    wm = jnp.where(tril[None], w_s0, 0)
    bs = jnp.repeat(b_s0.T.astype(F32), GROUP_DIM, axis=1)

    y_prompt, k0p, v0p, tail_p = _prompt(x_prompt, lamv, g0, win0, bias_p, subg, lng, lnb, wm.astype(BF16), bs, wout0,
                                         g1, win1, wdw, bdw, clng, clnb, wout1, fg)

    pad_rows = LANE
    kpos = np.arange(past + dec_seq)[:, None]
    qpos = past + np.arange(dec_seq)[None, :]
    bsmp = jnp.transpose(_lookup(rbs, _np_bucket(kpos - qpos)), (0, 2, 1))
    bsmp = jnp.broadcast_to(bsmp[:, :, None, :], (past + dec_seq, HEADS, 2, dec_seq)).reshape(past + dec_seq, -1)
    bias_s = jnp.pad(bsmp, ((0, pad_rows - dec_seq), (0, 0)), constant_values=NEG)
    wbd = jnp.stack([jnp.kron(jnp.eye(nb, dtype=F32), wm[g, :dec_seq, :dec_seq]) for g in range(GROUPS)])
    bs_s = jnp.tile(bs[:dec_seq], (nb, 1))

    zs, ys, zvs, us = _sample(x_sample, cache_k0, cache_v0, state_conv1, lamv, g0, win0, bias_s, subg, lng, lnb,
                              wbd.astype(BF16), bs_s, wout0, g1, win1, wdw, bdw, clng, clnb, wout1, fg)

    keep = CONV_WIDTH - 1
    return (y_prompt,
            ys.reshape(nb, dec_seq, d),
            k0p,
            v0p,
            tail_p[:, CONV_OFF:],
            zs[:, P_K0:P_K0 + A_WIDTH].reshape(nb, dec_seq, HEADS, HEAD_DIM),
            zs[:, P_V0:P_V0 + A_WIDTH].reshape(nb, dec_seq, HEADS, HEAD_DIM),
            zvs.reshape(nb, dec_seq, B_WIDTH),
            jnp.concatenate([state_conv1, us.reshape(nb, dec_seq, d)], axis=1)[:, -keep:])
```

```python
import functools
import math

import numpy as np
import jax
import jax.numpy as jnp
from jax import lax
from jax.experimental import pallas as pl
from jax.experimental.pallas import tpu as pltpu

F32 = jnp.float32
BF16 = jnp.bfloat16

CHUNK = 64
HEADS = 4
HALF = 64
HEAD_DIM = 2 * HALF
A_WIDTH = HEADS * HEAD_DIM
LAMBDA_INIT = 0.2
N_BUCKETS = 32
MAX_DISTANCE = 128
GROUPS = 4
GROUP_DIM = 128
B_WIDTH = GROUPS * GROUP_DIM
B_CHUNK = 128
CONV_WIDTH = 31
RMS_EPS = 1e-6
LN_EPS = 1e-5

LANE = 128
MXU_COLS = 256
VMEM_LIMIT = 56 * 1024 * 1024

UB0, VB0, GB0 = 4 * 512, 5 * 512, 6 * 512
L0_IN = 7 * 512
P_GMLP0 = 0
P_Q0, P_K0, P_V0, P_GA0 = (3 * B_WIDTH + i * 512 for i in range(4))

TM = 256
CONV_PAD = 32
CONV_OFF = CONV_PAD - (CONV_WIDTH - 1)
CONV_RB = 64
LN_ROWS = 64
PROJ_COLS = MXU_COLS
VT_ROWS = HEAD_DIM + 16
LOG2E = 1.4426950408889634
Q_SCALE_LOG2 = HALF ** -0.5 * LOG2E
NEG = -1e30


def _np_bucket(rel):
    nb = N_BUCKETS // 2
    ret = np.where(rel > 0, nb, 0)
    n = np.abs(rel)
    max_exact = nb // 2
    nf = np.maximum(n, 1).astype(np.float32)
    large = max_exact + (np.log(nf / np.float32(max_exact)) / np.float32(math.log(MAX_DISTANCE / max_exact))
                         * np.float32(nb - max_exact)).astype(np.int32)
    large = np.minimum(large, nb - 1)
    return (ret + np.where(n < max_exact, n, large)).astype(np.int32)


FAR_BUCKET = N_BUCKETS // 2 - 1


def _lookup(table, idx):
    idx_c = jnp.asarray(idx)[..., None]
    out = jnp.zeros(idx.shape + (table.shape[1],), F32)
    for b in np.unique(idx):
        out = jnp.where(idx_c == int(b), table[int(b)], out)
    return out


def _rms(x, g):
    return x * lax.rsqrt(jnp.mean(x * x, axis=-1, keepdims=True) + RMS_EPS) * g


def _ln(x, g, b):
    mu = jnp.mean(x, axis=-1, keepdims=True)
    xc = x - mu
    var = jnp.mean(xc * xc, axis=-1, keepdims=True)
    return xc * lax.rsqrt(var + LN_EPS) * g + b


def _lam(lamv):
    s1 = jnp.sum(lamv[0:1] * lamv[1:2], axis=-1, keepdims=True)
    s2 = jnp.sum(lamv[2:3] * lamv[3:4], axis=-1, keepdims=True)
    return jnp.exp(s1) - jnp.exp(s2) + LAMBDA_INIT


def _gmlp_cols(z_ref, g):
    base = P_GMLP0 + 3 * g * GROUP_DIM
    return tuple(z_ref[:, base + j * GROUP_DIM:base + (j + 1) * GROUP_DIM] for j in range(3))


def _dot(a, b):
    return jnp.dot(a, b, preferred_element_type=F32)


def _dot_nt(a, b):
    return lax.dot_general(a, b, (((1,), (1,)), ((), ())), preferred_element_type=F32)


def _conv_rows(buf_ref, c, base, rows, wdw_ref, bdw_ref):
    lanes = slice(c * LANE, (c + 1) * LANE)
    acc = jnp.broadcast_to(bdw_ref[:, lanes], (rows, LANE))
    for tap in range(CONV_WIDTH):
        lo = base + tap + CONV_OFF
        acc = acc + wdw_ref[tap:tap + 1, lanes] * buf_ref[c, lo:lo + rows, :]
    return acc


def _prompt_kernel(lamv_ref, x_ref, g0_ref, win_ref, bias_ref, subg_ref, lng_ref, lnb_ref, wm_ref, bs_ref,
                   wout_ref, g1_ref, win1_ref, wdw_ref, bdw_ref, clng_ref, clnb_ref, wout1_ref, fg_ref,
                   y_ref, k_ref, v_ref, tail_ref,
                   z_ref, kbf_ref, vt_ref, qm_ref, m_ref, acc_ref, s_ref, mix_ref,
                   buf_ref, conv_ref, gate_ref, y0_ref, y0p_ref, cn_ref, *, nt, ntiles):
    step = pl.program_id(0)
    i = jnp.minimum(step, ntiles - 1) % nt
    d = x_ref.shape[-1]
    ngroups = d // LANE

    @pl.when(step == 0)
    def _():
        buf_ref[...] = jnp.zeros(buf_ref.shape, F32)
        gate_ref[...] = jnp.zeros(gate_ref.shape, F32)
        y0p_ref[...] = jnp.zeros(y0p_ref.shape, F32)

    x = x_ref[0]
    xn = _rms(x, g0_ref[...]).astype(BF16)
    blocks = [(c, r) for c in range(ngroups) for r in range(TM // CONV_RB)]
    nchunks = L0_IN // PROJ_COLS
    for j in range(nchunks):
        cols = slice(j * PROJ_COLS, (j + 1) * PROJ_COLS)
        z_ref[:, cols] = _dot(xn, win_ref[:, cols])
        for c, r in blocks[j * len(blocks) // nchunks:(j + 1) * len(blocks) // nchunks]:
            conv_ref[r * CONV_RB:(r + 1) * CONV_RB, c * LANE:(c + 1) * LANE] = _conv_rows(
                buf_ref, c, r * CONV_RB, CONV_RB, wdw_ref, bdw_ref)

    for g in range(GROUPS):
        sl = slice(g * GROUP_DIM, (g + 1) * GROUP_DIM)
        ub, vb, gb = _gmlp_cols(z_ref, g)
        zv = _ln(jax.nn.gelu(vb), lng_ref[:, sl], lnb_ref[:, sl]).astype(BF16)
        zu = jax.nn.gelu(ub)
        gate = jax.nn.silu(gb)
        for c in range(TM // B_CHUNK):
            rows = slice(c * B_CHUNK, (c + 1) * B_CHUNK)
            sg = _dot(wm_ref[g], zv[rows]) + bs_ref[:, sl]
            mix_ref[rows, A_WIDTH + g * GROUP_DIM:A_WIDTH + (g + 1) * GROUP_DIM] = (
                zu[rows] * sg * gate[rows]).astype(BF16)

    k = z_ref[:, P_K0:P_K0 + A_WIDTH]
    v = z_ref[:, P_V0:P_V0 + A_WIDTH]
    k_ref[0] = k.reshape(TM, HEADS, HEAD_DIM)
    v_ref[0] = v.reshape(TM, HEADS, HEAD_DIM)
    kbf_ref[i] = k.astype(BF16)
    for h in range(HEADS):
        vt_ref[i, h, 0:HEAD_DIM, :] = v[:, h * HEAD_DIM:(h + 1) * HEAD_DIM].T.astype(BF16)
        vt_ref[i, h, HEAD_DIM:VT_ROWS, :] = jnp.ones((VT_ROWS - HEAD_DIM, TM), BF16)

    lo = lax.broadcasted_iota(jnp.int32, (TM, HEAD_DIM), 1) < HALF
    for h in range(HEADS):
        q = z_ref[:, P_Q0 + h * HEAD_DIM:P_Q0 + (h + 1) * HEAD_DIM] * Q_SCALE_LOG2
        qm_ref[2 * h] = jnp.where(lo, q, 0.0).astype(BF16)
        qm_ref[2 * h + 1] = jnp.where(lo, 0.0, q).astype(BF16)

    units = range(2 * HEADS)

    def attend(tiles, first):
        m_new, alpha = {}, {}
        for u in units:
            h = u // 2
            mt = None
            for t, (j, bias_idx) in enumerate(tiles):
                s = _dot_nt(kbf_ref[j, :, h * HEAD_DIM:(h + 1) * HEAD_DIM], qm_ref[u])
                if bias_idx is not None:
                    s = s + bias_ref[h, bias_idx]
                s_ref[t, u] = s
                smax = jnp.max(s, axis=0, keepdims=True)
                mt = smax if mt is None else jnp.maximum(mt, smax)
            if first:
                m_new[u] = mt
            else:
                m_old = m_ref[u]
                m_new[u] = jnp.maximum(m_old, mt)
                alpha[u] = jnp.exp2(m_old - m_new[u])
            m_ref[u] = m_new[u]
        for u in units:
            pv = None
            for t in range(len(tiles)):
                p = jnp.exp2(s_ref[t, u] - m_new[u]).astype(BF16)
                part = _dot(vt_ref[tiles[t][0], u // 2], p)
                pv = part if pv is None else pv + part
            acc_ref[u] = pv if first else alpha[u] * acc_ref[u] + pv

    @pl.when(i == 0)
    def _():
        attend([(i, 0)], True)

    @pl.when(i % 2 == 1)
    def _():
        attend([(i, 0), (i - 1, 1)], True)

    @pl.when(jnp.logical_and(i > 0, i % 2 == 0))
    def _():
        attend([(i, 0), (i - 1, 1), (i - 2, None)], True)

    def far_pair(jj, carry):
        attend([(2 * jj, None), (2 * jj + 1, None)], False)
        return carry

    lax.fori_loop(0, (i - 1) // 2, far_pair, 0)

    y0_ref[...] = x + _dot(mix_ref[:, A_WIDTH:], wout_ref[A_WIDTH:, :])

    lam = _lam(lamv_ref[...])
    for h in range(HEADS):
        on1 = acc_ref[2 * h, 0:HEAD_DIM, :] / acc_ref[2 * h, HEAD_DIM:HEAD_DIM + 1, :]
        on2 = acc_ref[2 * h + 1, 0:HEAD_DIM, :] / acc_ref[2 * h + 1, HEAD_DIM:HEAD_DIM + 1, :]
        o = (on1 - lam * on2).T
        o = _rms(o, subg_ref[...]) * (1.0 - LAMBDA_INIT)
        ga = z_ref[:, P_GA0 + h * HEAD_DIM:P_GA0 + (h + 1) * HEAD_DIM]
        mix_ref[:, h * HEAD_DIM:(h + 1) * HEAD_DIM] = (o * jax.nn.silu(ga)).astype(BF16)

    y0 = y0_ref[...] + _dot(mix_ref[:, 0:A_WIDTH], wout_ref[0:A_WIDTH, :])
    y0_ref[...] = y0

    xn1 = _rms(y0, g1_ref[...]).astype(BF16)
    starts_stream = i == 0
    nrow = TM // LN_ROWS
    n_glu = 2 * d // PROJ_COLS
    for j in range(3 * d // PROJ_COLS):
        cols = slice(j * PROJ_COLS, (j + 1) * PROJ_COLS)
        z_ref[:, cols] = _dot(xn1, win1_ref[:, cols])
        if j < nrow:
            rows = slice(j * LN_ROWS, (j + 1) * LN_ROWS)
            cn_ref[rows, :] = (jax.nn.silu(_ln(conv_ref[rows, :], clng_ref[...], clnb_ref[...]))
                               * gate_ref[rows, :]).astype(BF16)
        if j == nrow:
            y_ref[0] = _rms(y0p_ref[...] + _dot(cn_ref[...], wout1_ref[...]), fg_ref[...])
        if j < n_glu and j % 2 == 1:
            for c in (j - 1, j):
                a0 = (j - 1) * PROJ_COLS + (c % 2) * LANE
                tail = buf_ref[c, TM:TM + CONV_PAD, :]
                tail_ref[0, :, c * LANE:(c + 1) * LANE] = tail
                buf_ref[c, 0:CONV_PAD, :] = jnp.where(starts_stream, 0.0, tail)
                buf_ref[c, CONV_PAD:CONV_PAD + TM, :] = (z_ref[:, a0:a0 + LANE]
                                                         * jax.nn.sigmoid(z_ref[:, a0 + PROJ_COLS:a0 + PROJ_COLS + LANE]))
        if j >= n_glu:
            gcols = slice((j - n_glu) * PROJ_COLS, (j - n_glu + 1) * PROJ_COLS)
            gate_ref[:, gcols] = jax.nn.silu(z_ref[:, cols])
    y0p_ref[...] = y0


def _prompt(x, lamv, g0, win, bias, subg, lng, lnb, wm, bs, wout, g1, win1, wdw, bdw, clng, clnb, wout1, fg):
    bsz, t, d = x.shape
    nt = t // TM
    ntiles = bsz * nt
    const = lambda shape: pl.BlockSpec(shape, lambda s: (0,) * len(shape))
    cur = lambda s: jnp.minimum(s, ntiles - 1)
    done = lambda s: jnp.maximum(s - 1, 0)
    kv_row = pl.BlockSpec((1, TM, HEADS, HEAD_DIM), lambda s: (cur(s) // nt, cur(s) % nt, 0, 0))
    return pl.pallas_call(
        functools.partial(_prompt_kernel, nt=nt, ntiles=ntiles),
        grid=(ntiles + 1,),
        in_specs=[const((4, HALF)), pl.BlockSpec((1, TM, d), lambda s: (cur(s) // nt, cur(s) % nt, 0)),
                  const((1, d)), const((d, L0_IN)),
                  const((HEADS, 2, TM, TM)), const((1, HEAD_DIM)), const((1, B_WIDTH)), const((1, B_WIDTH)),
                  const((GROUPS, B_CHUNK, B_CHUNK)), const((B_CHUNK, B_WIDTH)), const((A_WIDTH + B_WIDTH, d)),
                  const((1, d)), const((d, 3 * d)), const((CONV_WIDTH, d)), const((1, d)),
                  const((1, d)), const((1, d)), const((d, d)), const((1, d))],
        out_specs=[pl.BlockSpec((1, TM, d), lambda s: (done(s) // nt, done(s) % nt, 0)), kv_row, kv_row,
                   pl.BlockSpec((1, CONV_PAD, d), lambda s: (done(s) // nt, 0, 0))],
        out_shape=[jax.ShapeDtypeStruct((bsz, t, d), F32),
                   jax.ShapeDtypeStruct((bsz, t, HEADS, HEAD_DIM), F32),
                   jax.ShapeDtypeStruct((bsz, t, HEADS, HEAD_DIM), F32),
                   jax.ShapeDtypeStruct((bsz, CONV_PAD, d), F32)],
        scratch_shapes=[pltpu.VMEM((TM, L0_IN), F32),
                        pltpu.VMEM((nt, TM, A_WIDTH), BF16),
                        pltpu.VMEM((nt, HEADS, VT_ROWS, TM), BF16),
                        pltpu.VMEM((2 * HEADS, TM, HEAD_DIM), BF16),
                        pltpu.VMEM((2 * HEADS, 1, TM), F32),
                        pltpu.VMEM((2 * HEADS, VT_ROWS, TM), F32),
                        pltpu.VMEM((3, 2 * HEADS, TM, TM), F32),
                        pltpu.VMEM((TM, A_WIDTH + B_WIDTH), BF16),
                        pltpu.VMEM((d // LANE, CONV_PAD + TM, LANE), F32),
                        pltpu.VMEM((TM, d), F32),
                        pltpu.VMEM((TM, d), F32),
                        pltpu.VMEM((TM, d), F32),
                        pltpu.VMEM((TM, d), F32),
                        pltpu.VMEM((TM, d), BF16)],
        compiler_params=pltpu.CompilerParams(dimension_semantics=("arbitrary",), vmem_limit_bytes=VMEM_LIMIT),
        name="prompt",
    )(lamv, x, g0, win, bias, subg, lng, lnb, wm, bs, wout, g1, win1, wdw, bdw, clng, clnb, wout1, fg)


def _s_proj_kernel(x_ref, g0_ref, win_ref, z_ref):
    z_ref[...] = _dot(_rms(x_ref[...], g0_ref[...]).astype(BF16), win_ref[...])


def _s_attn_kernel(lamv_ref, z_ref, ck_ref, cv_ref, bias_ref, o_ref, *, dec_seq, pad_rows):
    ncol = HEADS * 2 * dec_seq
    q = z_ref[:, P_Q0:P_Q0 + A_WIDTH] * (HALF ** -0.5)
    q_rep = jnp.concatenate([q] * (2 * HEADS), axis=0)
    row = lax.broadcasted_iota(jnp.int32, (ncol, A_WIDTH), 0)
    col = lax.broadcasted_iota(jnp.int32, (ncol, A_WIDTH), 1)
    q_bd = jnp.where(col // HALF == row // dec_seq, q_rep, 0.0).astype(BF16)

    zeros = jnp.zeros((pad_rows - dec_seq, A_WIDTH), F32)
    ck = ck_ref[0].reshape(ck_ref.shape[1], A_WIDTH)
    cv = cv_ref[0].reshape(cv_ref.shape[1], A_WIDTH)
    k_all = jnp.concatenate([ck, z_ref[:, P_K0:P_K0 + A_WIDTH], zeros], axis=0).astype(BF16)
    v_all = jnp.concatenate([cv, z_ref[:, P_V0:P_V0 + A_WIDTH], zeros], axis=0).astype(BF16)

    s = _dot_nt(k_all, q_bd) + bias_ref[...]
    p = jnp.exp(s - jnp.max(s, axis=0, keepdims=True))
    p = p / jnp.sum(p, axis=0, keepdims=True)
    w = p - _lam(lamv_ref[...]) * pltpu.roll(p, ncol - dec_seq, axis=1)
    o_all = _dot(w.T.astype(BF16), v_all)
    for h in range(HEADS):
        o_ref[:, h * HEAD_DIM:(h + 1) * HEAD_DIM] = o_all[2 * h * dec_seq:(2 * h + 1) * dec_seq,
                                                          h * HEAD_DIM:(h + 1) * HEAD_DIM]


def _s_rest_kernel(x_ref, z_ref, o_ref, subg_ref, lng_ref, lnb_ref, wbd_ref, bs_ref, wout0_ref,
                   g1_ref, win1_ref, st_ref, wdw_ref, bdw_ref, clng_ref, clnb_ref, wout1_ref, fg_ref,
                   y_ref, zv_ref, u_ref, mix_ref, buf_ref, conv_ref, *, dec_seq):
    d = x_ref.shape[-1]
    nb = st_ref.shape[0]
    for h in range(HEADS):
        sl = slice(h * HEAD_DIM, (h + 1) * HEAD_DIM)
        o = _rms(o_ref[:, sl], subg_ref[...]) * (1.0 - LAMBDA_INIT)
        mix_ref[:, sl] = (o * jax.nn.silu(z_ref[:, P_GA0 + h * HEAD_DIM:P_GA0 + (h + 1) * HEAD_DIM])).astype(BF16)
    for g in range(GROUPS):
        sl = slice(g * GROUP_DIM, (g + 1) * GROUP_DIM)
        ub, vb, gb = _gmlp_cols(z_ref, g)
        zv = _ln(jax.nn.gelu(vb), lng_ref[:, sl], lnb_ref[:, sl])
        zv_ref[:, sl] = zv
        sg = _dot(wbd_ref[g], zv.astype(BF16)) + bs_ref[:, sl]
        mix_ref[:, A_WIDTH + g * GROUP_DIM:A_WIDTH + (g + 1) * GROUP_DIM] = (
            jax.nn.gelu(ub) * sg * jax.nn.silu(gb)).astype(BF16)
    y0 = x_ref[...] + _dot(mix_ref[...], wout0_ref[...])

    z1 = _dot(_rms(y0, g1_ref[...]).astype(BF16), win1_ref[...])
    u = jnp.concatenate([z1[:, a0:a0 + PROJ_COLS] * jax.nn.sigmoid(z1[:, a0 + PROJ_COLS:a0 + 2 * PROJ_COLS])
                         for a0 in range(0, 2 * d, 2 * PROJ_COLS)], axis=1)
    u_ref[...] = u
    ngroups = d // LANE
    for b in range(nb):
        for c in range(ngroups):
            lanes = slice(c * LANE, (c + 1) * LANE)
            buf_ref[b * ngroups + c, 0:CONV_PAD, :] = st_ref[b, :, lanes]
            buf_ref[b * ngroups + c, CONV_PAD:CONV_PAD + dec_seq, :] = u[b * dec_seq:(b + 1) * dec_seq, lanes]
    for b in range(nb):
        for c in range(ngroups):
            conv_ref[b * dec_seq:(b + 1) * dec_seq, c * LANE:(c + 1) * LANE] = _conv_rows(
                buf_ref.at[pl.ds(b * ngroups, ngroups)], c, 0, dec_seq, wdw_ref, bdw_ref)
    c = jax.nn.silu(_ln(conv_ref[...], clng_ref[...], clnb_ref[...])) * jax.nn.silu(z1[:, 2 * d:3 * d])
    y1 = y0 + _dot(c.astype(BF16), wout1_ref[...])
    y_ref[...] = _rms(y1, fg_ref[...])


def _sample(xs, cache_k, cache_v, state, lamv, g0, win0, bias_s, subg, lng, lnb, wbd, bs_s, wout0,
            g1, win1, wdw, bdw, clng, clnb, wout1, fg):
    nb, dec_seq, d = xs.shape
    rows = nb * dec_seq
    past = cache_k.shape[1]
    pad_rows = bias_s.shape[0] - past
    x2 = xs.reshape(rows, d)
    z = pl.pallas_call(_s_proj_kernel, out_shape=jax.ShapeDtypeStruct((rows, L0_IN), F32),
                       compiler_params=pltpu.CompilerParams(vmem_limit_bytes=VMEM_LIMIT),
                       name="s_proj")(x2, g0, win0)

    ncol = HEADS * 2 * dec_seq
    o = pl.pallas_call(
        functools.partial(_s_attn_kernel, dec_seq=dec_seq, pad_rows=pad_rows),
        grid=(nb,),
        in_specs=[pl.BlockSpec((4, HALF), lambda b: (0, 0)),
                  pl.BlockSpec((dec_seq, L0_IN), lambda b: (b, 0)),
                  pl.BlockSpec((1, past, HEADS, HEAD_DIM), lambda b: (b, 0, 0, 0)),
                  pl.BlockSpec((1, past, HEADS, HEAD_DIM), lambda b: (b, 0, 0, 0)),
                  pl.BlockSpec((past + pad_rows, ncol), lambda b: (0, 0))],
        out_specs=pl.BlockSpec((dec_seq, A_WIDTH), lambda b: (b, 0)),
        out_shape=jax.ShapeDtypeStruct((rows, A_WIDTH), F32),
        compiler_params=pltpu.CompilerParams(dimension_semantics=("arbitrary",), vmem_limit_bytes=VMEM_LIMIT),
        name="s_attn",
    )(lamv, z, cache_k, cache_v, bias_s)

    state_pad = jnp.pad(state, ((0, 0), (CONV_OFF, 0), (0, 0)))
    y, zv, u = pl.pallas_call(
        functools.partial(_s_rest_kernel, dec_seq=dec_seq),
        out_shape=[jax.ShapeDtypeStruct((rows, d), F32),
                   jax.ShapeDtypeStruct((rows, B_WIDTH), F32),
                   jax.ShapeDtypeStruct((rows, d), F32)],
        scratch_shapes=[pltpu.VMEM((rows, A_WIDTH + B_WIDTH), BF16),
                        pltpu.VMEM((nb * (d // LANE), CONV_PAD + dec_seq, LANE), F32),
                        pltpu.VMEM((rows, d), F32)],
        compiler_params=pltpu.CompilerParams(vmem_limit_bytes=VMEM_LIMIT),
        name="s_rest",
    )(x2, z, o, subg, lng, lnb, wbd, bs_s, wout0, g1, win1, state_pad, wdw, bdw, clng, clnb, wout1, fg)
    return z, y, zv, u


def kernel(x_prompt, x_sample, cache_k0, cache_v0, state_conv1, rel_bias, norm_g0, w_in0, lambda_q1, lambda_k1,
           lambda_q2, lambda_k2, subln_g0, gv_ln_g0, gv_ln_b0, w_s0, b_s0, w_out0, norm_g1, w_in1, w_dw1, b_dw1,
           conv_ln_g1, conv_ln_b1, w_out1, final_g):
    bsz, t, d = x_prompt.shape
    nb, dec_seq, _ = x_sample.shape
    past = cache_k0.shape[1]
    assert t % TM == 0 and TM % B_CHUNK == 0 and TM % CHUNK == 0 and TM >= MAX_DISTANCE
    assert d % (2 * PROJ_COLS) == 0 and TM // LN_ROWS < 3 * d // PROJ_COLS
    assert dec_seq <= B_CHUNK and dec_seq % 8 == 0 and HEADS * 2 * dec_seq == LANE
    assert past % CHUNK == 0 and (past + dec_seq - 1) // CHUNK == past // CHUNK

    row = lambda a: a.reshape(1, -1).astype(F32)
    lamv = jnp.stack([lambda_q1, lambda_k1, lambda_q2, lambda_k2]).astype(F32)
    wout0, wout1 = w_out0.astype(BF16), w_out1.astype(BF16)
    cols0 = [w_in0[:, o + g * GROUP_DIM:o + (g + 1) * GROUP_DIM] for g in range(GROUPS) for o in (UB0, VB0, GB0)]
    win0 = jnp.concatenate(cols0 + [w_in0[:, :UB0]], axis=1).astype(BF16)
    cols1 = [w_in1[:, o + p * PROJ_COLS:o + (p + 1) * PROJ_COLS] for p in range(d // PROJ_COLS) for o in (0, d)]
    win1 = jnp.concatenate(cols1 + [w_in1[:, 2 * d:]], axis=1).astype(BF16)
    g0, g1, fg, subg = row(norm_g0), row(norm_g1), row(final_g), row(subln_g0)
    lng, lnb = row(gv_ln_g0), row(gv_ln_b0)
    bdw, clng, clnb = row(b_dw1), row(conv_ln_g1), row(conv_ln_b1)
    wdw = w_dw1.astype(F32)

    rbs = (rel_bias - rel_bias[FAR_BUCKET:FAR_BUCKET + 1]).astype(F32)
    assert np.all(_np_bucket(-np.arange(TM + 1, 2 * max(t, past + dec_seq))) == FAR_BUCKET)
    kk = np.arange(TM)[:, None]
    qq = np.arange(TM)[None, :]
    own = jnp.where((kk // CHUNK <= qq // CHUNK)[..., None], _lookup(rbs, _np_bucket(kk - qq)), NEG)
    prev = _lookup(rbs, _np_bucket(kk - qq - TM))
    bias_p = jnp.transpose(jnp.stack([own, prev]), (3, 0, 1, 2)) * LOG2E

    tril = np.tril(np.ones((B_CHUNK, B_CHUNK), dtype=bool))
    wm = jnp.where(tril[None], w_s0, 0)
    bs = jnp.repeat(b_s0.T.astype(F32), GROUP_DIM, axis=1)

    y_prompt, k0p, v0p, tail_p = _prompt(x_prompt, lamv, g0, win0, bias_p, subg, lng, lnb, wm.astype(BF16), bs, wout0,
                                         g1, win1, wdw, bdw, clng, clnb, wout1, fg)

    pad_rows = LANE
    kpos = np.arange(past + dec_seq)[:, None]
    qpos = past + np.arange(dec_seq)[None, :]
    bsmp = jnp.transpose(_lookup(rbs, _np_bucket(kpos - qpos)), (0, 2, 1))
    bsmp = jnp.broadcast_to(bsmp[:, :, None, :], (past + dec_seq, HEADS, 2, dec_seq)).reshape(past + dec_seq, -1)
    bias_s = jnp.pad(bsmp, ((0, pad_rows - dec_seq), (0, 0)), constant_values=NEG)
    wbd = jnp.stack([jnp.kron(jnp.eye(nb, dtype=F32), wm[g, :dec_seq, :dec_seq]) for g in range(GROUPS)])
    bs_s = jnp.tile(bs[:dec_seq], (nb, 1))

    zs, ys, zvs, us = _sample(x_sample, cache_k0, cache_v0, state_conv1, lamv, g0, win0, bias_s, subg, lng, lnb,
                              wbd.astype(BF16), bs_s, wout0, g1, win1, wdw, bdw, clng, clnb, wout1, fg)

    keep = CONV_WIDTH - 1
    return (y_prompt,
            ys.reshape(nb, dec_seq, d),
            k0p,
            v0p,
            tail_p[:, CONV_OFF:],
            zs[:, P_K0:P_K0 + A_WIDTH].reshape(nb, dec_seq, HEADS, HEAD_DIM),
            zs[:, P_V0:P_V0 + A_WIDTH].reshape(nb, dec_seq, HEADS, HEAD_DIM),
            zvs.reshape(nb, dec_seq, B_WIDTH),
            jnp.concatenate([state_conv1, us.reshape(nb, dec_seq, d)], axis=1)[:, -keep:])
```

```python
import functools
import math

import numpy as np
import jax
import jax.numpy as jnp
from jax import lax
from jax.experimental import pallas as pl
from jax.experimental.pallas import tpu as pltpu

F32 = jnp.float32
BF16 = jnp.bfloat16

CHUNK = 64
HEADS = 4
HALF = 64
HEAD_DIM = 2 * HALF
A_WIDTH = HEADS * HEAD_DIM
LAMBDA_INIT = 0.2
N_BUCKETS = 32
MAX_DISTANCE = 128
GROUPS = 4
GROUP_DIM = 128
B_WIDTH = GROUPS * GROUP_DIM
B_CHUNK = 128
CONV_WIDTH = 31
RMS_EPS = 1e-6
LN_EPS = 1e-5

LANE = 128
MXU_COLS = 256
VMEM_LIMIT = 56 * 1024 * 1024

UB0, VB0, GB0 = 4 * 512, 5 * 512, 6 * 512
L0_IN = 7 * 512
P_GMLP0 = 0
P_Q0, P_K0, P_V0, P_GA0 = (3 * B_WIDTH + i * 512 for i in range(4))

TM = 256
CONV_PAD = 32
CONV_OFF = CONV_PAD - (CONV_WIDTH - 1)
CONV_RB = 64
LN_ROWS = 64
PROJ_COLS = MXU_COLS
VT_ROWS = HEAD_DIM + 16
LOG2E = 1.4426950408889634
Q_SCALE_LOG2 = HALF ** -0.5 * LOG2E
NEG = -1e30


def _np_bucket(rel):
    nb = N_BUCKETS // 2
    ret = np.where(rel > 0, nb, 0)
    n = np.abs(rel)
    max_exact = nb // 2
    nf = np.maximum(n, 1).astype(np.float32)
    large = max_exact + (np.log(nf / np.float32(max_exact)) / np.float32(math.log(MAX_DISTANCE / max_exact))
                         * np.float32(nb - max_exact)).astype(np.int32)
    large = np.minimum(large, nb - 1)
    return (ret + np.where(n < max_exact, n, large)).astype(np.int32)


FAR_BUCKET = N_BUCKETS // 2 - 1


def _lookup(table, idx):
    idx_c = jnp.asarray(idx)[..., None]
    out = jnp.zeros(idx.shape + (table.shape[1],), F32)
    for b in np.unique(idx):
        out = jnp.where(idx_c == int(b), table[int(b)], out)
    return out


def _rms(x, g):
    return x * lax.rsqrt(jnp.mean(x * x, axis=-1, keepdims=True) + RMS_EPS) * g


def _ln(x, g, b):
    mu = jnp.mean(x, axis=-1, keepdims=True)
    xc = x - mu
    var = jnp.mean(xc * xc, axis=-1, keepdims=True)
    return xc * lax.rsqrt(var + LN_EPS) * g + b


def _lam(lamv):
    s1 = jnp.sum(lamv[0:1] * lamv[1:2], axis=-1, keepdims=True)
    s2 = jnp.sum(lamv[2:3] * lamv[3:4], axis=-1, keepdims=True)
    return jnp.exp(s1) - jnp.exp(s2) + LAMBDA_INIT


def _gmlp_cols(z_ref, g):
    base = P_GMLP0 + 3 * g * GROUP_DIM
    return tuple(z_ref[:, base + j * GROUP_DIM:base + (j + 1) * GROUP_DIM] for j in range(3))


def _dot(a, b):
    return jnp.dot(a, b, preferred_element_type=F32)


def _dot_nt(a, b):
    return lax.dot_general(a, b, (((1,), (1,)), ((), ())), preferred_element_type=F32)


def _conv_rows(buf_ref, c, base, rows, wdw_ref, bdw_ref):
    lanes = slice(c * LANE, (c + 1) * LANE)
    acc = jnp.broadcast_to(bdw_ref[:, lanes], (rows, LANE))
    for tap in range(CONV_WIDTH):
        lo = base + tap + CONV_OFF
        acc = acc + wdw_ref[tap:tap + 1, lanes] * buf_ref[c, lo:lo + rows, :]
    return acc


def _prompt_kernel(lamv_ref, x_ref, g0_ref, win_ref, bias_ref, subg_ref, lng_ref, lnb_ref, wm_ref, bs_ref,
                   wout_ref, g1_ref, win1_ref, wdw_ref, bdw_ref, clng_ref, clnb_ref, wout1_ref, fg_ref,
                   y_ref, k_ref, v_ref, tail_ref,
                   z_ref, kbf_ref, vt_ref, qm_ref, m_ref, acc_ref, s_ref, mix_ref,
                   buf_ref, conv_ref, gate_ref, y0_ref, y0p_ref, cn_ref, *, nt, ntiles):
    step = pl.program_id(0)
    i = jnp.minimum(step, ntiles - 1) % nt
    d = x_ref.shape[-1]
    ngroups = d // LANE

    @pl.when(step == 0)
    def _():
        buf_ref[...] = jnp.zeros(buf_ref.shape, F32)
        gate_ref[...] = jnp.zeros(gate_ref.shape, F32)
        y0p_ref[...] = jnp.zeros(y0p_ref.shape, F32)

    x = x_ref[0]
    xn = _rms(x, g0_ref[...]).astype(BF16)
    blocks = [(c, r) for c in range(ngroups) for r in range(TM // CONV_RB)]
    nchunks = L0_IN // PROJ_COLS
    for j in range(nchunks):
        cols = slice(j * PROJ_COLS, (j + 1) * PROJ_COLS)
        z_ref[:, cols] = _dot(xn, win_ref[:, cols])
        for c, r in blocks[j * len(blocks) // nchunks:(j + 1) * len(blocks) // nchunks]:
            conv_ref[r * CONV_RB:(r + 1) * CONV_RB, c * LANE:(c + 1) * LANE] = _conv_rows(
                buf_ref, c, r * CONV_RB, CONV_RB, wdw_ref, bdw_ref)

    for g in range(GROUPS):
        sl = slice(g * GROUP_DIM, (g + 1) * GROUP_DIM)
        ub, vb, gb = _gmlp_cols(z_ref, g)
        zv = _ln(jax.nn.gelu(vb), lng_ref[:, sl], lnb_ref[:, sl]).astype(BF16)
        zu = jax.nn.gelu(ub)
        gate = jax.nn.silu(gb)
        for c in range(TM // B_CHUNK):
            rows = slice(c * B_CHUNK, (c + 1) * B_CHUNK)
            sg = _dot(wm_ref[g], zv[rows]) + bs_ref[:, sl]
            mix_ref[rows, A_WIDTH + g * GROUP_DIM:A_WIDTH + (g + 1) * GROUP_DIM] = (
                zu[rows] * sg * gate[rows]).astype(BF16)

    k = z_ref[:, P_K0:P_K0 + A_WIDTH]
    v = z_ref[:, P_V0:P_V0 + A_WIDTH]
    k_ref[0] = k.reshape(TM, HEADS, HEAD_DIM)
    v_ref[0] = v.reshape(TM, HEADS, HEAD_DIM)
    kbf_ref[i] = k.astype(BF16)
    for h in range(HEADS):
        vt_ref[i, h, 0:HEAD_DIM, :] = v[:, h * HEAD_DIM:(h + 1) * HEAD_DIM].T.astype(BF16)
        vt_ref[i, h, HEAD_DIM:VT_ROWS, :] = jnp.ones((VT_ROWS - HEAD_DIM, TM), BF16)

    lo = lax.broadcasted_iota(jnp.int32, (TM, HEAD_DIM), 1) < HALF
    for h in range(HEADS):
        q = z_ref[:, P_Q0 + h * HEAD_DIM:P_Q0 + (h + 1) * HEAD_DIM] * Q_SCALE_LOG2
        qm_ref[2 * h] = jnp.where(lo, q, 0.0).astype(BF16)
        qm_ref[2 * h + 1] = jnp.where(lo, 0.0, q).astype(BF16)

    units = range(2 * HEADS)

    def attend(tiles, first):
        m_new, alpha = {}, {}
        for u in units:
            h = u // 2
            mt = None
            for t, (j, bias_idx) in enumerate(tiles):
                s = _dot_nt(kbf_ref[j, :, h * HEAD_DIM:(h + 1) * HEAD_DIM], qm_ref[u])
                if bias_idx is not None:
                    s = s + bias_ref[h, bias_idx]
                s_ref[t, u] = s
                smax = jnp.max(s, axis=0, keepdims=True)
                mt = smax if mt is None else jnp.maximum(mt, smax)
            if first:
                m_new[u] = mt
            else:
                m_old = m_ref[u]
                m_new[u] = jnp.maximum(m_old, mt)
                alpha[u] = jnp.exp2(m_old - m_new[u])
            m_ref[u] = m_new[u]
        for u in units:
            pv = None
            for t in range(len(tiles)):
                p = jnp.exp2(s_ref[t, u] - m_new[u]).astype(BF16)
                part = _dot(vt_ref[tiles[t][0], u // 2], p)
                pv = part if pv is None else pv + part
            acc_ref[u] = pv if first else alpha[u] * acc_ref[u] + pv

    @pl.when(i == 0)
    def _():
        attend([(i, 0)], True)

    @pl.when(i == 1)
    def _():
        attend([(i, 0), (i - 1, 1)], True)

    @pl.when(jnp.logical_and(i > 1, i % 2 == 0))
    def _():
        attend([(i, 0), (i - 1, 1), (i - 2, None)], True)

    @pl.when(jnp.logical_and(i > 1, i % 2 == 1))
    def _():
        attend([(i, 0), (i - 1, 1), (i - 2, None), (i - 3, None)], True)

    def far_pair(jj, carry):
        attend([(2 * jj, None), (2 * jj + 1, None)], False)
        return carry

    lax.fori_loop(0, (i - 2 - i % 2) // 2, far_pair, 0)

    y0_ref[...] = x + _dot(mix_ref[:, A_WIDTH:], wout_ref[A_WIDTH:, :])

    lam = _lam(lamv_ref[...])
    for h in range(HEADS):
        on1 = acc_ref[2 * h, 0:HEAD_DIM, :] / acc_ref[2 * h, HEAD_DIM:HEAD_DIM + 1, :]
        on2 = acc_ref[2 * h + 1, 0:HEAD_DIM, :] / acc_ref[2 * h + 1, HEAD_DIM:HEAD_DIM + 1, :]
        o = (on1 - lam * on2).T
        o = _rms(o, subg_ref[...]) * (1.0 - LAMBDA_INIT)
        ga = z_ref[:, P_GA0 + h * HEAD_DIM:P_GA0 + (h + 1) * HEAD_DIM]
        mix_ref[:, h * HEAD_DIM:(h + 1) * HEAD_DIM] = (o * jax.nn.silu(ga)).astype(BF16)

    y0 = y0_ref[...] + _dot(mix_ref[:, 0:A_WIDTH], wout_ref[0:A_WIDTH, :])
    y0_ref[...] = y0

    xn1 = _rms(y0, g1_ref[...]).astype(BF16)
    starts_stream = i == 0
    nrow = TM // LN_ROWS
    n_glu = 2 * d // PROJ_COLS
    for j in range(3 * d // PROJ_COLS):
        cols = slice(j * PROJ_COLS, (j + 1) * PROJ_COLS)
        z_ref[:, cols] = _dot(xn1, win1_ref[:, cols])
        if j < nrow:
            rows = slice(j * LN_ROWS, (j + 1) * LN_ROWS)
            cn_ref[rows, :] = (jax.nn.silu(_ln(conv_ref[rows, :], clng_ref[...], clnb_ref[...]))
                               * gate_ref[rows, :]).astype(BF16)
        if j == nrow:
            y_ref[0] = _rms(y0p_ref[...] + _dot(cn_ref[...], wout1_ref[...]), fg_ref[...])
        if j < n_glu and j % 2 == 1:
            for c in (j - 1, j):
                a0 = (j - 1) * PROJ_COLS + (c % 2) * LANE
                tail = buf_ref[c, TM:TM + CONV_PAD, :]
                tail_ref[0, :, c * LANE:(c + 1) * LANE] = tail
                buf_ref[c, 0:CONV_PAD, :] = jnp.where(starts_stream, 0.0, tail)
                buf_ref[c, CONV_PAD:CONV_PAD + TM, :] = (z_ref[:, a0:a0 + LANE]
                                                         * jax.nn.sigmoid(z_ref[:, a0 + PROJ_COLS:a0 + PROJ_COLS + LANE]))
        if j >= n_glu:
            gcols = slice((j - n_glu) * PROJ_COLS, (j - n_glu + 1) * PROJ_COLS)
            gate_ref[:, gcols] = jax.nn.silu(z_ref[:, cols])
    y0p_ref[...] = y0


def _prompt(x, lamv, g0, win, bias, subg, lng, lnb, wm, bs, wout, g1, win1, wdw, bdw, clng, clnb, wout1, fg):
    bsz, t, d = x.shape
    nt = t // TM
    ntiles = bsz * nt
    const = lambda shape: pl.BlockSpec(shape, lambda s: (0,) * len(shape))
    cur = lambda s: jnp.minimum(s, ntiles - 1)
    done = lambda s: jnp.maximum(s - 1, 0)
    kv_row = pl.BlockSpec((1, TM, HEADS, HEAD_DIM), lambda s: (cur(s) // nt, cur(s) % nt, 0, 0))
    return pl.pallas_call(
        functools.partial(_prompt_kernel, nt=nt, ntiles=ntiles),
        grid=(ntiles + 1,),
        in_specs=[const((4, HALF)), pl.BlockSpec((1, TM, d), lambda s: (cur(s) // nt, cur(s) % nt, 0)),
                  const((1, d)), const((d, L0_IN)),
                  const((HEADS, 2, TM, TM)), const((1, HEAD_DIM)), const((1, B_WIDTH)), const((1, B_WIDTH)),
                  const((GROUPS, B_CHUNK, B_CHUNK)), const((B_CHUNK, B_WIDTH)), const((A_WIDTH + B_WIDTH, d)),
                  const((1, d)), const((d, 3 * d)), const((CONV_WIDTH, d)), const((1, d)),
                  const((1, d)), const((1, d)), const((d, d)), const((1, d))],
        out_specs=[pl.BlockSpec((1, TM, d), lambda s: (done(s) // nt, done(s) % nt, 0)), kv_row, kv_row,
                   pl.BlockSpec((1, CONV_PAD, d), lambda s: (done(s) // nt, 0, 0))],
        out_shape=[jax.ShapeDtypeStruct((bsz, t, d), F32),
                   jax.ShapeDtypeStruct((bsz, t, HEADS, HEAD_DIM), F32),
                   jax.ShapeDtypeStruct((bsz, t, HEADS, HEAD_DIM), F32),
                   jax.ShapeDtypeStruct((bsz, CONV_PAD, d), F32)],
        scratch_shapes=[pltpu.VMEM((TM, L0_IN), F32),
                        pltpu.VMEM((nt, TM, A_WIDTH), BF16),
                        pltpu.VMEM((nt, HEADS, VT_ROWS, TM), BF16),
                        pltpu.VMEM((2 * HEADS, TM, HEAD_DIM), BF16),
                        pltpu.VMEM((2 * HEADS, 1, TM), F32),
                        pltpu.VMEM((2 * HEADS, VT_ROWS, TM), F32),
                        pltpu.VMEM((4, 2 * HEADS, TM, TM), F32),
                        pltpu.VMEM((TM, A_WIDTH + B_WIDTH), BF16),
                        pltpu.VMEM((d // LANE, CONV_PAD + TM, LANE), F32),
                        pltpu.VMEM((TM, d), F32),
                        pltpu.VMEM((TM, d), F32),
                        pltpu.VMEM((TM, d), F32),
                        pltpu.VMEM((TM, d), F32),
                        pltpu.VMEM((TM, d), BF16)],
        compiler_params=pltpu.CompilerParams(dimension_semantics=("arbitrary",), vmem_limit_bytes=VMEM_LIMIT),
        name="prompt",
    )(lamv, x, g0, win, bias, subg, lng, lnb, wm, bs, wout, g1, win1, wdw, bdw, clng, clnb, wout1, fg)


def _s_proj_kernel(x_ref, g0_ref, win_ref, z_ref):
    z_ref[...] = _dot(_rms(x_ref[...], g0_ref[...]).astype(BF16), win_ref[...])


def _s_attn_kernel(lamv_ref, z_ref, ck_ref, cv_ref, bias_ref, o_ref, *, dec_seq, pad_rows):
    ncol = HEADS * 2 * dec_seq
    q = z_ref[:, P_Q0:P_Q0 + A_WIDTH] * (HALF ** -0.5)
    q_rep = jnp.concatenate([q] * (2 * HEADS), axis=0)
    row = lax.broadcasted_iota(jnp.int32, (ncol, A_WIDTH), 0)
    col = lax.broadcasted_iota(jnp.int32, (ncol, A_WIDTH), 1)
    q_bd = jnp.where(col // HALF == row // dec_seq, q_rep, 0.0).astype(BF16)

    zeros = jnp.zeros((pad_rows - dec_seq, A_WIDTH), F32)
    ck = ck_ref[0].reshape(ck_ref.shape[1], A_WIDTH)
    cv = cv_ref[0].reshape(cv_ref.shape[1], A_WIDTH)
    k_all = jnp.concatenate([ck, z_ref[:, P_K0:P_K0 + A_WIDTH], zeros], axis=0).astype(BF16)
    v_all = jnp.concatenate([cv, z_ref[:, P_V0:P_V0 + A_WIDTH], zeros], axis=0).astype(BF16)

    s = _dot_nt(k_all, q_bd) + bias_ref[...]
    p = jnp.exp(s - jnp.max(s, axis=0, keepdims=True))
    p = p / jnp.sum(p, axis=0, keepdims=True)
    w = p - _lam(lamv_ref[...]) * pltpu.roll(p, ncol - dec_seq, axis=1)
    o_all = _dot(w.T.astype(BF16), v_all)
    for h in range(HEADS):
        o_ref[:, h * HEAD_DIM:(h + 1) * HEAD_DIM] = o_all[2 * h * dec_seq:(2 * h + 1) * dec_seq,
                                                          h * HEAD_DIM:(h + 1) * HEAD_DIM]


def _s_rest_kernel(x_ref, z_ref, o_ref, subg_ref, lng_ref, lnb_ref, wbd_ref, bs_ref, wout0_ref,
                   g1_ref, win1_ref, st_ref, wdw_ref, bdw_ref, clng_ref, clnb_ref, wout1_ref, fg_ref,
                   y_ref, zv_ref, u_ref, mix_ref, buf_ref, conv_ref, *, dec_seq):
    d = x_ref.shape[-1]
    nb = st_ref.shape[0]
    for h in range(HEADS):
        sl = slice(h * HEAD_DIM, (h + 1) * HEAD_DIM)
        o = _rms(o_ref[:, sl], subg_ref[...]) * (1.0 - LAMBDA_INIT)
        mix_ref[:, sl] = (o * jax.nn.silu(z_ref[:, P_GA0 + h * HEAD_DIM:P_GA0 + (h + 1) * HEAD_DIM])).astype(BF16)
    for g in range(GROUPS):
        sl = slice(g * GROUP_DIM, (g + 1) * GROUP_DIM)
        ub, vb, gb = _gmlp_cols(z_ref, g)
        zv = _ln(jax.nn.gelu(vb), lng_ref[:, sl], lnb_ref[:, sl])
        zv_ref[:, sl] = zv
        sg = _dot(wbd_ref[g], zv.astype(BF16)) + bs_ref[:, sl]
        mix_ref[:, A_WIDTH + g * GROUP_DIM:A_WIDTH + (g + 1) * GROUP_DIM] = (
            jax.nn.gelu(ub) * sg * jax.nn.silu(gb)).astype(BF16)
    y0 = x_ref[...] + _dot(mix_ref[...], wout0_ref[...])

    z1 = _dot(_rms(y0, g1_ref[...]).astype(BF16), win1_ref[...])
    u = jnp.concatenate([z1[:, a0:a0 + PROJ_COLS] * jax.nn.sigmoid(z1[:, a0 + PROJ_COLS:a0 + 2 * PROJ_COLS])
                         for a0 in range(0, 2 * d, 2 * PROJ_COLS)], axis=1)
    u_ref[...] = u
    ngroups = d // LANE
    for b in range(nb):
        for c in range(ngroups):
            lanes = slice(c * LANE, (c + 1) * LANE)
            buf_ref[b * ngroups + c, 0:CONV_PAD, :] = st_ref[b, :, lanes]
            buf_ref[b * ngroups + c, CONV_PAD:CONV_PAD + dec_seq, :] = u[b * dec_seq:(b + 1) * dec_seq, lanes]
    for b in range(nb):
        for c in range(ngroups):
            conv_ref[b * dec_seq:(b + 1) * dec_seq, c * LANE:(c + 1) * LANE] = _conv_rows(
                buf_ref.at[pl.ds(b * ngroups, ngroups)], c, 0, dec_seq, wdw_ref, bdw_ref)
    c = jax.nn.silu(_ln(conv_ref[...], clng_ref[...], clnb_ref[...])) * jax.nn.silu(z1[:, 2 * d:3 * d])
    y1 = y0 + _dot(c.astype(BF16), wout1_ref[...])
    y_ref[...] = _rms(y1, fg_ref[...])


def _sample(xs, cache_k, cache_v, state, lamv, g0, win0, bias_s, subg, lng, lnb, wbd, bs_s, wout0,
            g1, win1, wdw, bdw, clng, clnb, wout1, fg):
    nb, dec_seq, d = xs.shape
    rows = nb * dec_seq
    past = cache_k.shape[1]
    pad_rows = bias_s.shape[0] - past
    x2 = xs.reshape(rows, d)
    z = pl.pallas_call(_s_proj_kernel, out_shape=jax.ShapeDtypeStruct((rows, L0_IN), F32),
                       compiler_params=pltpu.CompilerParams(vmem_limit_bytes=VMEM_LIMIT),
                       name="s_proj")(x2, g0, win0)

    ncol = HEADS * 2 * dec_seq
    o = pl.pallas_call(
        functools.partial(_s_attn_kernel, dec_seq=dec_seq, pad_rows=pad_rows),
        grid=(nb,),
        in_specs=[pl.BlockSpec((4, HALF), lambda b: (0, 0)),
                  pl.BlockSpec((dec_seq, L0_IN), lambda b: (b, 0)),
                  pl.BlockSpec((1, past, HEADS, HEAD_DIM), lambda b: (b, 0, 0, 0)),
                  pl.BlockSpec((1, past, HEADS, HEAD_DIM), lambda b: (b, 0, 0, 0)),
                  pl.BlockSpec((past + pad_rows, ncol), lambda b: (0, 0))],
        out_specs=pl.BlockSpec((dec_seq, A_WIDTH), lambda b: (b, 0)),
        out_shape=jax.ShapeDtypeStruct((rows, A_WIDTH), F32),
        compiler_params=pltpu.CompilerParams(dimension_semantics=("arbitrary",), vmem_limit_bytes=VMEM_LIMIT),
        name="s_attn",
    )(lamv, z, cache_k, cache_v, bias_s)

    state_pad = jnp.pad(state, ((0, 0), (CONV_OFF, 0), (0, 0)))
    y, zv, u = pl.pallas_call(
        functools.partial(_s_rest_kernel, dec_seq=dec_seq),
        out_shape=[jax.ShapeDtypeStruct((rows, d), F32),
                   jax.ShapeDtypeStruct((rows, B_WIDTH), F32),
                   jax.ShapeDtypeStruct((rows, d), F32)],
        scratch_shapes=[pltpu.VMEM((rows, A_WIDTH + B_WIDTH), BF16),
                        pltpu.VMEM((nb * (d // LANE), CONV_PAD + dec_seq, LANE), F32),
                        pltpu.VMEM((rows, d), F32)],
        compiler_params=pltpu.CompilerParams(vmem_limit_bytes=VMEM_LIMIT),
        name="s_rest",
    )(x2, z, o, subg, lng, lnb, wbd, bs_s, wout0, g1, win1, state_pad, wdw, bdw, clng, clnb, wout1, fg)
    return z, y, zv, u


def kernel(x_prompt, x_sample, cache_k0, cache_v0, state_conv1, rel_bias, norm_g0, w_in0, lambda_q1, lambda_k1,
           lambda_q2, lambda_k2, subln_g0, gv_ln_g0, gv_ln_b0, w_s0, b_s0, w_out0, norm_g1, w_in1, w_dw1, b_dw1,
           conv_ln_g1, conv_ln_b1, w_out1, final_g):
    bsz, t, d = x_prompt.shape
    nb, dec_seq, _ = x_sample.shape
    past = cache_k0.shape[1]
    assert t % TM == 0 and TM % B_CHUNK == 0 and TM % CHUNK == 0 and TM >= MAX_DISTANCE
    assert d % (2 * PROJ_COLS) == 0 and TM // LN_ROWS < 3 * d // PROJ_COLS
    assert dec_seq <= B_CHUNK and dec_seq % 8 == 0 and HEADS * 2 * dec_seq == LANE
    assert past % CHUNK == 0 and (past + dec_seq - 1) // CHUNK == past // CHUNK

    row = lambda a: a.reshape(1, -1).astype(F32)
    lamv = jnp.stack([lambda_q1, lambda_k1, lambda_q2, lambda_k2]).astype(F32)
    wout0, wout1 = w_out0.astype(BF16), w_out1.astype(BF16)
    cols0 = [w_in0[:, o + g * GROUP_DIM:o + (g + 1) * GROUP_DIM] for g in range(GROUPS) for o in (UB0, VB0, GB0)]
    win0 = jnp.concatenate(cols0 + [w_in0[:, :UB0]], axis=1).astype(BF16)
    cols1 = [w_in1[:, o + p * PROJ_COLS:o + (p + 1) * PROJ_COLS] for p in range(d // PROJ_COLS) for o in (0, d)]
    win1 = jnp.concatenate(cols1 + [w_in1[:, 2 * d:]], axis=1).astype(BF16)
    g0, g1, fg, subg = row(norm_g0), row(norm_g1), row(final_g), row(subln_g0)
    lng, lnb = row(gv_ln_g0), row(gv_ln_b0)
    bdw, clng, clnb = row(b_dw1), row(conv_ln_g1), row(conv_ln_b1)
    wdw = w_dw1.astype(F32)

    rbs = (rel_bias - rel_bias[FAR_BUCKET:FAR_BUCKET + 1]).astype(F32)
    assert np.all(_np_bucket(-np.arange(TM + 1, 2 * max(t, past + dec_seq))) == FAR_BUCKET)
    kk = np.arange(TM)[:, None]
    qq = np.arange(TM)[None, :]
    own = jnp.where((kk // CHUNK <= qq // CHUNK)[..., None], _lookup(rbs, _np_bucket(kk - qq)), NEG)
    prev = _lookup(rbs, _np_bucket(kk - qq - TM))
    bias_p = jnp.transpose(jnp.stack([own, prev]), (3, 0, 1, 2)) * LOG2E

    tril = np.tril(np.ones((B_CHUNK, B_CHUNK), dtype=bool))
    wm = jnp.where(tril[None], w_s0, 0)
    bs = jnp.repeat(b_s0.T.astype(F32), GROUP_DIM, axis=1)

    y_prompt, k0p, v0p, tail_p = _prompt(x_prompt, lamv, g0, win0, bias_p, subg, lng, lnb, wm.astype(BF16), bs, wout0,
                                         g1, win1, wdw, bdw, clng, clnb, wout1, fg)

    pad_rows = LANE
    kpos = np.arange(past + dec_seq)[:, None]
    qpos = past + np.arange(dec_seq)[None, :]
    bsmp = jnp.transpose(_lookup(rbs, _np_bucket(kpos - qpos)), (0, 2, 1))
    bsmp = jnp.broadcast_to(bsmp[:, :, None, :], (past + dec_seq, HEADS, 2, dec_seq)).reshape(past + dec_seq, -1)
    bias_s = jnp.pad(bsmp, ((0, pad_rows - dec_seq), (0, 0)), constant_values=NEG)
    wbd = jnp.stack([jnp.kron(jnp.eye(nb, dtype=F32), wm[g, :dec_seq, :dec_seq]) for g in range(GROUPS)])
    bs_s = jnp.tile(bs[:dec_seq], (nb, 1))

    zs, ys, zvs, us = _sample(x_sample, cache_k0, cache_v0, state_conv1, lamv, g0, win0, bias_s, subg, lng, lnb,
                              wbd.astype(BF16), bs_s, wout0, g1, win1, wdw, bdw, clng, clnb, wout1, fg)

    keep = CONV_WIDTH - 1
    return (y_prompt,
            ys.reshape(nb, dec_seq, d),
            k0p,
            v0p,
            tail_p[:, CONV_OFF:],
            zs[:, P_K0:P_K0 + A_WIDTH].reshape(nb, dec_seq, HEADS, HEAD_DIM),
            zs[:, P_V0:P_V0 + A_WIDTH].reshape(nb, dec_seq, HEADS, HEAD_DIM),
            zvs.reshape(nb, dec_seq, B_WIDTH),
            jnp.concatenate([state_conv1, us.reshape(nb, dec_seq, d)], axis=1)[:, -keep:])
```

```python
import functools
import math

import numpy as np
import jax
import jax.numpy as jnp
from jax import lax
from jax.experimental import pallas as pl
from jax.experimental.pallas import tpu as pltpu

F32 = jnp.float32
BF16 = jnp.bfloat16

CHUNK = 64
HEADS = 4
HALF = 64
HEAD_DIM = 2 * HALF
A_WIDTH = HEADS * HEAD_DIM
LAMBDA_INIT = 0.2
N_BUCKETS = 32
MAX_DISTANCE = 128
GROUPS = 4
GROUP_DIM = 128
B_WIDTH = GROUPS * GROUP_DIM
B_CHUNK = 128
CONV_WIDTH = 31
RMS_EPS = 1e-6
LN_EPS = 1e-5

LANE = 128
MXU_COLS = 256
VMEM_LIMIT = 56 * 1024 * 1024

UB0, VB0, GB0 = 4 * 512, 5 * 512, 6 * 512
L0_IN = 7 * 512
P_GMLP0 = 0
P_Q0, P_K0, P_V0, P_GA0 = (3 * B_WIDTH + i * 512 for i in range(4))

TM = 256
CONV_PAD = 32
CONV_OFF = CONV_PAD - (CONV_WIDTH - 1)
CONV_RB = 32
LN_ROWS = 64
PROJ_COLS = MXU_COLS
VT_ROWS = HEAD_DIM + 16
LOG2E = 1.4426950408889634
Q_SCALE_LOG2 = HALF ** -0.5 * LOG2E
NEG = -1e30


def _np_bucket(rel):
    nb = N_BUCKETS // 2
    ret = np.where(rel > 0, nb, 0)
    n = np.abs(rel)
    max_exact = nb // 2
    nf = np.maximum(n, 1).astype(np.float32)
    large = max_exact + (np.log(nf / np.float32(max_exact)) / np.float32(math.log(MAX_DISTANCE / max_exact))
                         * np.float32(nb - max_exact)).astype(np.int32)
    large = np.minimum(large, nb - 1)
    return (ret + np.where(n < max_exact, n, large)).astype(np.int32)


FAR_BUCKET = N_BUCKETS // 2 - 1


def _lookup(table, idx):
    idx_c = jnp.asarray(idx)[..., None]
    out = jnp.zeros(idx.shape + (table.shape[1],), F32)
    for b in np.unique(idx):
        out = jnp.where(idx_c == int(b), table[int(b)], out)
    return out


def _rms(x, g):
    return x * lax.rsqrt(jnp.mean(x * x, axis=-1, keepdims=True) + RMS_EPS) * g


def _ln(x, g, b):
    mu = jnp.mean(x, axis=-1, keepdims=True)
    xc = x - mu
    var = jnp.mean(xc * xc, axis=-1, keepdims=True)
    return xc * lax.rsqrt(var + LN_EPS) * g + b


def _lam(lamv):
    s1 = jnp.sum(lamv[0:1] * lamv[1:2], axis=-1, keepdims=True)
    s2 = jnp.sum(lamv[2:3] * lamv[3:4], axis=-1, keepdims=True)
    return jnp.exp(s1) - jnp.exp(s2) + LAMBDA_INIT


def _gmlp_cols(z_ref, g):
    base = P_GMLP0 + 3 * g * GROUP_DIM
    return tuple(z_ref[:, base + j * GROUP_DIM:base + (j + 1) * GROUP_DIM] for j in range(3))


def _dot(a, b):
    return jnp.dot(a, b, preferred_element_type=F32)


def _dot_nt(a, b):
    return lax.dot_general(a, b, (((1,), (1,)), ((), ())), preferred_element_type=F32)


def _conv_rows(buf_ref, c, base, rows, wdw_ref, bdw_ref):
    lanes = slice(c * LANE, (c + 1) * LANE)
    acc = jnp.broadcast_to(bdw_ref[:, lanes], (rows, LANE))
    for tap in range(CONV_WIDTH):
        lo = base + tap + CONV_OFF
        acc = acc + wdw_ref[tap:tap + 1, lanes] * buf_ref[c, lo:lo + rows, :]
    return acc


def _prompt_kernel(lamv_ref, x_ref, g0_ref, win_ref, bias_ref, subg_ref, lng_ref, lnb_ref, wm_ref, bs_ref,
                   wout_ref, g1_ref, win1_ref, wdw_ref, bdw_ref, clng_ref, clnb_ref, wout1_ref, fg_ref,
                   y_ref, k_ref, v_ref, tail_ref,
                   z_ref, kbf_ref, vt_ref, qm_ref, m_ref, acc_ref, s_ref, mix_ref,
                   buf_ref, conv_ref, gate_ref, y0_ref, y0p_ref, cn_ref, *, nt, ntiles):
    step = pl.program_id(0)
    i = jnp.minimum(step, ntiles - 1) % nt
    d = x_ref.shape[-1]
    ngroups = d // LANE

    @pl.when(step == 0)
    def _():
        buf_ref[...] = jnp.zeros(buf_ref.shape, F32)
        gate_ref[...] = jnp.zeros(gate_ref.shape, F32)
        y0p_ref[...] = jnp.zeros(y0p_ref.shape, F32)

    x = x_ref[0]
    xn = _rms(x, g0_ref[...]).astype(BF16)
    blocks = [(c, r) for c in range(ngroups) for r in range(TM // CONV_RB)]
    nchunks = L0_IN // PROJ_COLS
    for j in range(nchunks):
        cols = slice(j * PROJ_COLS, (j + 1) * PROJ_COLS)
        z_ref[:, cols] = _dot(xn, win_ref[:, cols])
        for c, r in blocks[j * len(blocks) // nchunks:(j + 1) * len(blocks) // nchunks]:
            conv_ref[r * CONV_RB:(r + 1) * CONV_RB, c * LANE:(c + 1) * LANE] = _conv_rows(
                buf_ref, c, r * CONV_RB, CONV_RB, wdw_ref, bdw_ref)

    for g in range(GROUPS):
        sl = slice(g * GROUP_DIM, (g + 1) * GROUP_DIM)
        ub, vb, gb = _gmlp_cols(z_ref, g)
        zv = _ln(jax.nn.gelu(vb), lng_ref[:, sl], lnb_ref[:, sl]).astype(BF16)
        zu = jax.nn.gelu(ub)
        gate = jax.nn.silu(gb)
        for c in range(TM // B_CHUNK):
            rows = slice(c * B_CHUNK, (c + 1) * B_CHUNK)
            sg = _dot(wm_ref[g], zv[rows]) + bs_ref[:, sl]
            mix_ref[rows, A_WIDTH + g * GROUP_DIM:A_WIDTH + (g + 1) * GROUP_DIM] = (
                zu[rows] * sg * gate[rows]).astype(BF16)

    k = z_ref[:, P_K0:P_K0 + A_WIDTH]
    v = z_ref[:, P_V0:P_V0 + A_WIDTH]
    k_ref[0] = k.reshape(TM, HEADS, HEAD_DIM)
    v_ref[0] = v.reshape(TM, HEADS, HEAD_DIM)
    kbf_ref[i] = k.astype(BF16)
    for h in range(HEADS):
        vt_ref[i, h, 0:HEAD_DIM, :] = v[:, h * HEAD_DIM:(h + 1) * HEAD_DIM].T.astype(BF16)
        vt_ref[i, h, HEAD_DIM:VT_ROWS, :] = jnp.ones((VT_ROWS - HEAD_DIM, TM), BF16)

    lo = lax.broadcasted_iota(jnp.int32, (TM, HEAD_DIM), 1) < HALF
    for h in range(HEADS):
        q = z_ref[:, P_Q0 + h * HEAD_DIM:P_Q0 + (h + 1) * HEAD_DIM] * Q_SCALE_LOG2
        qm_ref[2 * h] = jnp.where(lo, q, 0.0).astype(BF16)
        qm_ref[2 * h + 1] = jnp.where(lo, 0.0, q).astype(BF16)

    units = range(2 * HEADS)

    def attend(tiles, first):
        m_new, alpha = {}, {}
        for u in units:
            h = u // 2
            mt = None
            for t, (j, bias_idx) in enumerate(tiles):
                s = _dot_nt(kbf_ref[j, :, h * HEAD_DIM:(h + 1) * HEAD_DIM], qm_ref[u])
                if bias_idx is not None:
                    s = s + bias_ref[h, bias_idx]
                s_ref[t, u] = s
                smax = jnp.max(s, axis=0, keepdims=True)
                mt = smax if mt is None else jnp.maximum(mt, smax)
            if first:
                m_new[u] = mt
            else:
                m_old = m_ref[u]
                m_new[u] = jnp.maximum(m_old, mt)
                alpha[u] = jnp.exp2(m_old - m_new[u])
            m_ref[u] = m_new[u]
        for u in units:
            pv = None
            for t in range(len(tiles)):
                p = jnp.exp2(s_ref[t, u] - m_new[u]).astype(BF16)
                part = _dot(vt_ref[tiles[t][0], u // 2], p)
                pv = part if pv is None else pv + part
            acc_ref[u] = pv if first else alpha[u] * acc_ref[u] + pv

    @pl.when(i == 0)
    def _():
        attend([(i, 0)], True)

    @pl.when(i == 1)
    def _():
        attend([(i, 0), (i - 1, 1)], True)

    @pl.when(jnp.logical_and(i > 1, i % 2 == 0))
    def _():
        attend([(i, 0), (i - 1, 1), (i - 2, None)], True)

    @pl.when(jnp.logical_and(i > 1, i % 2 == 1))
    def _():
        attend([(i, 0), (i - 1, 1), (i - 2, None), (i - 3, None)], True)

    n_left = jnp.maximum(i - 2 - i % 2, 0)

    def far_quad(jj, carry):
        attend([(4 * jj + t, None) for t in range(4)], False)
        return carry

    lax.fori_loop(0, n_left // 4, far_quad, 0)

    @pl.when(n_left % 4 == 2)
    def _():
        attend([(n_left - 2, None), (n_left - 1, None)], False)

    y0_ref[...] = x + _dot(mix_ref[:, A_WIDTH:], wout_ref[A_WIDTH:, :])

    lam = _lam(lamv_ref[...])
    for h in range(HEADS):
        on1 = acc_ref[2 * h, 0:HEAD_DIM, :] / acc_ref[2 * h, HEAD_DIM:HEAD_DIM + 1, :]
        on2 = acc_ref[2 * h + 1, 0:HEAD_DIM, :] / acc_ref[2 * h + 1, HEAD_DIM:HEAD_DIM + 1, :]
        o = (on1 - lam * on2).T
        o = _rms(o, subg_ref[...]) * (1.0 - LAMBDA_INIT)
        ga = z_ref[:, P_GA0 + h * HEAD_DIM:P_GA0 + (h + 1) * HEAD_DIM]
        mix_ref[:, h * HEAD_DIM:(h + 1) * HEAD_DIM] = (o * jax.nn.silu(ga)).astype(BF16)

    y0 = y0_ref[...] + _dot(mix_ref[:, 0:A_WIDTH], wout_ref[0:A_WIDTH, :])
    y0_ref[...] = y0

    xn1 = _rms(y0, g1_ref[...]).astype(BF16)
    starts_stream = i == 0
    nrow = TM // LN_ROWS
    n_glu = 2 * d // PROJ_COLS
    for j in range(3 * d // PROJ_COLS):
        cols = slice(j * PROJ_COLS, (j + 1) * PROJ_COLS)
        z_ref[:, cols] = _dot(xn1, win1_ref[:, cols])
        if j < nrow:
            rows = slice(j * LN_ROWS, (j + 1) * LN_ROWS)
            cn_ref[rows, :] = (jax.nn.silu(_ln(conv_ref[rows, :], clng_ref[...], clnb_ref[...]))
                               * gate_ref[rows, :]).astype(BF16)
        if j == nrow:
            y_ref[0] = _rms(y0p_ref[...] + _dot(cn_ref[...], wout1_ref[...]), fg_ref[...])
        if j < n_glu and j % 2 == 1:
            for c in (j - 1, j):
                a0 = (j - 1) * PROJ_COLS + (c % 2) * LANE
                tail = buf_ref[c, TM:TM + CONV_PAD, :]
                tail_ref[0, :, c * LANE:(c + 1) * LANE] = tail
                buf_ref[c, 0:CONV_PAD, :] = jnp.where(starts_stream, 0.0, tail)
                buf_ref[c, CONV_PAD:CONV_PAD + TM, :] = (z_ref[:, a0:a0 + LANE]
                                                         * jax.nn.sigmoid(z_ref[:, a0 + PROJ_COLS:a0 + PROJ_COLS + LANE]))
        if j >= n_glu:
            gcols = slice((j - n_glu) * PROJ_COLS, (j - n_glu + 1) * PROJ_COLS)
            gate_ref[:, gcols] = jax.nn.silu(z_ref[:, cols])
    y0p_ref[...] = y0


def _prompt(x, lamv, g0, win, bias, subg, lng, lnb, wm, bs, wout, g1, win1, wdw, bdw, clng, clnb, wout1, fg):
    bsz, t, d = x.shape
    nt = t // TM
    ntiles = bsz * nt
    const = lambda shape: pl.BlockSpec(shape, lambda s: (0,) * len(shape))
    cur = lambda s: jnp.minimum(s, ntiles - 1)
    done = lambda s: jnp.maximum(s - 1, 0)
    kv_row = pl.BlockSpec((1, TM, HEADS, HEAD_DIM), lambda s: (cur(s) // nt, cur(s) % nt, 0, 0))
    return pl.pallas_call(
        functools.partial(_prompt_kernel, nt=nt, ntiles=ntiles),
        grid=(ntiles + 1,),
        in_specs=[const((4, HALF)), pl.BlockSpec((1, TM, d), lambda s: (cur(s) // nt, cur(s) % nt, 0)),
                  const((1, d)), const((d, L0_IN)),
                  const((HEADS, 2, TM, TM)), const((1, HEAD_DIM)), const((1, B_WIDTH)), const((1, B_WIDTH)),
                  const((GROUPS, B_CHUNK, B_CHUNK)), const((B_CHUNK, B_WIDTH)), const((A_WIDTH + B_WIDTH, d)),
                  const((1, d)), const((d, 3 * d)), const((CONV_WIDTH, d)), const((1, d)),
                  const((1, d)), const((1, d)), const((d, d)), const((1, d))],
        out_specs=[pl.BlockSpec((1, TM, d), lambda s: (done(s) // nt, done(s) % nt, 0)), kv_row, kv_row,
                   pl.BlockSpec((1, CONV_PAD, d), lambda s: (done(s) // nt, 0, 0))],
        out_shape=[jax.ShapeDtypeStruct((bsz, t, d), F32),
                   jax.ShapeDtypeStruct((bsz, t, HEADS, HEAD_DIM), F32),
                   jax.ShapeDtypeStruct((bsz, t, HEADS, HEAD_DIM), F32),
                   jax.ShapeDtypeStruct((bsz, CONV_PAD, d), F32)],
        scratch_shapes=[pltpu.VMEM((TM, L0_IN), F32),
                        pltpu.VMEM((nt, TM, A_WIDTH), BF16),
                        pltpu.VMEM((nt, HEADS, VT_ROWS, TM), BF16),
                        pltpu.VMEM((2 * HEADS, TM, HEAD_DIM), BF16),
                        pltpu.VMEM((2 * HEADS, 1, TM), F32),
                        pltpu.VMEM((2 * HEADS, VT_ROWS, TM), F32),
                        pltpu.VMEM((4, 2 * HEADS, TM, TM), F32),
                        pltpu.VMEM((TM, A_WIDTH + B_WIDTH), BF16),
                        pltpu.VMEM((d // LANE, CONV_PAD + TM, LANE), F32),
                        pltpu.VMEM((TM, d), F32),
                        pltpu.VMEM((TM, d), F32),
                        pltpu.VMEM((TM, d), F32),
                        pltpu.VMEM((TM, d), F32),
                        pltpu.VMEM((TM, d), BF16)],
        compiler_params=pltpu.CompilerParams(dimension_semantics=("arbitrary",), vmem_limit_bytes=VMEM_LIMIT),
        name="prompt",
    )(lamv, x, g0, win, bias, subg, lng, lnb, wm, bs, wout, g1, win1, wdw, bdw, clng, clnb, wout1, fg)


def _s_proj_kernel(x_ref, g0_ref, win_ref, z_ref):
    z_ref[...] = _dot(_rms(x_ref[...], g0_ref[...]).astype(BF16), win_ref[...])


def _s_attn_kernel(lamv_ref, z_ref, ck_ref, cv_ref, bias_ref, o_ref, *, dec_seq, pad_rows):
    ncol = HEADS * 2 * dec_seq
    q = z_ref[:, P_Q0:P_Q0 + A_WIDTH] * (HALF ** -0.5)
    q_rep = jnp.concatenate([q] * (2 * HEADS), axis=0)
    row = lax.broadcasted_iota(jnp.int32, (ncol, A_WIDTH), 0)
    col = lax.broadcasted_iota(jnp.int32, (ncol, A_WIDTH), 1)
    q_bd = jnp.where(col // HALF == row // dec_seq, q_rep, 0.0).astype(BF16)

    zeros = jnp.zeros((pad_rows - dec_seq, A_WIDTH), F32)
    ck = ck_ref[0].reshape(ck_ref.shape[1], A_WIDTH)
    cv = cv_ref[0].reshape(cv_ref.shape[1], A_WIDTH)
    k_all = jnp.concatenate([ck, z_ref[:, P_K0:P_K0 + A_WIDTH], zeros], axis=0).astype(BF16)
    v_all = jnp.concatenate([cv, z_ref[:, P_V0:P_V0 + A_WIDTH], zeros], axis=0).astype(BF16)

    s = _dot_nt(k_all, q_bd) + bias_ref[...]
    p = jnp.exp(s - jnp.max(s, axis=0, keepdims=True))
    p = p / jnp.sum(p, axis=0, keepdims=True)
    w = p - _lam(lamv_ref[...]) * pltpu.roll(p, ncol - dec_seq, axis=1)
    o_all = _dot(w.T.astype(BF16), v_all)
    for h in range(HEADS):
        o_ref[:, h * HEAD_DIM:(h + 1) * HEAD_DIM] = o_all[2 * h * dec_seq:(2 * h + 1) * dec_seq,
                                                          h * HEAD_DIM:(h + 1) * HEAD_DIM]


def _s_rest_kernel(x_ref, z_ref, o_ref, subg_ref, lng_ref, lnb_ref, wbd_ref, bs_ref, wout0_ref,
                   g1_ref, win1_ref, st_ref, wdw_ref, bdw_ref, clng_ref, clnb_ref, wout1_ref, fg_ref,
                   y_ref, zv_ref, u_ref, mix_ref, buf_ref, conv_ref, *, dec_seq):
    d = x_ref.shape[-1]
    nb = st_ref.shape[0]
    for h in range(HEADS):
        sl = slice(h * HEAD_DIM, (h + 1) * HEAD_DIM)
        o = _rms(o_ref[:, sl], subg_ref[...]) * (1.0 - LAMBDA_INIT)
        mix_ref[:, sl] = (o * jax.nn.silu(z_ref[:, P_GA0 + h * HEAD_DIM:P_GA0 + (h + 1) * HEAD_DIM])).astype(BF16)
    for g in range(GROUPS):
        sl = slice(g * GROUP_DIM, (g + 1) * GROUP_DIM)
        ub, vb, gb = _gmlp_cols(z_ref, g)
        zv = _ln(jax.nn.gelu(vb), lng_ref[:, sl], lnb_ref[:, sl])
        zv_ref[:, sl] = zv
        sg = _dot(wbd_ref[g], zv.astype(BF16)) + bs_ref[:, sl]
        mix_ref[:, A_WIDTH + g * GROUP_DIM:A_WIDTH + (g + 1) * GROUP_DIM] = (
            jax.nn.gelu(ub) * sg * jax.nn.silu(gb)).astype(BF16)
    y0 = x_ref[...] + _dot(mix_ref[...], wout0_ref[...])

    z1 = _dot(_rms(y0, g1_ref[...]).astype(BF16), win1_ref[...])
    u = jnp.concatenate([z1[:, a0:a0 + PROJ_COLS] * jax.nn.sigmoid(z1[:, a0 + PROJ_COLS:a0 + 2 * PROJ_COLS])
                         for a0 in range(0, 2 * d, 2 * PROJ_COLS)], axis=1)
    u_ref[...] = u
    ngroups = d // LANE
    for b in range(nb):
        for c in range(ngroups):
            lanes = slice(c * LANE, (c + 1) * LANE)
            buf_ref[b * ngroups + c, 0:CONV_PAD, :] = st_ref[b, :, lanes]
            buf_ref[b * ngroups + c, CONV_PAD:CONV_PAD + dec_seq, :] = u[b * dec_seq:(b + 1) * dec_seq, lanes]
    for b in range(nb):
        for c in range(ngroups):
            conv_ref[b * dec_seq:(b + 1) * dec_seq, c * LANE:(c + 1) * LANE] = _conv_rows(
                buf_ref.at[pl.ds(b * ngroups, ngroups)], c, 0, dec_seq, wdw_ref, bdw_ref)
    c = jax.nn.silu(_ln(conv_ref[...], clng_ref[...], clnb_ref[...])) * jax.nn.silu(z1[:, 2 * d:3 * d])
    y1 = y0 + _dot(c.astype(BF16), wout1_ref[...])
    y_ref[...] = _rms(y1, fg_ref[...])


def _sample(xs, cache_k, cache_v, state, lamv, g0, win0, bias_s, subg, lng, lnb, wbd, bs_s, wout0,
            g1, win1, wdw, bdw, clng, clnb, wout1, fg):
    nb, dec_seq, d = xs.shape
    rows = nb * dec_seq
    past = cache_k.shape[1]
    pad_rows = bias_s.shape[0] - past
    x2 = xs.reshape(rows, d)
    z = pl.pallas_call(_s_proj_kernel, out_shape=jax.ShapeDtypeStruct((rows, L0_IN), F32),
                       compiler_params=pltpu.CompilerParams(vmem_limit_bytes=VMEM_LIMIT),
                       name="s_proj")(x2, g0, win0)

    ncol = HEADS * 2 * dec_seq
    o = pl.pallas_call(
        functools.partial(_s_attn_kernel, dec_seq=dec_seq, pad_rows=pad_rows),
        grid=(nb,),
        in_specs=[pl.BlockSpec((4, HALF), lambda b: (0, 0)),
                  pl.BlockSpec((dec_seq, L0_IN), lambda b: (b, 0)),
                  pl.BlockSpec((1, past, HEADS, HEAD_DIM), lambda b: (b, 0, 0, 0)),
                  pl.BlockSpec((1, past, HEADS, HEAD_DIM), lambda b: (b, 0, 0, 0)),
                  pl.BlockSpec((past + pad_rows, ncol), lambda b: (0, 0))],
        out_specs=pl.BlockSpec((dec_seq, A_WIDTH), lambda b: (b, 0)),
        out_shape=jax.ShapeDtypeStruct((rows, A_WIDTH), F32),
        compiler_params=pltpu.CompilerParams(dimension_semantics=("arbitrary",), vmem_limit_bytes=VMEM_LIMIT),
        name="s_attn",
    )(lamv, z, cache_k, cache_v, bias_s)

    state_pad = jnp.pad(state, ((0, 0), (CONV_OFF, 0), (0, 0)))
    y, zv, u = pl.pallas_call(
        functools.partial(_s_rest_kernel, dec_seq=dec_seq),
        out_shape=[jax.ShapeDtypeStruct((rows, d), F32),
                   jax.ShapeDtypeStruct((rows, B_WIDTH), F32),
                   jax.ShapeDtypeStruct((rows, d), F32)],
        scratch_shapes=[pltpu.VMEM((rows, A_WIDTH + B_WIDTH), BF16),
                        pltpu.VMEM((nb * (d // LANE), CONV_PAD + dec_seq, LANE), F32),
                        pltpu.VMEM((rows, d), F32)],
        compiler_params=pltpu.CompilerParams(vmem_limit_bytes=VMEM_LIMIT),
        name="s_rest",
    )(x2, z, o, subg, lng, lnb, wbd, bs_s, wout0, g1, win1, state_pad, wdw, bdw, clng, clnb, wout1, fg)
    return z, y, zv, u


def kernel(x_prompt, x_sample, cache_k0, cache_v0, state_conv1, rel_bias, norm_g0, w_in0, lambda_q1, lambda_k1,
           lambda_q2, lambda_k2, subln_g0, gv_ln_g0, gv_ln_b0, w_s0, b_s0, w_out0, norm_g1, w_in1, w_dw1, b_dw1,
           conv_ln_g1, conv_ln_b1, w_out1, final_g):
    bsz, t, d = x_prompt.shape
    nb, dec_seq, _ = x_sample.shape
    past = cache_k0.shape[1]
    assert t % TM == 0 and TM % B_CHUNK == 0 and TM % CHUNK == 0 and TM >= MAX_DISTANCE
    assert d % (2 * PROJ_COLS) == 0 and TM // LN_ROWS < 3 * d // PROJ_COLS
    assert dec_seq <= B_CHUNK and dec_seq % 8 == 0 and HEADS * 2 * dec_seq == LANE
    assert past % CHUNK == 0 and (past + dec_seq - 1) // CHUNK == past // CHUNK

    row = lambda a: a.reshape(1, -1).astype(F32)
    lamv = jnp.stack([lambda_q1, lambda_k1, lambda_q2, lambda_k2]).astype(F32)
    wout0, wout1 = w_out0.astype(BF16), w_out1.astype(BF16)
    cols0 = [w_in0[:, o + g * GROUP_DIM:o + (g + 1) * GROUP_DIM] for g in range(GROUPS) for o in (UB0, VB0, GB0)]
    win0 = jnp.concatenate(cols0 + [w_in0[:, :UB0]], axis=1).astype(BF16)
    cols1 = [w_in1[:, o + p * PROJ_COLS:o + (p + 1) * PROJ_COLS] for p in range(d // PROJ_COLS) for o in (0, d)]
    win1 = jnp.concatenate(cols1 + [w_in1[:, 2 * d:]], axis=1).astype(BF16)
    g0, g1, fg, subg = row(norm_g0), row(norm_g1), row(final_g), row(subln_g0)
    lng, lnb = row(gv_ln_g0), row(gv_ln_b0)
    bdw, clng, clnb = row(b_dw1), row(conv_ln_g1), row(conv_ln_b1)
    wdw = w_dw1.astype(F32)

    rbs = (rel_bias - rel_bias[FAR_BUCKET:FAR_BUCKET + 1]).astype(F32)
    assert np.all(_np_bucket(-np.arange(TM + 1, 2 * max(t, past + dec_seq))) == FAR_BUCKET)
    kk = np.arange(TM)[:, None]
    qq = np.arange(TM)[None, :]
    own = jnp.where((kk // CHUNK <= qq // CHUNK)[..., None], _lookup(rbs, _np_bucket(kk - qq)), NEG)
    prev = _lookup(rbs, _np_bucket(kk - qq - TM))
    bias_p = jnp.transpose(jnp.stack([own, prev]), (3, 0, 1, 2)) * LOG2E

    tril = np.tril(np.ones((B_CHUNK, B_CHUNK), dtype=bool))
    wm = jnp.where(tril[None], w_s0, 0)
    bs = jnp.repeat(b_s0.T.astype(F32), GROUP_DIM, axis=1)

    y_prompt, k0p, v0p, tail_p = _prompt(x_prompt, lamv, g0, win0, bias_p, subg, lng, lnb, wm.astype(BF16), bs, wout0,
                                         g1, win1, wdw, bdw, clng, clnb, wout1, fg)

    pad_rows = LANE
    kpos = np.arange(past + dec_seq)[:, None]
    qpos = past + np.arange(dec_seq)[None, :]
    bsmp = jnp.transpose(_lookup(rbs, _np_bucket(kpos - qpos)), (0, 2, 1))
    bsmp = jnp.broadcast_to(bsmp[:, :, None, :], (past + dec_seq, HEADS, 2, dec_seq)).reshape(past + dec_seq, -1)
    bias_s = jnp.pad(bsmp, ((0, pad_rows - dec_seq), (0, 0)), constant_values=NEG)
    wbd = jnp.stack([jnp.kron(jnp.eye(nb, dtype=F32), wm[g, :dec_seq, :dec_seq]) for g in range(GROUPS)])
    bs_s = jnp.tile(bs[:dec_seq], (nb, 1))

    zs, ys, zvs, us = _sample(x_sample, cache_k0, cache_v0, state_conv1, lamv, g0, win0, bias_s, subg, lng, lnb,
                              wbd.astype(BF16), bs_s, wout0, g1, win1, wdw, bdw, clng, clnb, wout1, fg)

    keep = CONV_WIDTH - 1
    return (y_prompt,
            ys.reshape(nb, dec_seq, d),
            k0p,
            v0p,
            tail_p[:, CONV_OFF:],
            zs[:, P_K0:P_K0 + A_WIDTH].reshape(nb, dec_seq, HEADS, HEAD_DIM),
            zs[:, P_V0:P_V0 + A_WIDTH].reshape(nb, dec_seq, HEADS, HEAD_DIM),
            zvs.reshape(nb, dec_seq, B_WIDTH),
            jnp.concatenate([state_conv1, us.reshape(nb, dec_seq, d)], axis=1)[:, -keep:])
```

```python
import functools
import math

import numpy as np
import jax
import jax.numpy as jnp
from jax import lax
from jax.experimental import pallas as pl
from jax.experimental.pallas import tpu as pltpu

F32 = jnp.float32
BF16 = jnp.bfloat16

CHUNK = 64
HEADS = 4
HALF = 64
HEAD_DIM = 2 * HALF
A_WIDTH = HEADS * HEAD_DIM
LAMBDA_INIT = 0.2
N_BUCKETS = 32
MAX_DISTANCE = 128
GROUPS = 4
GROUP_DIM = 128
B_WIDTH = GROUPS * GROUP_DIM
B_CHUNK = 128
CONV_WIDTH = 31
RMS_EPS = 1e-6
LN_EPS = 1e-5

LANE = 128
MXU_COLS = 256
VMEM_LIMIT = 56 * 1024 * 1024

UB0, VB0, GB0 = 4 * 512, 5 * 512, 6 * 512
L0_IN = 7 * 512
P_GMLP0 = 0
P_Q0, P_K0, P_V0, P_GA0 = (3 * B_WIDTH + i * 512 for i in range(4))

TM = 256
CONV_PAD = 32
CONV_OFF = CONV_PAD - (CONV_WIDTH - 1)
CONV_RB = 32
PHASE_LAG = 3
LN_ROWS = 64
PROJ_COLS = MXU_COLS
VT_ROWS = HEAD_DIM + 16
LOG2E = 1.4426950408889634
Q_SCALE_LOG2 = HALF ** -0.5 * LOG2E
NEG = -1e30


def _np_bucket(rel):
    nb = N_BUCKETS // 2
    ret = np.where(rel > 0, nb, 0)
    n = np.abs(rel)
    max_exact = nb // 2
    nf = np.maximum(n, 1).astype(np.float32)
    large = max_exact + (np.log(nf / np.float32(max_exact)) / np.float32(math.log(MAX_DISTANCE / max_exact))
                         * np.float32(nb - max_exact)).astype(np.int32)
    large = np.minimum(large, nb - 1)
    return (ret + np.where(n < max_exact, n, large)).astype(np.int32)


FAR_BUCKET = N_BUCKETS // 2 - 1


def _lookup(table, idx):
    idx_c = jnp.asarray(idx)[..., None]
    out = jnp.zeros(idx.shape + (table.shape[1],), F32)
    for b in np.unique(idx):
        out = jnp.where(idx_c == int(b), table[int(b)], out)
    return out


def _rms(x, g):
    return x * lax.rsqrt(jnp.mean(x * x, axis=-1, keepdims=True) + RMS_EPS) * g


def _ln(x, g, b):
    mu = jnp.mean(x, axis=-1, keepdims=True)
    xc = x - mu
    var = jnp.mean(xc * xc, axis=-1, keepdims=True)
    return xc * lax.rsqrt(var + LN_EPS) * g + b


def _lam(lamv):
    s1 = jnp.sum(lamv[0:1] * lamv[1:2], axis=-1, keepdims=True)
    s2 = jnp.sum(lamv[2:3] * lamv[3:4], axis=-1, keepdims=True)
    return jnp.exp(s1) - jnp.exp(s2) + LAMBDA_INIT


def _gmlp_cols(z_ref, g):
    base = P_GMLP0 + 3 * g * GROUP_DIM
    return tuple(z_ref[:, base + j * GROUP_DIM:base + (j + 1) * GROUP_DIM] for j in range(3))


def _dot(a, b):
    return jnp.dot(a, b, preferred_element_type=F32)


def _dot_nt(a, b):
    return lax.dot_general(a, b, (((1,), (1,)), ((), ())), preferred_element_type=F32)


def _conv_rows(buf_ref, c, base, rows, wdw_ref, bdw_ref):
    lanes = slice(c * LANE, (c + 1) * LANE)
    acc = jnp.broadcast_to(bdw_ref[:, lanes], (rows, LANE))
    for tap in range(CONV_WIDTH):
        lo = base + tap + CONV_OFF
        acc = acc + wdw_ref[tap:tap + 1, lanes] * buf_ref[c, lo:lo + rows, :]
    return acc


def _prompt_kernel(lamv_ref, x_ref, g0_ref, win_ref, bias_ref, subg_ref, lng_ref, lnb_ref, wm_ref, bs_ref,
                   wout_ref, g1_ref, win1_ref, wdw_ref, bdw_ref, clng_ref, clnb_ref, wout1_ref, fg_ref,
                   y_ref, k_ref, v_ref, tail_ref,
                   z_ref, kbf_ref, vt_ref, qm_ref, m_ref, acc_ref, s_ref, mix_ref,
                   buf_ref, conv_ref, gate_ref, y0_ref, y0p_ref, cn_ref, *, nt, ntiles):
    step = pl.program_id(0)
    i = jnp.minimum(step, ntiles - 1) % nt
    d = x_ref.shape[-1]
    ngroups = d // LANE

    @pl.when(step == 0)
    def _():
        buf_ref[...] = jnp.zeros(buf_ref.shape, F32)
        gate_ref[...] = jnp.zeros(gate_ref.shape, F32)
        y0p_ref[...] = jnp.zeros(y0p_ref.shape, F32)

    x = x_ref[0]
    xn = _rms(x, g0_ref[...]).astype(BF16)
    blocks = [(c, r) for c in range(ngroups) for r in range(TM // CONV_RB)]
    nchunks = L0_IN // PROJ_COLS
    for j in range(nchunks):
        cols = slice(j * PROJ_COLS, (j + 1) * PROJ_COLS)
        z_ref[:, cols] = _dot(xn, win_ref[:, cols])
        for c, r in blocks[j * len(blocks) // nchunks:(j + 1) * len(blocks) // nchunks]:
            conv_ref[r * CONV_RB:(r + 1) * CONV_RB, c * LANE:(c + 1) * LANE] = _conv_rows(
                buf_ref, c, r * CONV_RB, CONV_RB, wdw_ref, bdw_ref)

    for g in range(GROUPS):
        sl = slice(g * GROUP_DIM, (g + 1) * GROUP_DIM)
        ub, vb, gb = _gmlp_cols(z_ref, g)
        zv = _ln(jax.nn.gelu(vb), lng_ref[:, sl], lnb_ref[:, sl]).astype(BF16)
        zu = jax.nn.gelu(ub)
        gate = jax.nn.silu(gb)
        for c in range(TM // B_CHUNK):
            rows = slice(c * B_CHUNK, (c + 1) * B_CHUNK)
            sg = _dot(wm_ref[g], zv[rows]) + bs_ref[:, sl]
            mix_ref[rows, A_WIDTH + g * GROUP_DIM:A_WIDTH + (g + 1) * GROUP_DIM] = (
                zu[rows] * sg * gate[rows]).astype(BF16)

    k = z_ref[:, P_K0:P_K0 + A_WIDTH]
    v = z_ref[:, P_V0:P_V0 + A_WIDTH]
    k_ref[0] = k.reshape(TM, HEADS, HEAD_DIM)
    v_ref[0] = v.reshape(TM, HEADS, HEAD_DIM)
    kbf_ref[i] = k.astype(BF16)
    for h in range(HEADS):
        vt_ref[i, h, 0:HEAD_DIM, :] = v[:, h * HEAD_DIM:(h + 1) * HEAD_DIM].T.astype(BF16)
        vt_ref[i, h, HEAD_DIM:VT_ROWS, :] = jnp.ones((VT_ROWS - HEAD_DIM, TM), BF16)

    lo = lax.broadcasted_iota(jnp.int32, (TM, HEAD_DIM), 1) < HALF
    for h in range(HEADS):
        q = z_ref[:, P_Q0 + h * HEAD_DIM:P_Q0 + (h + 1) * HEAD_DIM] * Q_SCALE_LOG2
        qm_ref[2 * h] = jnp.where(lo, q, 0.0).astype(BF16)
        qm_ref[2 * h + 1] = jnp.where(lo, 0.0, q).astype(BF16)

    units = range(2 * HEADS)

    def attend(tiles, first):
        m_new, alpha = {}, {}

        def scores(u):
            h = u // 2
            mt = None
            for t, (j, bias_idx) in enumerate(tiles):
                s = _dot_nt(kbf_ref[j, :, h * HEAD_DIM:(h + 1) * HEAD_DIM], qm_ref[u])
                if bias_idx is not None:
                    s = s + bias_ref[h, bias_idx]
                s_ref[t, u] = s
                smax = jnp.max(s, axis=0, keepdims=True)
                mt = smax if mt is None else jnp.maximum(mt, smax)
            if first:
                m_new[u] = mt
            else:
                m_old = m_ref[u]
                m_new[u] = jnp.maximum(m_old, mt)
                alpha[u] = jnp.exp2(m_old - m_new[u])
            m_ref[u] = m_new[u]

        def values(u):
            pv = None
            for t in range(len(tiles)):
                p = jnp.exp2(s_ref[t, u] - m_new[u]).astype(BF16)
                part = _dot(vt_ref[tiles[t][0], u // 2], p)
                pv = part if pv is None else pv + part
            acc_ref[u] = pv if first else alpha[u] * acc_ref[u] + pv

        n = len(units)
        lag = PHASE_LAG if len(tiles) > 1 else n
        for k in range(n + lag):
            if k < n:
                scores(k)
            if 0 <= k - lag < n:
                values(k - lag)

    @pl.when(i == 0)
    def _():
        attend([(i, 0)], True)

    @pl.when(i == 1)
    def _():
        attend([(i, 0), (i - 1, 1)], True)

    @pl.when(jnp.logical_and(i > 1, i % 2 == 0))
    def _():
        attend([(i, 0), (i - 1, 1), (i - 2, None)], True)

    @pl.when(jnp.logical_and(i > 1, i % 2 == 1))
    def _():
        attend([(i, 0), (i - 1, 1), (i - 2, None), (i - 3, None)], True)

    n_left = jnp.maximum(i - 2 - i % 2, 0)

    def far_quad(jj, carry):
        attend([(4 * jj + t, None) for t in range(4)], False)
        return carry

    lax.fori_loop(0, n_left // 4, far_quad, 0)

    @pl.when(n_left % 4 == 2)
    def _():
        attend([(n_left - 2, None), (n_left - 1, None)], False)

    y0_ref[...] = x + _dot(mix_ref[:, A_WIDTH:], wout_ref[A_WIDTH:, :])

    lam = _lam(lamv_ref[...])
    for h in range(HEADS):
        on1 = acc_ref[2 * h, 0:HEAD_DIM, :] / acc_ref[2 * h, HEAD_DIM:HEAD_DIM + 1, :]
        on2 = acc_ref[2 * h + 1, 0:HEAD_DIM, :] / acc_ref[2 * h + 1, HEAD_DIM:HEAD_DIM + 1, :]
        o = (on1 - lam * on2).T
        o = _rms(o, subg_ref[...]) * (1.0 - LAMBDA_INIT)
        ga = z_ref[:, P_GA0 + h * HEAD_DIM:P_GA0 + (h + 1) * HEAD_DIM]
        mix_ref[:, h * HEAD_DIM:(h + 1) * HEAD_DIM] = (o * jax.nn.silu(ga)).astype(BF16)

    y0 = y0_ref[...] + _dot(mix_ref[:, 0:A_WIDTH], wout_ref[0:A_WIDTH, :])
    y0_ref[...] = y0

    xn1 = _rms(y0, g1_ref[...]).astype(BF16)
    starts_stream = i == 0
    nrow = TM // LN_ROWS
    n_glu = 2 * d // PROJ_COLS
    for j in range(3 * d // PROJ_COLS):
        cols = slice(j * PROJ_COLS, (j + 1) * PROJ_COLS)
        z_ref[:, cols] = _dot(xn1, win1_ref[:, cols])
        if j < nrow:
            rows = slice(j * LN_ROWS, (j + 1) * LN_ROWS)
            cn_ref[rows, :] = (jax.nn.silu(_ln(conv_ref[rows, :], clng_ref[...], clnb_ref[...]))
                               * gate_ref[rows, :]).astype(BF16)
        if j == nrow:
            y_ref[0] = _rms(y0p_ref[...] + _dot(cn_ref[...], wout1_ref[...]), fg_ref[...])
        if j < n_glu and j % 2 == 1:
            for c in (j - 1, j):
                a0 = (j - 1) * PROJ_COLS + (c % 2) * LANE
                tail = buf_ref[c, TM:TM + CONV_PAD, :]
                tail_ref[0, :, c * LANE:(c + 1) * LANE] = tail
                buf_ref[c, 0:CONV_PAD, :] = jnp.where(starts_stream, 0.0, tail)
                buf_ref[c, CONV_PAD:CONV_PAD + TM, :] = (z_ref[:, a0:a0 + LANE]
                                                         * jax.nn.sigmoid(z_ref[:, a0 + PROJ_COLS:a0 + PROJ_COLS + LANE]))
        if j >= n_glu:
            gcols = slice((j - n_glu) * PROJ_COLS, (j - n_glu + 1) * PROJ_COLS)
            gate_ref[:, gcols] = jax.nn.silu(z_ref[:, cols])
    y0p_ref[...] = y0


def _prompt(x, lamv, g0, win, bias, subg, lng, lnb, wm, bs, wout, g1, win1, wdw, bdw, clng, clnb, wout1, fg):
    bsz, t, d = x.shape
    nt = t // TM
    ntiles = bsz * nt
    const = lambda shape: pl.BlockSpec(shape, lambda s: (0,) * len(shape))
    cur = lambda s: jnp.minimum(s, ntiles - 1)
    done = lambda s: jnp.maximum(s - 1, 0)
    kv_row = pl.BlockSpec((1, TM, HEADS, HEAD_DIM), lambda s: (cur(s) // nt, cur(s) % nt, 0, 0))
    return pl.pallas_call(
        functools.partial(_prompt_kernel, nt=nt, ntiles=ntiles),
        grid=(ntiles + 1,),
        in_specs=[const((4, HALF)), pl.BlockSpec((1, TM, d), lambda s: (cur(s) // nt, cur(s) % nt, 0)),
                  const((1, d)), const((d, L0_IN)),
                  const((HEADS, 2, TM, TM)), const((1, HEAD_DIM)), const((1, B_WIDTH)), const((1, B_WIDTH)),
                  const((GROUPS, B_CHUNK, B_CHUNK)), const((B_CHUNK, B_WIDTH)), const((A_WIDTH + B_WIDTH, d)),
                  const((1, d)), const((d, 3 * d)), const((CONV_WIDTH, d)), const((1, d)),
                  const((1, d)), const((1, d)), const((d, d)), const((1, d))],
        out_specs=[pl.BlockSpec((1, TM, d), lambda s: (done(s) // nt, done(s) % nt, 0)), kv_row, kv_row,
                   pl.BlockSpec((1, CONV_PAD, d), lambda s: (done(s) // nt, 0, 0))],
        out_shape=[jax.ShapeDtypeStruct((bsz, t, d), F32),
                   jax.ShapeDtypeStruct((bsz, t, HEADS, HEAD_DIM), F32),
                   jax.ShapeDtypeStruct((bsz, t, HEADS, HEAD_DIM), F32),
                   jax.ShapeDtypeStruct((bsz, CONV_PAD, d), F32)],
        scratch_shapes=[pltpu.VMEM((TM, L0_IN), F32),
                        pltpu.VMEM((nt, TM, A_WIDTH), BF16),
                        pltpu.VMEM((nt, HEADS, VT_ROWS, TM), BF16),
                        pltpu.VMEM((2 * HEADS, TM, HEAD_DIM), BF16),
                        pltpu.VMEM((2 * HEADS, 1, TM), F32),
                        pltpu.VMEM((2 * HEADS, VT_ROWS, TM), F32),
                        pltpu.VMEM((4, 2 * HEADS, TM, TM), F32),
                        pltpu.VMEM((TM, A_WIDTH + B_WIDTH), BF16),
                        pltpu.VMEM((d // LANE, CONV_PAD + TM, LANE), F32),
                        pltpu.VMEM((TM, d), F32),
                        pltpu.VMEM((TM, d), F32),
                        pltpu.VMEM((TM, d), F32),
                        pltpu.VMEM((TM, d), F32),
                        pltpu.VMEM((TM, d), BF16)],
        compiler_params=pltpu.CompilerParams(dimension_semantics=("arbitrary",), vmem_limit_bytes=VMEM_LIMIT),
        name="prompt",
    )(lamv, x, g0, win, bias, subg, lng, lnb, wm, bs, wout, g1, win1, wdw, bdw, clng, clnb, wout1, fg)


def _s_proj_kernel(x_ref, g0_ref, win_ref, z_ref):
    z_ref[...] = _dot(_rms(x_ref[...], g0_ref[...]).astype(BF16), win_ref[...])


def _s_attn_kernel(lamv_ref, z_ref, ck_ref, cv_ref, bias_ref, o_ref, *, dec_seq, pad_rows):
    ncol = HEADS * 2 * dec_seq
    q = z_ref[:, P_Q0:P_Q0 + A_WIDTH] * (HALF ** -0.5)
    q_rep = jnp.concatenate([q] * (2 * HEADS), axis=0)
    row = lax.broadcasted_iota(jnp.int32, (ncol, A_WIDTH), 0)
    col = lax.broadcasted_iota(jnp.int32, (ncol, A_WIDTH), 1)
    q_bd = jnp.where(col // HALF == row // dec_seq, q_rep, 0.0).astype(BF16)

    zeros = jnp.zeros((pad_rows - dec_seq, A_WIDTH), F32)
    ck = ck_ref[0].reshape(ck_ref.shape[1], A_WIDTH)
    cv = cv_ref[0].reshape(cv_ref.shape[1], A_WIDTH)
    k_all = jnp.concatenate([ck, z_ref[:, P_K0:P_K0 + A_WIDTH], zeros], axis=0).astype(BF16)
    v_all = jnp.concatenate([cv, z_ref[:, P_V0:P_V0 + A_WIDTH], zeros], axis=0).astype(BF16)

    s = _dot_nt(k_all, q_bd) + bias_ref[...]
    p = jnp.exp(s - jnp.max(s, axis=0, keepdims=True))
    p = p / jnp.sum(p, axis=0, keepdims=True)
    w = p - _lam(lamv_ref[...]) * pltpu.roll(p, ncol - dec_seq, axis=1)
    o_all = _dot(w.T.astype(BF16), v_all)
    for h in range(HEADS):
        o_ref[:, h * HEAD_DIM:(h + 1) * HEAD_DIM] = o_all[2 * h * dec_seq:(2 * h + 1) * dec_seq,
                                                          h * HEAD_DIM:(h + 1) * HEAD_DIM]


def _s_rest_kernel(x_ref, z_ref, o_ref, subg_ref, lng_ref, lnb_ref, wbd_ref, bs_ref, wout0_ref,
                   g1_ref, win1_ref, st_ref, wdw_ref, bdw_ref, clng_ref, clnb_ref, wout1_ref, fg_ref,
                   y_ref, zv_ref, u_ref, mix_ref, buf_ref, conv_ref, *, dec_seq):
    d = x_ref.shape[-1]
    nb = st_ref.shape[0]
    for h in range(HEADS):
        sl = slice(h * HEAD_DIM, (h + 1) * HEAD_DIM)
        o = _rms(o_ref[:, sl], subg_ref[...]) * (1.0 - LAMBDA_INIT)
        mix_ref[:, sl] = (o * jax.nn.silu(z_ref[:, P_GA0 + h * HEAD_DIM:P_GA0 + (h + 1) * HEAD_DIM])).astype(BF16)
    for g in range(GROUPS):
        sl = slice(g * GROUP_DIM, (g + 1) * GROUP_DIM)
        ub, vb, gb = _gmlp_cols(z_ref, g)
        zv = _ln(jax.nn.gelu(vb), lng_ref[:, sl], lnb_ref[:, sl])
        zv_ref[:, sl] = zv
        sg = _dot(wbd_ref[g], zv.astype(BF16)) + bs_ref[:, sl]
        mix_ref[:, A_WIDTH + g * GROUP_DIM:A_WIDTH + (g + 1) * GROUP_DIM] = (
            jax.nn.gelu(ub) * sg * jax.nn.silu(gb)).astype(BF16)
    y0 = x_ref[...] + _dot(mix_ref[...], wout0_ref[...])

    z1 = _dot(_rms(y0, g1_ref[...]).astype(BF16), win1_ref[...])
    u = jnp.concatenate([z1[:, a0:a0 + PROJ_COLS] * jax.nn.sigmoid(z1[:, a0 + PROJ_COLS:a0 + 2 * PROJ_COLS])
                         for a0 in range(0, 2 * d, 2 * PROJ_COLS)], axis=1)
    u_ref[...] = u
    ngroups = d // LANE
    for b in range(nb):
        for c in range(ngroups):
            lanes = slice(c * LANE, (c + 1) * LANE)
            buf_ref[b * ngroups + c, 0:CONV_PAD, :] = st_ref[b, :, lanes]
            buf_ref[b * ngroups + c, CONV_PAD:CONV_PAD + dec_seq, :] = u[b * dec_seq:(b + 1) * dec_seq, lanes]
    for b in range(nb):
        for c in range(ngroups):
            conv_ref[b * dec_seq:(b + 1) * dec_seq, c * LANE:(c + 1) * LANE] = _conv_rows(
                buf_ref.at[pl.ds(b * ngroups, ngroups)], c, 0, dec_seq, wdw_ref, bdw_ref)
    c = jax.nn.silu(_ln(conv_ref[...], clng_ref[...], clnb_ref[...])) * jax.nn.silu(z1[:, 2 * d:3 * d])
    y1 = y0 + _dot(c.astype(BF16), wout1_ref[...])
    y_ref[...] = _rms(y1, fg_ref[...])


def _sample(xs, cache_k, cache_v, state, lamv, g0, win0, bias_s, subg, lng, lnb, wbd, bs_s, wout0,
            g1, win1, wdw, bdw, clng, clnb, wout1, fg):
    nb, dec_seq, d = xs.shape
    rows = nb * dec_seq
    past = cache_k.shape[1]
    pad_rows = bias_s.shape[0] - past
    x2 = xs.reshape(rows, d)
    z = pl.pallas_call(_s_proj_kernel, out_shape=jax.ShapeDtypeStruct((rows, L0_IN), F32),
                       compiler_params=pltpu.CompilerParams(vmem_limit_bytes=VMEM_LIMIT),
                       name="s_proj")(x2, g0, win0)

    ncol = HEADS * 2 * dec_seq
    o = pl.pallas_call(
        functools.partial(_s_attn_kernel, dec_seq=dec_seq, pad_rows=pad_rows),
        grid=(nb,),
        in_specs=[pl.BlockSpec((4, HALF), lambda b: (0, 0)),
                  pl.BlockSpec((dec_seq, L0_IN), lambda b: (b, 0)),
                  pl.BlockSpec((1, past, HEADS, HEAD_DIM), lambda b: (b, 0, 0, 0)),
                  pl.BlockSpec((1, past, HEADS, HEAD_DIM), lambda b: (b, 0, 0, 0)),
                  pl.BlockSpec((past + pad_rows, ncol), lambda b: (0, 0))],
        out_specs=pl.BlockSpec((dec_seq, A_WIDTH), lambda b: (b, 0)),
        out_shape=jax.ShapeDtypeStruct((rows, A_WIDTH), F32),
        compiler_params=pltpu.CompilerParams(dimension_semantics=("arbitrary",), vmem_limit_bytes=VMEM_LIMIT),
        name="s_attn",
    )(lamv, z, cache_k, cache_v, bias_s)

    state_pad = jnp.pad(state, ((0, 0), (CONV_OFF, 0), (0, 0)))
    y, zv, u = pl.pallas_call(
        functools.partial(_s_rest_kernel, dec_seq=dec_seq),
        out_shape=[jax.ShapeDtypeStruct((rows, d), F32),
                   jax.ShapeDtypeStruct((rows, B_WIDTH), F32),
                   jax.ShapeDtypeStruct((rows, d), F32)],
        scratch_shapes=[pltpu.VMEM((rows, A_WIDTH + B_WIDTH), BF16),
                        pltpu.VMEM((nb * (d // LANE), CONV_PAD + dec_seq, LANE), F32),
                        pltpu.VMEM((rows, d), F32)],
        compiler_params=pltpu.CompilerParams(vmem_limit_bytes=VMEM_LIMIT),
        name="s_rest",
    )(x2, z, o, subg, lng, lnb, wbd, bs_s, wout0, g1, win1, state_pad, wdw, bdw, clng, clnb, wout1, fg)
    return z, y, zv, u


def kernel(x_prompt, x_sample, cache_k0, cache_v0, state_conv1, rel_bias, norm_g0, w_in0, lambda_q1, lambda_k1,
           lambda_q2, lambda_k2, subln_g0, gv_ln_g0, gv_ln_b0, w_s0, b_s0, w_out0, norm_g1, w_in1, w_dw1, b_dw1,
           conv_ln_g1, conv_ln_b1, w_out1, final_g):
    bsz, t, d = x_prompt.shape
    nb, dec_seq, _ = x_sample.shape
    past = cache_k0.shape[1]
    assert t % TM == 0 and TM % B_CHUNK == 0 and TM % CHUNK == 0 and TM >= MAX_DISTANCE
    assert d % (2 * PROJ_COLS) == 0 and TM // LN_ROWS < 3 * d // PROJ_COLS
    assert dec_seq <= B_CHUNK and dec_seq % 8 == 0 and HEADS * 2 * dec_seq == LANE
    assert past % CHUNK == 0 and (past + dec_seq - 1) // CHUNK == past // CHUNK

    row = lambda a: a.reshape(1, -1).astype(F32)
    lamv = jnp.stack([lambda_q1, lambda_k1, lambda_q2, lambda_k2]).astype(F32)
    wout0, wout1 = w_out0.astype(BF16), w_out1.astype(BF16)
    cols0 = [w_in0[:, o + g * GROUP_DIM:o + (g + 1) * GROUP_DIM] for g in range(GROUPS) for o in (UB0, VB0, GB0)]
    win0 = jnp.concatenate(cols0 + [w_in0[:, :UB0]], axis=1).astype(BF16)
    cols1 = [w_in1[:, o + p * PROJ_COLS:o + (p + 1) * PROJ_COLS] for p in range(d // PROJ_COLS) for o in (0, d)]
    win1 = jnp.concatenate(cols1 + [w_in1[:, 2 * d:]], axis=1).astype(BF16)
    g0, g1, fg, subg = row(norm_g0), row(norm_g1), row(final_g), row(subln_g0)
    lng, lnb = row(gv_ln_g0), row(gv_ln_b0)
    bdw, clng, clnb = row(b_dw1), row(conv_ln_g1), row(conv_ln_b1)
    wdw = w_dw1.astype(F32)

    rbs = (rel_bias - rel_bias[FAR_BUCKET:FAR_BUCKET + 1]).astype(F32)
    assert np.all(_np_bucket(-np.arange(TM + 1, 2 * max(t, past + dec_seq))) == FAR_BUCKET)
    kk = np.arange(TM)[:, None]
    qq = np.arange(TM)[None, :]
    own = jnp.where((kk // CHUNK <= qq // CHUNK)[..., None], _lookup(rbs, _np_bucket(kk - qq)), NEG)
    prev = _lookup(rbs, _np_bucket(kk - qq - TM))
    bias_p = jnp.transpose(jnp.stack([own, prev]), (3, 0, 1, 2)) * LOG2E

    tril = np.tril(np.ones((B_CHUNK, B_CHUNK), dtype=bool))
    wm = jnp.where(tril[None], w_s0, 0)
    bs = jnp.repeat(b_s0.T.astype(F32), GROUP_DIM, axis=1)

    y_prompt, k0p, v0p, tail_p = _prompt(x_prompt, lamv, g0, win0, bias_p, subg, lng, lnb, wm.astype(BF16), bs, wout0,
                                         g1, win1, wdw, bdw, clng, clnb, wout1, fg)

    pad_rows = LANE
    kpos = np.arange(past + dec_seq)[:, None]
    qpos = past + np.arange(dec_seq)[None, :]
    bsmp = jnp.transpose(_lookup(rbs, _np_bucket(kpos - qpos)), (0, 2, 1))
    bsmp = jnp.broadcast_to(bsmp[:, :, None, :], (past + dec_seq, HEADS, 2, dec_seq)).reshape(past + dec_seq, -1)
    bias_s = jnp.pad(bsmp, ((0, pad_rows - dec_seq), (0, 0)), constant_values=NEG)
    wbd = jnp.stack([jnp.kron(jnp.eye(nb, dtype=F32), wm[g, :dec_seq, :dec_seq]) for g in range(GROUPS)])
    bs_s = jnp.tile(bs[:dec_seq], (nb, 1))

    zs, ys, zvs, us = _sample(x_sample, cache_k0, cache_v0, state_conv1, lamv, g0, win0, bias_s, subg, lng, lnb,
                              wbd.astype(BF16), bs_s, wout0, g1, win1, wdw, bdw, clng, clnb, wout1, fg)

    keep = CONV_WIDTH - 1
    return (y_prompt,
            ys.reshape(nb, dec_seq, d),
            k0p,
            v0p,
            tail_p[:, CONV_OFF:],
            zs[:, P_K0:P_K0 + A_WIDTH].reshape(nb, dec_seq, HEADS, HEAD_DIM),
            zs[:, P_V0:P_V0 + A_WIDTH].reshape(nb, dec_seq, HEADS, HEAD_DIM),
            zvs.reshape(nb, dec_seq, B_WIDTH),
            jnp.concatenate([state_conv1, us.reshape(nb, dec_seq, d)], axis=1)[:, -keep:])
```

```python
import functools
import math

import numpy as np
import jax
import jax.numpy as jnp
from jax import lax
from jax.experimental import pallas as pl
from jax.experimental.pallas import tpu as pltpu

F32 = jnp.float32
BF16 = jnp.bfloat16

CHUNK = 64
HEADS = 4
HALF = 64
HEAD_DIM = 2 * HALF
A_WIDTH = HEADS * HEAD_DIM
LAMBDA_INIT = 0.2
N_BUCKETS = 32
MAX_DISTANCE = 128
GROUPS = 4
GROUP_DIM = 128
B_WIDTH = GROUPS * GROUP_DIM
B_CHUNK = 128
CONV_WIDTH = 31
RMS_EPS = 1e-6
LN_EPS = 1e-5

LANE = 128
MXU_COLS = 256
VMEM_LIMIT = 56 * 1024 * 1024

UB0, VB0, GB0 = 4 * 512, 5 * 512, 6 * 512
L0_IN = 7 * 512
P_GMLP0 = 0
P_Q0, P_K0, P_V0, P_GA0 = (3 * B_WIDTH + i * 512 for i in range(4))

TM = 256
CONV_PAD = 32
CONV_OFF = CONV_PAD - (CONV_WIDTH - 1)
CONV_RB = 32
PHASE_LAG = 3
LN_ROWS = 64
PROJ_COLS = MXU_COLS
VT_ROWS = HEAD_DIM + 16
LOG2E = 1.4426950408889634
Q_SCALE_LOG2 = HALF ** -0.5 * LOG2E
NEG = -1e30


def _np_bucket(rel):
    nb = N_BUCKETS // 2
    ret = np.where(rel > 0, nb, 0)
    n = np.abs(rel)
    max_exact = nb // 2
    nf = np.maximum(n, 1).astype(np.float32)
    large = max_exact + (np.log(nf / np.float32(max_exact)) / np.float32(math.log(MAX_DISTANCE / max_exact))
                         * np.float32(nb - max_exact)).astype(np.int32)
    large = np.minimum(large, nb - 1)
    return (ret + np.where(n < max_exact, n, large)).astype(np.int32)


FAR_BUCKET = N_BUCKETS // 2 - 1


def _lookup(table, idx):
    idx_c = jnp.asarray(idx)[..., None]
    out = jnp.zeros(idx.shape + (table.shape[1],), F32)
    for b in np.unique(idx):
        out = jnp.where(idx_c == int(b), table[int(b)], out)
    return out


def _rms(x, g):
    return x * lax.rsqrt(jnp.mean(x * x, axis=-1, keepdims=True) + RMS_EPS) * g


def _ln(x, g, b):
    mu = jnp.mean(x, axis=-1, keepdims=True)
    xc = x - mu
    var = jnp.mean(xc * xc, axis=-1, keepdims=True)
    return xc * lax.rsqrt(var + LN_EPS) * g + b


def _lam(lamv):
    s1 = jnp.sum(lamv[0:1] * lamv[1:2], axis=-1, keepdims=True)
    s2 = jnp.sum(lamv[2:3] * lamv[3:4], axis=-1, keepdims=True)
    return jnp.exp(s1) - jnp.exp(s2) + LAMBDA_INIT


def _gmlp_cols(z_ref, g):
    base = P_GMLP0 + 3 * g * GROUP_DIM
    return tuple(z_ref[:, base + j * GROUP_DIM:base + (j + 1) * GROUP_DIM] for j in range(3))


def _proj0_rhs(win_ref, j):
    pieces = []
    for c in range(j * PROJ_COLS, (j + 1) * PROJ_COLS, GROUP_DIM):
        if c < 3 * B_WIDTH:
            g, part = divmod(c // GROUP_DIM, 3)
            src = (UB0, VB0, GB0)[part] + g * GROUP_DIM
        else:
            src = c - 3 * B_WIDTH
        pieces.append(win_ref[:, src:src + GROUP_DIM])
    return jnp.concatenate(pieces, axis=1)


def _proj1_src(j, d):
    return (j // 2) * PROJ_COLS + (j % 2) * d if j < 2 * d // PROJ_COLS else j * PROJ_COLS


def _dot(a, b):
    return jnp.dot(a, b, preferred_element_type=F32)


def _dot_nt(a, b):
    return lax.dot_general(a, b, (((1,), (1,)), ((), ())), preferred_element_type=F32)


def _conv_rows(buf_ref, c, base, rows, wdw_ref, bdw_ref):
    lanes = slice(c * LANE, (c + 1) * LANE)
    acc = jnp.broadcast_to(bdw_ref[:, lanes], (rows, LANE))
    for tap in range(CONV_WIDTH):
        lo = base + tap + CONV_OFF
        acc = acc + wdw_ref[tap:tap + 1, lanes] * buf_ref[c, lo:lo + rows, :]
    return acc


def _prompt_kernel(lamv_ref, x_ref, g0_ref, win_ref, bias_ref, subg_ref, lng_ref, lnb_ref, wm_ref, bs_ref,
                   wout_ref, g1_ref, win1_ref, wdw_ref, bdw_ref, clng_ref, clnb_ref, wout1_ref, fg_ref,
                   y_ref, k_ref, v_ref, tail_ref,
                   z_ref, kbf_ref, vt_ref, qm_ref, m_ref, acc_ref, s_ref, mix_ref,
                   buf_ref, conv_ref, gate_ref, y0_ref, y0p_ref, cn_ref, *, nt, ntiles):
    step = pl.program_id(0)
    i = jnp.minimum(step, ntiles - 1) % nt
    d = x_ref.shape[-1]
    ngroups = d // LANE

    @pl.when(step == 0)
    def _():
        buf_ref[...] = jnp.zeros(buf_ref.shape, F32)
        gate_ref[...] = jnp.zeros(gate_ref.shape, F32)
        y0p_ref[...] = jnp.zeros(y0p_ref.shape, F32)

    x = x_ref[0]
    xn = _rms(x, g0_ref[...]).astype(BF16)
    blocks = [(c, r) for c in range(ngroups) for r in range(TM // CONV_RB)]
    nchunks = L0_IN // PROJ_COLS
    for j in range(nchunks):
        cols = slice(j * PROJ_COLS, (j + 1) * PROJ_COLS)
        z_ref[:, cols] = _dot(xn, _proj0_rhs(win_ref, j))
        for c, r in blocks[j * len(blocks) // nchunks:(j + 1) * len(blocks) // nchunks]:
            conv_ref[r * CONV_RB:(r + 1) * CONV_RB, c * LANE:(c + 1) * LANE] = _conv_rows(
                buf_ref, c, r * CONV_RB, CONV_RB, wdw_ref, bdw_ref)

    for g in range(GROUPS):
        sl = slice(g * GROUP_DIM, (g + 1) * GROUP_DIM)
        ub, vb, gb = _gmlp_cols(z_ref, g)
        zv = _ln(jax.nn.gelu(vb), lng_ref[:, sl], lnb_ref[:, sl]).astype(BF16)
        zu = jax.nn.gelu(ub)
        gate = jax.nn.silu(gb)
        for c in range(TM // B_CHUNK):
            rows = slice(c * B_CHUNK, (c + 1) * B_CHUNK)
            sg = _dot(wm_ref[g], zv[rows]) + bs_ref[:, sl]
            mix_ref[rows, A_WIDTH + g * GROUP_DIM:A_WIDTH + (g + 1) * GROUP_DIM] = (
                zu[rows] * sg * gate[rows]).astype(BF16)

    k = z_ref[:, P_K0:P_K0 + A_WIDTH]
    v = z_ref[:, P_V0:P_V0 + A_WIDTH]
    k_ref[0] = k.reshape(TM, HEADS, HEAD_DIM)
    v_ref[0] = v.reshape(TM, HEADS, HEAD_DIM)
    kbf_ref[i] = k.astype(BF16)
    for h in range(HEADS):
        vt_ref[i, h, 0:HEAD_DIM, :] = v[:, h * HEAD_DIM:(h + 1) * HEAD_DIM].T.astype(BF16)
        vt_ref[i, h, HEAD_DIM:VT_ROWS, :] = jnp.ones((VT_ROWS - HEAD_DIM, TM), BF16)

    lo = lax.broadcasted_iota(jnp.int32, (TM, HEAD_DIM), 1) < HALF
    for h in range(HEADS):
        q = z_ref[:, P_Q0 + h * HEAD_DIM:P_Q0 + (h + 1) * HEAD_DIM] * Q_SCALE_LOG2
        qm_ref[2 * h] = jnp.where(lo, q, 0.0).astype(BF16)
        qm_ref[2 * h + 1] = jnp.where(lo, 0.0, q).astype(BF16)

    units = range(2 * HEADS)

    def attend(tiles, first):
        m_new, alpha = {}, {}

        def scores(u):
            h = u // 2
            mt = None
            for t, (j, bias_idx) in enumerate(tiles):
                s = _dot_nt(kbf_ref[j, :, h * HEAD_DIM:(h + 1) * HEAD_DIM], qm_ref[u])
                if bias_idx is not None:
                    s = s + bias_ref[h, bias_idx]
                s_ref[t, u] = s
                smax = jnp.max(s, axis=0, keepdims=True)
                mt = smax if mt is None else jnp.maximum(mt, smax)
            if first:
                m_new[u] = mt
            else:
                m_old = m_ref[u]
                m_new[u] = jnp.maximum(m_old, mt)
                alpha[u] = jnp.exp2(m_old - m_new[u])
            m_ref[u] = m_new[u]

        def values(u):
            pv = None
            for t in range(len(tiles)):
                p = jnp.exp2(s_ref[t, u] - m_new[u]).astype(BF16)
                part = _dot(vt_ref[tiles[t][0], u // 2], p)
                pv = part if pv is None else pv + part
            acc_ref[u] = pv if first else alpha[u] * acc_ref[u] + pv

        n = len(units)
        lag = PHASE_LAG if len(tiles) > 1 else n
        for k in range(n + lag):
            if k < n:
                scores(k)
            if 0 <= k - lag < n:
                values(k - lag)

    @pl.when(i == 0)
    def _():
        attend([(i, 0)], True)

    @pl.when(i == 1)
    def _():
        attend([(i, 0), (i - 1, 1)], True)

    @pl.when(jnp.logical_and(i > 1, i % 2 == 0))
    def _():
        attend([(i, 0), (i - 1, 1), (i - 2, None)], True)

    @pl.when(jnp.logical_and(i > 1, i % 2 == 1))
    def _():
        attend([(i, 0), (i - 1, 1), (i - 2, None), (i - 3, None)], True)

    n_left = jnp.maximum(i - 2 - i % 2, 0)

    def far_quad(jj, carry):
        attend([(4 * jj + t, None) for t in range(4)], False)
        return carry

    lax.fori_loop(0, n_left // 4, far_quad, 0)

    @pl.when(n_left % 4 == 2)
    def _():
        attend([(n_left - 2, None), (n_left - 1, None)], False)

    y0_ref[...] = x + _dot(mix_ref[:, A_WIDTH:], wout_ref[A_WIDTH:, :])

    lam = _lam(lamv_ref[...])
    for h in range(HEADS):
        on1 = acc_ref[2 * h, 0:HEAD_DIM, :] / acc_ref[2 * h, HEAD_DIM:HEAD_DIM + 1, :]
        on2 = acc_ref[2 * h + 1, 0:HEAD_DIM, :] / acc_ref[2 * h + 1, HEAD_DIM:HEAD_DIM + 1, :]
        o = (on1 - lam * on2).T
        o = _rms(o, subg_ref[...]) * (1.0 - LAMBDA_INIT)
        ga = z_ref[:, P_GA0 + h * HEAD_DIM:P_GA0 + (h + 1) * HEAD_DIM]
        mix_ref[:, h * HEAD_DIM:(h + 1) * HEAD_DIM] = (o * jax.nn.silu(ga)).astype(BF16)

    y0 = y0_ref[...] + _dot(mix_ref[:, 0:A_WIDTH], wout_ref[0:A_WIDTH, :])
    y0_ref[...] = y0

    xn1 = _rms(y0, g1_ref[...]).astype(BF16)
    starts_stream = i == 0
    nrow = TM // LN_ROWS
    n_glu = 2 * d // PROJ_COLS
    for j in range(3 * d // PROJ_COLS):
        cols = slice(j * PROJ_COLS, (j + 1) * PROJ_COLS)
        src = _proj1_src(j, d)
        z_ref[:, cols] = _dot(xn1, win1_ref[:, src:src + PROJ_COLS])
        if j < nrow:
            rows = slice(j * LN_ROWS, (j + 1) * LN_ROWS)
            cn_ref[rows, :] = (jax.nn.silu(_ln(conv_ref[rows, :], clng_ref[...], clnb_ref[...]))
                               * gate_ref[rows, :]).astype(BF16)
        if j == nrow:
            y_ref[0] = _rms(y0p_ref[...] + _dot(cn_ref[...], wout1_ref[...]), fg_ref[...])
        if j < n_glu and j % 2 == 1:
            for c in (j - 1, j):
                a0 = (j - 1) * PROJ_COLS + (c % 2) * LANE
                tail = buf_ref[c, TM:TM + CONV_PAD, :]
                tail_ref[0, :, c * LANE:(c + 1) * LANE] = tail
                buf_ref[c, 0:CONV_PAD, :] = jnp.where(starts_stream, 0.0, tail)
                buf_ref[c, CONV_PAD:CONV_PAD + TM, :] = (z_ref[:, a0:a0 + LANE]
                                                         * jax.nn.sigmoid(z_ref[:, a0 + PROJ_COLS:a0 + PROJ_COLS + LANE]))
        if j >= n_glu:
            gcols = slice((j - n_glu) * PROJ_COLS, (j - n_glu + 1) * PROJ_COLS)
            gate_ref[:, gcols] = jax.nn.silu(z_ref[:, cols])
    y0p_ref[...] = y0


def _prompt(x, lamv, g0, win, bias, subg, lng, lnb, wm, bs, wout, g1, win1, wdw, bdw, clng, clnb, wout1, fg):
    bsz, t, d = x.shape
    nt = t // TM
    ntiles = bsz * nt
    const = lambda shape: pl.BlockSpec(shape, lambda s: (0,) * len(shape))
    cur = lambda s: jnp.minimum(s, ntiles - 1)
    done = lambda s: jnp.maximum(s - 1, 0)
    kv_row = pl.BlockSpec((1, TM, HEADS, HEAD_DIM), lambda s: (cur(s) // nt, cur(s) % nt, 0, 0))
    return pl.pallas_call(
        functools.partial(_prompt_kernel, nt=nt, ntiles=ntiles),
        grid=(ntiles + 1,),
        in_specs=[const((4, HALF)), pl.BlockSpec((1, TM, d), lambda s: (cur(s) // nt, cur(s) % nt, 0)),
                  const((1, d)), const((d, L0_IN)),
                  const((HEADS, 2, TM, TM)), const((1, HEAD_DIM)), const((1, B_WIDTH)), const((1, B_WIDTH)),
                  const((GROUPS, B_CHUNK, B_CHUNK)), const((B_CHUNK, B_WIDTH)), const((A_WIDTH + B_WIDTH, d)),
                  const((1, d)), const((d, 3 * d)), const((CONV_WIDTH, d)), const((1, d)),
                  const((1, d)), const((1, d)), const((d, d)), const((1, d))],
        out_specs=[pl.BlockSpec((1, TM, d), lambda s: (done(s) // nt, done(s) % nt, 0)), kv_row, kv_row,
                   pl.BlockSpec((1, CONV_PAD, d), lambda s: (done(s) // nt, 0, 0))],
        out_shape=[jax.ShapeDtypeStruct((bsz, t, d), F32),
                   jax.ShapeDtypeStruct((bsz, t, HEADS, HEAD_DIM), F32),
                   jax.ShapeDtypeStruct((bsz, t, HEADS, HEAD_DIM), F32),
                   jax.ShapeDtypeStruct((bsz, CONV_PAD, d), F32)],
        scratch_shapes=[pltpu.VMEM((TM, L0_IN), F32),
                        pltpu.VMEM((nt, TM, A_WIDTH), BF16),
                        pltpu.VMEM((nt, HEADS, VT_ROWS, TM), BF16),
                        pltpu.VMEM((2 * HEADS, TM, HEAD_DIM), BF16),
                        pltpu.VMEM((2 * HEADS, 1, TM), F32),
                        pltpu.VMEM((2 * HEADS, VT_ROWS, TM), F32),
                        pltpu.VMEM((4, 2 * HEADS, TM, TM), F32),
                        pltpu.VMEM((TM, A_WIDTH + B_WIDTH), BF16),
                        pltpu.VMEM((d // LANE, CONV_PAD + TM, LANE), F32),
                        pltpu.VMEM((TM, d), F32),
                        pltpu.VMEM((TM, d), F32),
                        pltpu.VMEM((TM, d), F32),
                        pltpu.VMEM((TM, d), F32),
                        pltpu.VMEM((TM, d), BF16)],
        compiler_params=pltpu.CompilerParams(dimension_semantics=("arbitrary",), vmem_limit_bytes=VMEM_LIMIT),
        name="prompt",
    )(lamv, x, g0, win, bias, subg, lng, lnb, wm, bs, wout, g1, win1, wdw, bdw, clng, clnb, wout1, fg)


def _s_proj_kernel(x_ref, g0_ref, win_ref, z_ref):
    xn = _rms(x_ref[...], g0_ref[...]).astype(BF16)
    for j in range(L0_IN // PROJ_COLS):
        z_ref[:, j * PROJ_COLS:(j + 1) * PROJ_COLS] = _dot(xn, _proj0_rhs(win_ref, j))


def _s_attn_kernel(lamv_ref, z_ref, ck_ref, cv_ref, bias_ref, o_ref, *, dec_seq, pad_rows):
    ncol = HEADS * 2 * dec_seq
    q = z_ref[:, P_Q0:P_Q0 + A_WIDTH] * (HALF ** -0.5)
    q_rep = jnp.concatenate([q] * (2 * HEADS), axis=0)
    row = lax.broadcasted_iota(jnp.int32, (ncol, A_WIDTH), 0)
    col = lax.broadcasted_iota(jnp.int32, (ncol, A_WIDTH), 1)
    q_bd = jnp.where(col // HALF == row // dec_seq, q_rep, 0.0).astype(BF16)

    zeros = jnp.zeros((pad_rows - dec_seq, A_WIDTH), F32)
    ck = ck_ref[0].reshape(ck_ref.shape[1], A_WIDTH)
    cv = cv_ref[0].reshape(cv_ref.shape[1], A_WIDTH)
    k_all = jnp.concatenate([ck, z_ref[:, P_K0:P_K0 + A_WIDTH], zeros], axis=0).astype(BF16)
    v_all = jnp.concatenate([cv, z_ref[:, P_V0:P_V0 + A_WIDTH], zeros], axis=0).astype(BF16)

    s = _dot_nt(k_all, q_bd) + bias_ref[...]
    p = jnp.exp(s - jnp.max(s, axis=0, keepdims=True))
    p = p / jnp.sum(p, axis=0, keepdims=True)
    w = p - _lam(lamv_ref[...]) * pltpu.roll(p, ncol - dec_seq, axis=1)
    o_all = _dot(w.T.astype(BF16), v_all)
    for h in range(HEADS):
        o_ref[:, h * HEAD_DIM:(h + 1) * HEAD_DIM] = o_all[2 * h * dec_seq:(2 * h + 1) * dec_seq,
                                                          h * HEAD_DIM:(h + 1) * HEAD_DIM]


def _s_rest_kernel(x_ref, z_ref, o_ref, subg_ref, lng_ref, lnb_ref, wbd_ref, bs_ref, wout0_ref,
                   g1_ref, win1_ref, st_ref, wdw_ref, bdw_ref, clng_ref, clnb_ref, wout1_ref, fg_ref,
                   y_ref, zv_ref, u_ref, mix_ref, buf_ref, conv_ref, *, dec_seq):
    d = x_ref.shape[-1]
    nb = st_ref.shape[0]
    for h in range(HEADS):
        sl = slice(h * HEAD_DIM, (h + 1) * HEAD_DIM)
        o = _rms(o_ref[:, sl], subg_ref[...]) * (1.0 - LAMBDA_INIT)
        mix_ref[:, sl] = (o * jax.nn.silu(z_ref[:, P_GA0 + h * HEAD_DIM:P_GA0 + (h + 1) * HEAD_DIM])).astype(BF16)
    for g in range(GROUPS):
        sl = slice(g * GROUP_DIM, (g + 1) * GROUP_DIM)
        ub, vb, gb = _gmlp_cols(z_ref, g)
        zv = _ln(jax.nn.gelu(vb), lng_ref[:, sl], lnb_ref[:, sl])
        zv_ref[:, sl] = zv
        sg = _dot(wbd_ref[g], zv.astype(BF16)) + bs_ref[:, sl]
        mix_ref[:, A_WIDTH + g * GROUP_DIM:A_WIDTH + (g + 1) * GROUP_DIM] = (
            jax.nn.gelu(ub) * sg * jax.nn.silu(gb)).astype(BF16)
    y0 = x_ref[...] + _dot(mix_ref[...], wout0_ref[...])

    z1 = _dot(_rms(y0, g1_ref[...]).astype(BF16), win1_ref[...])
    u = z1[:, 0:d] * jax.nn.sigmoid(z1[:, d:2 * d])
    u_ref[...] = u
    ngroups = d // LANE
    for b in range(nb):
        for c in range(ngroups):
            lanes = slice(c * LANE, (c + 1) * LANE)
            buf_ref[b * ngroups + c, 0:CONV_PAD, :] = st_ref[b, :, lanes]
            buf_ref[b * ngroups + c, CONV_PAD:CONV_PAD + dec_seq, :] = u[b * dec_seq:(b + 1) * dec_seq, lanes]
    for b in range(nb):
        for c in range(ngroups):
            conv_ref[b * dec_seq:(b + 1) * dec_seq, c * LANE:(c + 1) * LANE] = _conv_rows(
                buf_ref.at[pl.ds(b * ngroups, ngroups)], c, 0, dec_seq, wdw_ref, bdw_ref)
    c = jax.nn.silu(_ln(conv_ref[...], clng_ref[...], clnb_ref[...])) * jax.nn.silu(z1[:, 2 * d:3 * d])
    y1 = y0 + _dot(c.astype(BF16), wout1_ref[...])
    y_ref[...] = _rms(y1, fg_ref[...])


def _sample(xs, cache_k, cache_v, state, lamv, g0, win0, bias_s, subg, lng, lnb, wbd, bs_s, wout0,
            g1, win1, wdw, bdw, clng, clnb, wout1, fg):
    nb, dec_seq, d = xs.shape
    rows = nb * dec_seq
    past = cache_k.shape[1]
    pad_rows = bias_s.shape[0] - past
    x2 = xs.reshape(rows, d)
    z = pl.pallas_call(_s_proj_kernel, out_shape=jax.ShapeDtypeStruct((rows, L0_IN), F32),
                       compiler_params=pltpu.CompilerParams(vmem_limit_bytes=VMEM_LIMIT),
                       name="s_proj")(x2, g0, win0)

    ncol = HEADS * 2 * dec_seq
    o = pl.pallas_call(
        functools.partial(_s_attn_kernel, dec_seq=dec_seq, pad_rows=pad_rows),
        grid=(nb,),
        in_specs=[pl.BlockSpec((4, HALF), lambda b: (0, 0)),
                  pl.BlockSpec((dec_seq, L0_IN), lambda b: (b, 0)),
                  pl.BlockSpec((1, past, HEADS, HEAD_DIM), lambda b: (b, 0, 0, 0)),
                  pl.BlockSpec((1, past, HEADS, HEAD_DIM), lambda b: (b, 0, 0, 0)),
                  pl.BlockSpec((past + pad_rows, ncol), lambda b: (0, 0))],
        out_specs=pl.BlockSpec((dec_seq, A_WIDTH), lambda b: (b, 0)),
        out_shape=jax.ShapeDtypeStruct((rows, A_WIDTH), F32),
        compiler_params=pltpu.CompilerParams(dimension_semantics=("arbitrary",), vmem_limit_bytes=VMEM_LIMIT),
        name="s_attn",
    )(lamv, z, cache_k, cache_v, bias_s)

    state_pad = jnp.pad(state, ((0, 0), (CONV_OFF, 0), (0, 0)))
    y, zv, u = pl.pallas_call(
        functools.partial(_s_rest_kernel, dec_seq=dec_seq),
        out_shape=[jax.ShapeDtypeStruct((rows, d), F32),
                   jax.ShapeDtypeStruct((rows, B_WIDTH), F32),
                   jax.ShapeDtypeStruct((rows, d), F32)],
        scratch_shapes=[pltpu.VMEM((rows, A_WIDTH + B_WIDTH), BF16),
                        pltpu.VMEM((nb * (d // LANE), CONV_PAD + dec_seq, LANE), F32),
                        pltpu.VMEM((rows, d), F32)],
        compiler_params=pltpu.CompilerParams(vmem_limit_bytes=VMEM_LIMIT),
        name="s_rest",
    )(x2, z, o, subg, lng, lnb, wbd, bs_s, wout0, g1, win1, state_pad, wdw, bdw, clng, clnb, wout1, fg)
    return z, y, zv, u


def kernel(x_prompt, x_sample, cache_k0, cache_v0, state_conv1, rel_bias, norm_g0, w_in0, lambda_q1, lambda_k1,
           lambda_q2, lambda_k2, subln_g0, gv_ln_g0, gv_ln_b0, w_s0, b_s0, w_out0, norm_g1, w_in1, w_dw1, b_dw1,
           conv_ln_g1, conv_ln_b1, w_out1, final_g):
    bsz, t, d = x_prompt.shape
    nb, dec_seq, _ = x_sample.shape
    past = cache_k0.shape[1]
    assert t % TM == 0 and TM % B_CHUNK == 0 and TM % CHUNK == 0 and TM >= MAX_DISTANCE
    assert d % (2 * PROJ_COLS) == 0 and TM // LN_ROWS < 3 * d // PROJ_COLS
    assert dec_seq <= B_CHUNK and dec_seq % 8 == 0 and HEADS * 2 * dec_seq == LANE
    assert past % CHUNK == 0 and (past + dec_seq - 1) // CHUNK == past // CHUNK

    row = lambda a: a.reshape(1, -1).astype(F32)
    lamv = jnp.stack([lambda_q1, lambda_k1, lambda_q2, lambda_k2]).astype(F32)
    wout0, wout1 = w_out0.astype(BF16), w_out1.astype(BF16)
    win0, win1 = w_in0.astype(BF16), w_in1.astype(BF16)
    g0, g1, fg, subg = row(norm_g0), row(norm_g1), row(final_g), row(subln_g0)
    lng, lnb = row(gv_ln_g0), row(gv_ln_b0)
    bdw, clng, clnb = row(b_dw1), row(conv_ln_g1), row(conv_ln_b1)
    wdw = w_dw1.astype(F32)

    rbs = (rel_bias - rel_bias[FAR_BUCKET:FAR_BUCKET + 1]).astype(F32)
    assert np.all(_np_bucket(-np.arange(TM + 1, 2 * max(t, past + dec_seq))) == FAR_BUCKET)
    kk = np.arange(TM)[:, None]
    qq = np.arange(TM)[None, :]
    own = jnp.where((kk // CHUNK <= qq // CHUNK)[..., None], _lookup(rbs, _np_bucket(kk - qq)), NEG)
    prev = _lookup(rbs, _np_bucket(kk - qq - TM))
    bias_p = jnp.transpose(jnp.stack([own, prev]), (3, 0, 1, 2)) * LOG2E

    tril = np.tril(np.ones((B_CHUNK, B_CHUNK), dtype=bool))
    wm = jnp.where(tril[None], w_s0, 0)
    bs = jnp.repeat(b_s0.T.astype(F32), GROUP_DIM, axis=1)

    y_prompt, k0p, v0p, tail_p = _prompt(x_prompt, lamv, g0, win0, bias_p, subg, lng, lnb, wm.astype(BF16), bs, wout0,
                                         g1, win1, wdw, bdw, clng, clnb, wout1, fg)

    pad_rows = LANE
    kpos = np.arange(past + dec_seq)[:, None]
    qpos = past + np.arange(dec_seq)[None, :]
    bsmp = jnp.transpose(_lookup(rbs, _np_bucket(kpos - qpos)), (0, 2, 1))
    bsmp = jnp.broadcast_to(bsmp[:, :, None, :], (past + dec_seq, HEADS, 2, dec_seq)).reshape(past + dec_seq, -1)
    bias_s = jnp.pad(bsmp, ((0, pad_rows - dec_seq), (0, 0)), constant_values=NEG)
    wbd = jnp.stack([jnp.kron(jnp.eye(nb, dtype=F32), wm[g, :dec_seq, :dec_seq]) for g in range(GROUPS)])
    bs_s = jnp.tile(bs[:dec_seq], (nb, 1))

    zs, ys, zvs, us = _sample(x_sample, cache_k0, cache_v0, state_conv1, lamv, g0, win0, bias_s, subg, lng, lnb,
                              wbd.astype(BF16), bs_s, wout0, g1, win1, wdw, bdw, clng, clnb, wout1, fg)

    keep = CONV_WIDTH - 1
    return (y_prompt,
            ys.reshape(nb, dec_seq, d),
            k0p,
            v0p,
            tail_p[:, CONV_OFF:],
            zs[:, P_K0:P_K0 + A_WIDTH].reshape(nb, dec_seq, HEADS, HEAD_DIM),
            zs[:, P_V0:P_V0 + A_WIDTH].reshape(nb, dec_seq, HEADS, HEAD_DIM),
            zvs.reshape(nb, dec_seq, B_WIDTH),
            jnp.concatenate([state_conv1, us.reshape(nb, dec_seq, d)], axis=1)[:, -keep:])
```

```python
import functools
import math

import numpy as np
import jax
import jax.numpy as jnp
from jax import lax
from jax.experimental import pallas as pl
from jax.experimental.pallas import tpu as pltpu

F32 = jnp.float32
BF16 = jnp.bfloat16

CHUNK = 64
HEADS = 4
HALF = 64
HEAD_DIM = 2 * HALF
A_WIDTH = HEADS * HEAD_DIM
LAMBDA_INIT = 0.2
N_BUCKETS = 32
MAX_DISTANCE = 128
GROUPS = 4
GROUP_DIM = 128
B_WIDTH = GROUPS * GROUP_DIM
B_CHUNK = 128
CONV_WIDTH = 31
RMS_EPS = 1e-6
LN_EPS = 1e-5

LANE = 128
MXU_COLS = 256
VMEM_LIMIT = 56 * 1024 * 1024

UB0, VB0, GB0 = 4 * 512, 5 * 512, 6 * 512
L0_IN = 7 * 512
P_GMLP0 = 0
P_Q0, P_K0, P_V0, P_GA0 = (3 * B_WIDTH + i * 512 for i in range(4))

TM = 256
CONV_PAD = 32
CONV_OFF = CONV_PAD - (CONV_WIDTH - 1)
CONV_RB = 32
PHASE_LAG = 3
LN_ROWS = 64
PROJ_COLS = MXU_COLS
VT_ROWS = HEAD_DIM + 16
LOG2E = 1.4426950408889634
Q_SCALE_LOG2 = HALF ** -0.5 * LOG2E
NEG = -1e30


def _np_bucket(rel):
    nb = N_BUCKETS // 2
    ret = np.where(rel > 0, nb, 0)
    n = np.abs(rel)
    max_exact = nb // 2
    nf = np.maximum(n, 1).astype(np.float32)
    large = max_exact + (np.log(nf / np.float32(max_exact)) / np.float32(math.log(MAX_DISTANCE / max_exact))
                         * np.float32(nb - max_exact)).astype(np.int32)
    large = np.minimum(large, nb - 1)
    return (ret + np.where(n < max_exact, n, large)).astype(np.int32)


FAR_BUCKET = N_BUCKETS // 2 - 1


def _lookup(table, idx):
    idx_c = jnp.asarray(idx)[..., None]
    out = jnp.zeros(idx.shape + (table.shape[1],), F32)
    for b in np.unique(idx):
        out = jnp.where(idx_c == int(b), table[int(b)], out)
    return out


def _rms(x, g):
    return x * lax.rsqrt(jnp.mean(x * x, axis=-1, keepdims=True) + RMS_EPS) * g


def _ln(x, g, b):
    mu = jnp.mean(x, axis=-1, keepdims=True)
    xc = x - mu
    var = jnp.mean(xc * xc, axis=-1, keepdims=True)
    return xc * lax.rsqrt(var + LN_EPS) * g + b


def _lam(lamv):
    s1 = jnp.sum(lamv[0:1] * lamv[1:2], axis=-1, keepdims=True)
    s2 = jnp.sum(lamv[2:3] * lamv[3:4], axis=-1, keepdims=True)
    return jnp.exp(s1) - jnp.exp(s2) + LAMBDA_INIT


def _gmlp_cols(z_ref, g):
    base = P_GMLP0 + g * GROUP_DIM
    return tuple(z_ref[:, base + j * B_WIDTH:base + j * B_WIDTH + GROUP_DIM] for j in range(3))


def _proj0_rhs(win_ref, j):
    c = j * PROJ_COLS
    src = UB0 + c if c < 3 * B_WIDTH else c - 3 * B_WIDTH
    return win_ref[:, src:src + PROJ_COLS]


def _proj1_src(j, d):
    return (j // 2) * PROJ_COLS + (j % 2) * d if j < 2 * d // PROJ_COLS else j * PROJ_COLS


def _dot(a, b):
    return jnp.dot(a, b, preferred_element_type=F32)


def _dot_nt(a, b):
    return lax.dot_general(a, b, (((1,), (1,)), ((), ())), preferred_element_type=F32)


def _conv_rows(buf_ref, c, base, rows, wdw_ref, bdw_ref):
    lanes = slice(c * LANE, (c + 1) * LANE)
    acc = jnp.broadcast_to(bdw_ref[:, lanes], (rows, LANE))
    for tap in range(CONV_WIDTH):
        lo = base + tap + CONV_OFF
        acc = acc + wdw_ref[tap:tap + 1, lanes] * buf_ref[c, lo:lo + rows, :]
    return acc


def _prompt_kernel(lamv_ref, x_ref, g0_ref, win_ref, bias_ref, subg_ref, lng_ref, lnb_ref, wm_ref, bs_ref,
                   wout_ref, g1_ref, win1_ref, wdw_ref, bdw_ref, clng_ref, clnb_ref, wout1_ref, fg_ref,
                   y_ref, k_ref, v_ref, tail_ref,
                   z_ref, kbf_ref, vt_ref, qm_ref, m_ref, acc_ref, s_ref, mix_ref,
                   buf_ref, conv_ref, gate_ref, y0_ref, y0p_ref, cn_ref, *, nt, ntiles):
    step = pl.program_id(0)
    i = jnp.minimum(step, ntiles - 1) % nt
    d = x_ref.shape[-1]
    ngroups = d // LANE

    @pl.when(step == 0)
    def _():
        buf_ref[...] = jnp.zeros(buf_ref.shape, F32)
        gate_ref[...] = jnp.zeros(gate_ref.shape, F32)
        y0p_ref[...] = jnp.zeros(y0p_ref.shape, F32)

    x = x_ref[0]
    xn = _rms(x, g0_ref[...]).astype(BF16)
    blocks = [(c, r) for c in range(ngroups) for r in range(TM // CONV_RB)]
    nchunks = L0_IN // PROJ_COLS
    for j in range(nchunks):
        cols = slice(j * PROJ_COLS, (j + 1) * PROJ_COLS)
        z_ref[:, cols] = _dot(xn, _proj0_rhs(win_ref, j))
        for c, r in blocks[j * len(blocks) // nchunks:(j + 1) * len(blocks) // nchunks]:
            conv_ref[r * CONV_RB:(r + 1) * CONV_RB, c * LANE:(c + 1) * LANE] = _conv_rows(
                buf_ref, c, r * CONV_RB, CONV_RB, wdw_ref, bdw_ref)

    for g in range(GROUPS):
        sl = slice(g * GROUP_DIM, (g + 1) * GROUP_DIM)
        ub, vb, gb = _gmlp_cols(z_ref, g)
        zv = _ln(jax.nn.gelu(vb), lng_ref[:, sl], lnb_ref[:, sl]).astype(BF16)
        zu = jax.nn.gelu(ub)
        gate = jax.nn.silu(gb)
        for c in range(TM // B_CHUNK):
            rows = slice(c * B_CHUNK, (c + 1) * B_CHUNK)
            sg = _dot(wm_ref[g], zv[rows]) + bs_ref[:, sl]
            mix_ref[rows, A_WIDTH + g * GROUP_DIM:A_WIDTH + (g + 1) * GROUP_DIM] = (
                zu[rows] * sg * gate[rows]).astype(BF16)

    k = z_ref[:, P_K0:P_K0 + A_WIDTH]
    v = z_ref[:, P_V0:P_V0 + A_WIDTH]
    k_ref[0] = k.reshape(TM, HEADS, HEAD_DIM)
    v_ref[0] = v.reshape(TM, HEADS, HEAD_DIM)
    kbf_ref[i] = k.astype(BF16)
    for h in range(HEADS):
        vt_ref[i, h, 0:HEAD_DIM, :] = v[:, h * HEAD_DIM:(h + 1) * HEAD_DIM].T.astype(BF16)
        vt_ref[i, h, HEAD_DIM:VT_ROWS, :] = jnp.ones((VT_ROWS - HEAD_DIM, TM), BF16)

    lo = lax.broadcasted_iota(jnp.int32, (TM, HEAD_DIM), 1) < HALF
    for h in range(HEADS):
        q = z_ref[:, P_Q0 + h * HEAD_DIM:P_Q0 + (h + 1) * HEAD_DIM] * Q_SCALE_LOG2
        qm_ref[2 * h] = jnp.where(lo, q, 0.0).astype(BF16)
        qm_ref[2 * h + 1] = jnp.where(lo, 0.0, q).astype(BF16)

    units = range(2 * HEADS)

    def attend(tiles, first):
        m_new, alpha = {}, {}

        def scores(u):
            h = u // 2
            mt = None
            for t, (j, bias_idx) in enumerate(tiles):
                s = _dot_nt(kbf_ref[j, :, h * HEAD_DIM:(h + 1) * HEAD_DIM], qm_ref[u])
                if bias_idx is not None:
                    s = s + bias_ref[h, bias_idx]
                s_ref[t, u] = s
                smax = jnp.max(s, axis=0, keepdims=True)
                mt = smax if mt is None else jnp.maximum(mt, smax)
            if first:
                m_new[u] = mt
            else:
                m_old = m_ref[u]
                m_new[u] = jnp.maximum(m_old, mt)
                alpha[u] = jnp.exp2(m_old - m_new[u])
            m_ref[u] = m_new[u]

        def values(u):
            pv = None
            for t in range(len(tiles)):
                p = jnp.exp2(s_ref[t, u] - m_new[u]).astype(BF16)
                part = _dot(vt_ref[tiles[t][0], u // 2], p)
                pv = part if pv is None else pv + part
            acc_ref[u] = pv if first else alpha[u] * acc_ref[u] + pv

        n = len(units)
        lag = PHASE_LAG if len(tiles) > 1 else n
        for k in range(n + lag):
            if k < n:
                scores(k)
            if 0 <= k - lag < n:
                values(k - lag)

    @pl.when(i == 0)
    def _():
        attend([(i, 0)], True)

    @pl.when(i == 1)
    def _():
        attend([(i, 0), (i - 1, 1)], True)

    @pl.when(jnp.logical_and(i > 1, i % 2 == 0))
    def _():
        attend([(i, 0), (i - 1, 1), (i - 2, None)], True)

    @pl.when(jnp.logical_and(i > 1, i % 2 == 1))
    def _():
        attend([(i, 0), (i - 1, 1), (i - 2, None), (i - 3, None)], True)

    n_left = jnp.maximum(i - 2 - i % 2, 0)

    def far_quad(jj, carry):
        attend([(4 * jj + t, None) for t in range(4)], False)
        return carry

    lax.fori_loop(0, n_left // 4, far_quad, 0)

    @pl.when(n_left % 4 == 2)
    def _():
        attend([(n_left - 2, None), (n_left - 1, None)], False)

    y0_ref[...] = x + _dot(mix_ref[:, A_WIDTH:], wout_ref[A_WIDTH:, :])

    lam = _lam(lamv_ref[...])
    for h in range(HEADS):
        on1 = acc_ref[2 * h, 0:HEAD_DIM, :] / acc_ref[2 * h, HEAD_DIM:HEAD_DIM + 1, :]
        on2 = acc_ref[2 * h + 1, 0:HEAD_DIM, :] / acc_ref[2 * h + 1, HEAD_DIM:HEAD_DIM + 1, :]
        o = (on1 - lam * on2).T
        o = _rms(o, subg_ref[...]) * (1.0 - LAMBDA_INIT)
        ga = z_ref[:, P_GA0 + h * HEAD_DIM:P_GA0 + (h + 1) * HEAD_DIM]
        mix_ref[:, h * HEAD_DIM:(h + 1) * HEAD_DIM] = (o * jax.nn.silu(ga)).astype(BF16)

    y0 = y0_ref[...] + _dot(mix_ref[:, 0:A_WIDTH], wout_ref[0:A_WIDTH, :])
    y0_ref[...] = y0

    xn1 = _rms(y0, g1_ref[...]).astype(BF16)
    starts_stream = i == 0
    nrow = TM // LN_ROWS
    n_glu = 2 * d // PROJ_COLS
    for j in range(3 * d // PROJ_COLS):
        cols = slice(j * PROJ_COLS, (j + 1) * PROJ_COLS)
        src = _proj1_src(j, d)
        z_ref[:, cols] = _dot(xn1, win1_ref[:, src:src + PROJ_COLS])
        if j < nrow:
            rows = slice(j * LN_ROWS, (j + 1) * LN_ROWS)
            cn_ref[rows, :] = (jax.nn.silu(_ln(conv_ref[rows, :], clng_ref[...], clnb_ref[...]))
                               * gate_ref[rows, :]).astype(BF16)
        if j == nrow:
            y_ref[0] = _rms(y0p_ref[...] + _dot(cn_ref[...], wout1_ref[...]), fg_ref[...])
        if j < n_glu and j % 2 == 1:
            for c in (j - 1, j):
                a0 = (j - 1) * PROJ_COLS + (c % 2) * LANE
                tail = buf_ref[c, TM:TM + CONV_PAD, :]
                tail_ref[0, :, c * LANE:(c + 1) * LANE] = tail
                buf_ref[c, 0:CONV_PAD, :] = jnp.where(starts_stream, 0.0, tail)
                buf_ref[c, CONV_PAD:CONV_PAD + TM, :] = (z_ref[:, a0:a0 + LANE]
                                                         * jax.nn.sigmoid(z_ref[:, a0 + PROJ_COLS:a0 + PROJ_COLS + LANE]))
        if j >= n_glu:
            gcols = slice((j - n_glu) * PROJ_COLS, (j - n_glu + 1) * PROJ_COLS)
            gate_ref[:, gcols] = jax.nn.silu(z_ref[:, cols])
    y0p_ref[...] = y0


def _prompt(x, lamv, g0, win, bias, subg, lng, lnb, wm, bs, wout, g1, win1, wdw, bdw, clng, clnb, wout1, fg):
    bsz, t, d = x.shape
    nt = t // TM
    ntiles = bsz * nt
    const = lambda shape: pl.BlockSpec(shape, lambda s: (0,) * len(shape))
    cur = lambda s: jnp.minimum(s, ntiles - 1)
    done = lambda s: jnp.maximum(s - 1, 0)
    kv_row = pl.BlockSpec((1, TM, HEADS, HEAD_DIM), lambda s: (cur(s) // nt, cur(s) % nt, 0, 0))
    return pl.pallas_call(
        functools.partial(_prompt_kernel, nt=nt, ntiles=ntiles),
        grid=(ntiles + 1,),
        in_specs=[const((4, HALF)), pl.BlockSpec((1, TM, d), lambda s: (cur(s) // nt, cur(s) % nt, 0)),
                  const((1, d)), const((d, L0_IN)),
                  const((HEADS, 2, TM, TM)), const((1, HEAD_DIM)), const((1, B_WIDTH)), const((1, B_WIDTH)),
                  const((GROUPS, B_CHUNK, B_CHUNK)), const((B_CHUNK, B_WIDTH)), const((A_WIDTH + B_WIDTH, d)),
                  const((1, d)), const((d, 3 * d)), const((CONV_WIDTH, d)), const((1, d)),
                  const((1, d)), const((1, d)), const((d, d)), const((1, d))],
        out_specs=[pl.BlockSpec((1, TM, d), lambda s: (done(s) // nt, done(s) % nt, 0)), kv_row, kv_row,
                   pl.BlockSpec((1, CONV_PAD, d), lambda s: (done(s) // nt, 0, 0))],
        out_shape=[jax.ShapeDtypeStruct((bsz, t, d), F32),
                   jax.ShapeDtypeStruct((bsz, t, HEADS, HEAD_DIM), F32),
                   jax.ShapeDtypeStruct((bsz, t, HEADS, HEAD_DIM), F32),
                   jax.ShapeDtypeStruct((bsz, CONV_PAD, d), F32)],
        scratch_shapes=[pltpu.VMEM((TM, L0_IN), F32),
                        pltpu.VMEM((nt, TM, A_WIDTH), BF16),
                        pltpu.VMEM((nt, HEADS, VT_ROWS, TM), BF16),
                        pltpu.VMEM((2 * HEADS, TM, HEAD_DIM), BF16),
                        pltpu.VMEM((2 * HEADS, 1, TM), F32),
                        pltpu.VMEM((2 * HEADS, VT_ROWS, TM), F32),
                        pltpu.VMEM((4, 2 * HEADS, TM, TM), F32),
                        pltpu.VMEM((TM, A_WIDTH + B_WIDTH), BF16),
                        pltpu.VMEM((d // LANE, CONV_PAD + TM, LANE), F32),
                        pltpu.VMEM((TM, d), F32),
                        pltpu.VMEM((TM, d), F32),
                        pltpu.VMEM((TM, d), F32),
                        pltpu.VMEM((TM, d), F32),
                        pltpu.VMEM((TM, d), BF16)],
        compiler_params=pltpu.CompilerParams(dimension_semantics=("arbitrary",), vmem_limit_bytes=VMEM_LIMIT),
        name="prompt",
    )(lamv, x, g0, win, bias, subg, lng, lnb, wm, bs, wout, g1, win1, wdw, bdw, clng, clnb, wout1, fg)


def _s_proj_kernel(x_ref, g0_ref, win_ref, z_ref):
    xn = _rms(x_ref[...], g0_ref[...]).astype(BF16)
    for j in range(L0_IN // PROJ_COLS):
        z_ref[:, j * PROJ_COLS:(j + 1) * PROJ_COLS] = _dot(xn, _proj0_rhs(win_ref, j))


def _s_attn_kernel(lamv_ref, z_ref, ck_ref, cv_ref, bias_ref, o_ref, *, dec_seq, pad_rows):
    ncol = HEADS * 2 * dec_seq
    q = z_ref[:, P_Q0:P_Q0 + A_WIDTH] * (HALF ** -0.5)
    q_rep = jnp.concatenate([q] * (2 * HEADS), axis=0)
    row = lax.broadcasted_iota(jnp.int32, (ncol, A_WIDTH), 0)
    col = lax.broadcasted_iota(jnp.int32, (ncol, A_WIDTH), 1)
    q_bd = jnp.where(col // HALF == row // dec_seq, q_rep, 0.0).astype(BF16)

    zeros = jnp.zeros((pad_rows - dec_seq, A_WIDTH), F32)
    ck = ck_ref[0].reshape(ck_ref.shape[1], A_WIDTH)
    cv = cv_ref[0].reshape(cv_ref.shape[1], A_WIDTH)
    k_all = jnp.concatenate([ck, z_ref[:, P_K0:P_K0 + A_WIDTH], zeros], axis=0).astype(BF16)
    v_all = jnp.concatenate([cv, z_ref[:, P_V0:P_V0 + A_WIDTH], zeros], axis=0).astype(BF16)

    s = _dot_nt(k_all, q_bd) + bias_ref[...]
    p = jnp.exp(s - jnp.max(s, axis=0, keepdims=True))
    p = p / jnp.sum(p, axis=0, keepdims=True)
    w = p - _lam(lamv_ref[...]) * pltpu.roll(p, ncol - dec_seq, axis=1)
    o_all = _dot(w.T.astype(BF16), v_all)
    for h in range(HEADS):
        o_ref[:, h * HEAD_DIM:(h + 1) * HEAD_DIM] = o_all[2 * h * dec_seq:(2 * h + 1) * dec_seq,
                                                          h * HEAD_DIM:(h + 1) * HEAD_DIM]


def _s_rest_kernel(x_ref, z_ref, o_ref, subg_ref, lng_ref, lnb_ref, wbd_ref, bs_ref, wout0_ref,
                   g1_ref, win1_ref, st_ref, wdw_ref, bdw_ref, clng_ref, clnb_ref, wout1_ref, fg_ref,
                   y_ref, zv_ref, u_ref, mix_ref, buf_ref, conv_ref, *, dec_seq):
    d = x_ref.shape[-1]
    nb = st_ref.shape[0]
    for h in range(HEADS):
        sl = slice(h * HEAD_DIM, (h + 1) * HEAD_DIM)
        o = _rms(o_ref[:, sl], subg_ref[...]) * (1.0 - LAMBDA_INIT)
        mix_ref[:, sl] = (o * jax.nn.silu(z_ref[:, P_GA0 + h * HEAD_DIM:P_GA0 + (h + 1) * HEAD_DIM])).astype(BF16)
    for g in range(GROUPS):
        sl = slice(g * GROUP_DIM, (g + 1) * GROUP_DIM)
        ub, vb, gb = _gmlp_cols(z_ref, g)
        zv = _ln(jax.nn.gelu(vb), lng_ref[:, sl], lnb_ref[:, sl])
        zv_ref[:, sl] = zv
        sg = _dot(wbd_ref[g], zv.astype(BF16)) + bs_ref[:, sl]
        mix_ref[:, A_WIDTH + g * GROUP_DIM:A_WIDTH + (g + 1) * GROUP_DIM] = (
            jax.nn.gelu(ub) * sg * jax.nn.silu(gb)).astype(BF16)
    y0 = x_ref[...] + _dot(mix_ref[...], wout0_ref[...])

    z1 = _dot(_rms(y0, g1_ref[...]).astype(BF16), win1_ref[...])
    u = z1[:, 0:d] * jax.nn.sigmoid(z1[:, d:2 * d])
    u_ref[...] = u
    ngroups = d // LANE
    for b in range(nb):
        for c in range(ngroups):
            lanes = slice(c * LANE, (c + 1) * LANE)
            buf_ref[b * ngroups + c, 0:CONV_PAD, :] = st_ref[b, :, lanes]
            buf_ref[b * ngroups + c, CONV_PAD:CONV_PAD + dec_seq, :] = u[b * dec_seq:(b + 1) * dec_seq, lanes]
    for b in range(nb):
        for c in range(ngroups):
            conv_ref[b * dec_seq:(b + 1) * dec_seq, c * LANE:(c + 1) * LANE] = _conv_rows(
                buf_ref.at[pl.ds(b * ngroups, ngroups)], c, 0, dec_seq, wdw_ref, bdw_ref)
    c = jax.nn.silu(_ln(conv_ref[...], clng_ref[...], clnb_ref[...])) * jax.nn.silu(z1[:, 2 * d:3 * d])
    y1 = y0 + _dot(c.astype(BF16), wout1_ref[...])
    y_ref[...] = _rms(y1, fg_ref[...])


def _sample(xs, cache_k, cache_v, state, lamv, g0, win0, bias_s, subg, lng, lnb, wbd, bs_s, wout0,
            g1, win1, wdw, bdw, clng, clnb, wout1, fg):
    nb, dec_seq, d = xs.shape
    rows = nb * dec_seq
    past = cache_k.shape[1]
    pad_rows = bias_s.shape[0] - past
    x2 = xs.reshape(rows, d)
    z = pl.pallas_call(_s_proj_kernel, out_shape=jax.ShapeDtypeStruct((rows, L0_IN), F32),
                       compiler_params=pltpu.CompilerParams(vmem_limit_bytes=VMEM_LIMIT),
                       name="s_proj")(x2, g0, win0)

    ncol = HEADS * 2 * dec_seq
    o = pl.pallas_call(
        functools.partial(_s_attn_kernel, dec_seq=dec_seq, pad_rows=pad_rows),
        grid=(nb,),
        in_specs=[pl.BlockSpec((4, HALF), lambda b: (0, 0)),
                  pl.BlockSpec((dec_seq, L0_IN), lambda b: (b, 0)),
                  pl.BlockSpec((1, past, HEADS, HEAD_DIM), lambda b: (b, 0, 0, 0)),
                  pl.BlockSpec((1, past, HEADS, HEAD_DIM), lambda b: (b, 0, 0, 0)),
                  pl.BlockSpec((past + pad_rows, ncol), lambda b: (0, 0))],
        out_specs=pl.BlockSpec((dec_seq, A_WIDTH), lambda b: (b, 0)),
        out_shape=jax.ShapeDtypeStruct((rows, A_WIDTH), F32),
        compiler_params=pltpu.CompilerParams(dimension_semantics=("arbitrary",), vmem_limit_bytes=VMEM_LIMIT),
        name="s_attn",
    )(lamv, z, cache_k, cache_v, bias_s)

    state_pad = jnp.pad(state, ((0, 0), (CONV_OFF, 0), (0, 0)))
    y, zv, u = pl.pallas_call(
        functools.partial(_s_rest_kernel, dec_seq=dec_seq),
        out_shape=[jax.ShapeDtypeStruct((rows, d), F32),
                   jax.ShapeDtypeStruct((rows, B_WIDTH), F32),
                   jax.ShapeDtypeStruct((rows, d), F32)],
        scratch_shapes=[pltpu.VMEM((rows, A_WIDTH + B_WIDTH), BF16),
                        pltpu.VMEM((nb * (d // LANE), CONV_PAD + dec_seq, LANE), F32),
                        pltpu.VMEM((rows, d), F32)],
        compiler_params=pltpu.CompilerParams(vmem_limit_bytes=VMEM_LIMIT),
        name="s_rest",
    )(x2, z, o, subg, lng, lnb, wbd, bs_s, wout0, g1, win1, state_pad, wdw, bdw, clng, clnb, wout1, fg)
    return z, y, zv, u


def kernel(x_prompt, x_sample, cache_k0, cache_v0, state_conv1, rel_bias, norm_g0, w_in0, lambda_q1, lambda_k1,
           lambda_q2, lambda_k2, subln_g0, gv_ln_g0, gv_ln_b0, w_s0, b_s0, w_out0, norm_g1, w_in1, w_dw1, b_dw1,
           conv_ln_g1, conv_ln_b1, w_out1, final_g):
    bsz, t, d = x_prompt.shape
    nb, dec_seq, _ = x_sample.shape
    past = cache_k0.shape[1]
    assert t % TM == 0 and TM % B_CHUNK == 0 and TM % CHUNK == 0 and TM >= MAX_DISTANCE
    assert d % (2 * PROJ_COLS) == 0 and TM // LN_ROWS < 3 * d // PROJ_COLS
    assert dec_seq <= B_CHUNK and dec_seq % 8 == 0 and HEADS * 2 * dec_seq == LANE
    assert past % CHUNK == 0 and (past + dec_seq - 1) // CHUNK == past // CHUNK

    row = lambda a: a.reshape(1, -1).astype(F32)
    lamv = jnp.stack([lambda_q1, lambda_k1, lambda_q2, lambda_k2]).astype(F32)
    wout0, wout1 = w_out0.astype(BF16), w_out1.astype(BF16)
    win0, win1 = w_in0.astype(BF16), w_in1.astype(BF16)
    g0, g1, fg, subg = row(norm_g0), row(norm_g1), row(final_g), row(subln_g0)
    lng, lnb = row(gv_ln_g0), row(gv_ln_b0)
    bdw, clng, clnb = row(b_dw1), row(conv_ln_g1), row(conv_ln_b1)
    wdw = w_dw1.astype(F32)

    rbs = (rel_bias - rel_bias[FAR_BUCKET:FAR_BUCKET + 1]).astype(F32)
    assert np.all(_np_bucket(-np.arange(TM + 1, 2 * max(t, past + dec_seq))) == FAR_BUCKET)
    kk = np.arange(TM)[:, None]
    qq = np.arange(TM)[None, :]
    own = jnp.where((kk // CHUNK <= qq // CHUNK)[..., None], _lookup(rbs, _np_bucket(kk - qq)), NEG)
    prev = _lookup(rbs, _np_bucket(kk - qq - TM))
    bias_p = jnp.transpose(jnp.stack([own, prev]), (3, 0, 1, 2)) * LOG2E

    tril = np.tril(np.ones((B_CHUNK, B_CHUNK), dtype=bool))
    wm = jnp.where(tril[None], w_s0, 0)
    bs = jnp.repeat(b_s0.T.astype(F32), GROUP_DIM, axis=1)

    y_prompt, k0p, v0p, tail_p = _prompt(x_prompt, lamv, g0, win0, bias_p, subg, lng, lnb, wm.astype(BF16), bs, wout0,
                                         g1, win1, wdw, bdw, clng, clnb, wout1, fg)

    pad_rows = LANE
    kpos = np.arange(past + dec_seq)[:, None]
    qpos = past + np.arange(dec_seq)[None, :]
    bsmp = jnp.transpose(_lookup(rbs, _np_bucket(kpos - qpos)), (0, 2, 1))
    bsmp = jnp.broadcast_to(bsmp[:, :, None, :], (past + dec_seq, HEADS, 2, dec_seq)).reshape(past + dec_seq, -1)
    bias_s = jnp.pad(bsmp, ((0, pad_rows - dec_seq), (0, 0)), constant_values=NEG)
    wbd = jnp.stack([jnp.kron(jnp.eye(nb, dtype=F32), wm[g, :dec_seq, :dec_seq]) for g in range(GROUPS)])
    bs_s = jnp.tile(bs[:dec_seq], (nb, 1))

    zs, ys, zvs, us = _sample(x_sample, cache_k0, cache_v0, state_conv1, lamv, g0, win0, bias_s, subg, lng, lnb,
                              wbd.astype(BF16), bs_s, wout0, g1, win1, wdw, bdw, clng, clnb, wout1, fg)

    keep = CONV_WIDTH - 1
    return (y_prompt,
            ys.reshape(nb, dec_seq, d),
            k0p,
            v0p,
            tail_p[:, CONV_OFF:],
            zs[:, P_K0:P_K0 + A_WIDTH].reshape(nb, dec_seq, HEADS, HEAD_DIM),
            zs[:, P_V0:P_V0 + A_WIDTH].reshape(nb, dec_seq, HEADS, HEAD_DIM),
            zvs.reshape(nb, dec_seq, B_WIDTH),
            jnp.concatenate([state_conv1, us.reshape(nb, dec_seq, d)], axis=1)[:, -keep:])
```
